```python
import jax, jax.numpy as jnp
from jax import lax
import numpy as np

D_MODEL = 2048
BATCH = 4
SEQ = 4096
DEPTH = 1

D_MIX = D_MODEL
D_RNN = D_MIX // 2
RNN_BLOCKS = 16
RNN_BLOCK = D_RNN // RNN_BLOCKS
CONV_WIDTH = 4
LRU_C = 8.0
MLA_HEADS = 8
QK_NOPE = 128
QK_ROPE = 64
V_HEAD = 128
D_ATT = MLA_HEADS * V_HEAD
Q_LORA = 512
KV_LORA = 512
ROPE_THETA = 10000.0
Q_BLOCK = 128
EPS = 1e-6
IN_SPLITS = (D_RNN, D_RNN, Q_LORA, KV_LORA, QK_ROPE, D_ATT)
D_IN = D_RNN + D_RNN + Q_LORA + KV_LORA + QK_ROPE + D_ATT
ADA_SCALE = 0.5

kernel_name = "hymba_rglru_mla_adaln_layer"


def rms_norm(x, gain=None):
    xf = x.astype(jnp.float32)
    y = xf * lax.rsqrt(jnp.mean(xf * xf, axis=-1, keepdims=True) + EPS)
    if gain is not None:
        y = y * gain.astype(jnp.float32)
    return y.astype(x.dtype)


def apply_rope(x, cos, sin):
    half = x.shape[-1] // 2
    x1, x2 = x[..., :half], x[..., half:]
    return jnp.concatenate([x1 * cos - x2 * sin, x1 * sin + x2 * cos], axis=-1).astype(x.dtype)


def split_points(sizes):
    pts, acc = [], 0
    for s in sizes[:-1]:
        acc += s
        pts.append(acc)
    return pts


def _lin_rec_combine(e1, e2):
    a1, b1 = e1
    a2, b2 = e2
    return a1 * a2, a2 * b1 + b2


def rg_lru_branch(xr, conv_w, conv_b, w_rg_a, b_rg_a, w_rg_x, b_rg_x, lru_lambda):
    B, S, C = xr.shape
    xc = lax.conv_general_dilated(
        xr, conv_w[:, None, :].astype(xr.dtype), window_strides=(1,),
        padding=[(CONV_WIDTH - 1, 0)], dimension_numbers=('NWC', 'WIO', 'NWC'),
        feature_group_count=C) + conv_b
    xb = xc.reshape(B, S, RNN_BLOCKS, RNN_BLOCK)
    r = jax.nn.sigmoid(jnp.einsum('bshi,hij->bshj', xb, w_rg_a).reshape(B, S, C) + b_rg_a)
    i = jax.nn.sigmoid(jnp.einsum('bshi,hij->bshj', xb, w_rg_x).reshape(B, S, C) + b_rg_x)
    log_a = -LRU_C * r.astype(jnp.float32) * jax.nn.softplus(-lru_lambda.astype(jnp.float32))
    a = jnp.exp(log_a)
    b = jnp.sqrt(-jnp.expm1(2.0 * log_a)) * (i * xc).astype(jnp.float32)
    _, h = lax.associative_scan(_lin_rec_combine, (a, b), axis=1)
    return h.astype(xr.dtype)


def mla_branch(q_comp, kv_comp, k_rope_raw, positions, q_a_norm, w_uq, kv_a_norm, w_ukv,
               q_norm_nope, q_norm_rope, k_norm_nope, k_norm_rope):
    B, S, _ = q_comp.shape
    q = (rms_norm(q_comp, q_a_norm) @ w_uq).reshape(B, S, MLA_HEADS, QK_NOPE + QK_ROPE)
    kv = (rms_norm(kv_comp, kv_a_norm) @ w_ukv).reshape(B, S, MLA_HEADS, QK_NOPE + V_HEAD)
    q_nope, q_pe = q[..., :QK_NOPE], q[..., QK_NOPE:]
    k_nope, v = kv[..., :QK_NOPE], kv[..., QK_NOPE:]

    inv_freq = 1.0 / (ROPE_THETA ** (jnp.arange(0, QK_ROPE, 2, dtype=jnp.float32) / QK_ROPE))
    ang = positions.astype(jnp.float32)[..., None] * inv_freq
    cos, sin = jnp.cos(ang), jnp.sin(ang)

    q_nope = rms_norm(q_nope, q_norm_nope)
    q_pe = apply_rope(rms_norm(q_pe, q_norm_rope), cos[:, :, None], sin[:, :, None])
    k_nope = rms_norm(k_nope, k_norm_nope)
    k_pe = apply_rope(rms_norm(k_rope_raw, k_norm_rope), cos, sin)

    sm_scale = (QK_NOPE + QK_ROPE) ** -0.5
    n_blk = S // Q_BLOCK
    qn_b = q_nope.reshape(B, n_blk, Q_BLOCK, MLA_HEADS, QK_NOPE).transpose(1, 0, 2, 3, 4)
    qr_b = q_pe.reshape(B, n_blk, Q_BLOCK, MLA_HEADS, QK_ROPE).transpose(1, 0, 2, 3, 4)
    k_idx = jnp.arange(S)

    def attend_block(args):
        qn, qr, blk = args
        s = (jnp.einsum('bqhd,bkhd->bhqk', qn, k_nope)
             + jnp.einsum('bqhr,bkr->bhqk', qr, k_pe)).astype(jnp.float32) * sm_scale
        q_idx = blk * Q_BLOCK + jnp.arange(Q_BLOCK)
        s = jnp.where(q_idx[:, None] >= k_idx[None, :], s, -jnp.inf)
        p = jax.nn.softmax(s, axis=-1).astype(v.dtype)
        return jnp.einsum('bhqk,bkhd->bqhd', p, v)

    o = lax.map(attend_block, (qn_b, qr_b, jnp.arange(n_blk)))
    return o.transpose(1, 0, 2, 3, 4).reshape(B, S, D_ATT)


def setup_inputs(seed: int = 0) -> dict:
    key = jax.random.key(seed)
    ks = jax.random.split(key, 24)
    f32 = jnp.float32
    nrm = lambda k, shape, s: jax.random.normal(k, shape, f32) * s
    gain = lambda k, n: 1.0 + 0.02 * jax.random.normal(k, (DEPTH, n), f32)

    x = jax.random.normal(ks[0], (BATCH, SEQ, D_MODEL), f32)
    c = jax.random.normal(ks[1], (BATCH, D_MODEL), f32)
    offsets = jax.random.randint(ks[2], (BATCH, 1), 0, 1024, dtype=jnp.int32)
    positions = offsets + jnp.arange(SEQ, dtype=jnp.int32)[None, :]

    u = jax.random.uniform(ks[3], (DEPTH, D_RNN), f32, 0.9, 0.999)
    sl = u ** (1.0 / LRU_C)
    lru_lambda = jnp.log(sl) - jnp.log1p(-sl)

    return {
        "x": x,
        "c": c,
        "positions": positions,
        "w_ada": nrm(ks[4], (DEPTH, D_MODEL, 3 * D_MODEL), ADA_SCALE * D_MODEL ** -0.5),
        "b_ada": nrm(ks[5], (DEPTH, 3 * D_MODEL), 0.01),
        "w_in": nrm(ks[6], (DEPTH, D_MODEL, D_IN), D_MODEL ** -0.5),
        "conv_w": nrm(ks[7], (DEPTH, CONV_WIDTH, D_RNN), CONV_WIDTH ** -0.5),
        "conv_b": nrm(ks[8], (DEPTH, D_RNN), 0.01),
        "w_rg_a": nrm(ks[9], (DEPTH, RNN_BLOCKS, RNN_BLOCK, RNN_BLOCK), RNN_BLOCK ** -0.5),
        "b_rg_a": nrm(ks[10], (DEPTH, D_RNN), 0.01),
        "w_rg_x": nrm(ks[11], (DEPTH, RNN_BLOCKS, RNN_BLOCK, RNN_BLOCK), RNN_BLOCK ** -0.5),
        "b_rg_x": nrm(ks[12], (DEPTH, D_RNN), 0.01),
        "lru_lambda": lru_lambda,
        "q_a_norm": gain(ks[13], Q_LORA),
        "w_uq": nrm(ks[14], (DEPTH, Q_LORA, MLA_HEADS * (QK_NOPE + QK_ROPE)), Q_LORA ** -0.5),
        "kv_a_norm": gain(ks[15], KV_LORA),
        "w_ukv": nrm(ks[16], (DEPTH, KV_LORA, MLA_HEADS * (QK_NOPE + V_HEAD)), KV_LORA ** -0.5),
        "q_norm_nope": gain(ks[17], QK_NOPE),
        "q_norm_rope": gain(ks[18], QK_ROPE),
        "k_norm_nope": gain(ks[19], QK_NOPE),
        "k_norm_rope": gain(ks[20], QK_ROPE),
        "w_out": nrm(ks[21], (DEPTH, D_MIX, D_MODEL), D_MIX ** -0.5),
    }


def reference(x, c, positions, w_ada, b_ada, w_in, conv_w, conv_b, w_rg_a, b_rg_a, w_rg_x,
              b_rg_x, lru_lambda, q_a_norm, w_uq, kv_a_norm, w_ukv, q_norm_nope, q_norm_rope,
              k_norm_nope, k_norm_rope, w_out):
    c_act = jax.nn.silu(c)
    for l in range(DEPTH):
        mod = c_act @ w_ada[l] + b_ada[l]
        shift, scale, gate = jnp.split(mod, 3, axis=-1)
        h = rms_norm(x) * (1.0 + scale[:, None, :]) + shift[:, None, :]

        proj = h @ w_in[l]
        xr, gr, qc, kvc, kr, ga = jnp.split(proj, split_points(IN_SPLITS), axis=-1)

        y_rnn = rg_lru_branch(xr, conv_w[l], conv_b[l], w_rg_a[l], b_rg_a[l], w_rg_x[l],
                              b_rg_x[l], lru_lambda[l]) * jax.nn.silu(gr)
        y_att = mla_branch(qc, kvc, kr, positions, q_a_norm[l], w_uq[l], kv_a_norm[l], w_ukv[l],
                           q_norm_nope[l], q_norm_rope[l], k_norm_nope[l],
                           k_norm_rope[l]) * jax.nn.silu(ga)

        y = jnp.concatenate([y_rnn, y_att], axis=-1) @ w_out[l]
        x = x + gate[:, None, :] * y
    return x
```

```python
import functools
import math

import jax
import jax.numpy as jnp
from jax import lax
from jax.experimental import pallas as pl
from jax.experimental.pallas import tpu as pltpu

RNN_BLOCKS = 16
CONV_WIDTH = 4
LRU_C = 8.0
MLA_HEADS = 8
QK_NOPE = 128
QK_ROPE = 64
V_HEAD = 128
ROPE_THETA = 10000.0
EPS = 1e-6

QK_PAD = 256
MXU_DIM = 256
VMEM_LIMIT = 56 * 1024 * 1024

F32 = jnp.float32
BF16 = jnp.bfloat16

NT_DIMS = (((1,), (1,)), ((), ()))


def _const_spec(shape):
    nd = len(shape)
    return pl.BlockSpec(shape, lambda *_: (0,) * nd, pipeline_mode=pl.Buffered(1))


def _adaln_kernel(c_ref, w_ref, b_ref, o_ref):
    c = c_ref[...]
    c_act = c * jax.nn.sigmoid(c)
    o_ref[...] = jnp.dot(c_act, w_ref[...], preferred_element_type=F32,
                         precision=lax.Precision.HIGHEST) + b_ref[...]


def _adaln(c_pad, w_ada, b_ada):
    rows, d = c_pad.shape
    n = w_ada.shape[1]
    tn = 1536
    return pl.pallas_call(
        _adaln_kernel,
        grid=(n // tn,),
        in_specs=[
            pl.BlockSpec((rows, d), lambda j: (0, 0)),
            pl.BlockSpec((d, tn), lambda j: (0, j)),
            pl.BlockSpec((1, tn), lambda j: (0, j)),
        ],
        out_specs=pl.BlockSpec((rows, tn), lambda j: (0, j)),
        out_shape=jax.ShapeDtypeStruct((rows, n), F32),
        compiler_params=pltpu.CompilerParams(
            dimension_semantics=("arbitrary",), vmem_limit_bytes=VMEM_LIMIT),
        name="adaln_mod",
    )(c_pad, w_ada, b_ada)


def _rms(v, axis):
    return v * lax.rsqrt(jnp.mean(v * v, axis=axis, keepdims=True) + EPS)


def _in_proj_kernel(x_ref, mod_ref, posr_ref, posc_ref, invr_ref, invc_ref,
                    wxr_ref, wgr_ref, wqc_ref, wkvc_ref, wkr_ref, wga_ref,
                    qan_ref, kvan_ref, wuqt_ref, wuk_ref, wuvt_ref,
                    gqn_ref, gqr_ref, gkn_ref, gkr_ref,
                    xr_ref, sgr_ref, sga_ref, qt_ref, k_ref, vt_ref, *, q_scale):
    x = x_ref[0]
    scale = mod_ref[0, 1:2, :]
    shift = mod_ref[0, 0:1, :]
    h = (_rms(x, -1) * (1.0 + scale) + shift).astype(BF16)

    def proj(w_ref):
        return jnp.dot(h, w_ref[...], preferred_element_type=F32)

    xr_ref[0] = proj(wxr_ref).astype(BF16)
    gr = proj(wgr_ref)
    sgr_ref[0] = (gr * jax.nn.sigmoid(gr)).astype(BF16)
    ga = proj(wga_ref)
    sga_ref[0] = (ga * jax.nn.sigmoid(ga)).astype(BF16)

    ang = posc_ref[0] * invr_ref[...]
    cos, sin = jnp.cos(ang), jnp.sin(ang)
    ang_t = invc_ref[...] * posr_ref[0]
    cos_t, sin_t = jnp.cos(ang_t), jnp.sin(ang_t)
    half = QK_ROPE // 2

    qcn = (_rms(proj(wqc_ref), -1) * qan_ref[...]).astype(BF16)
    q_t = lax.dot_general(wuqt_ref[...], qcn, NT_DIMS, preferred_element_type=F32)
    hd = QK_NOPE + QK_ROPE
    zpad = jnp.zeros((QK_PAD - hd, q_t.shape[1]), BF16)
    for hh in range(MLA_HEADS):
        qn = q_t[hh * hd:hh * hd + QK_NOPE]
        qn = _rms(qn, 0) * (gqn_ref[...] * q_scale)
        qr = q_t[hh * hd + QK_NOPE:(hh + 1) * hd]
        qr = _rms(qr, 0) * (gqr_ref[...] * q_scale)
        x1, x2 = qr[:half], qr[half:]
        qt_ref[0, hh, 0:QK_NOPE, :] = qn.astype(BF16)
        qt_ref[0, hh, QK_NOPE:QK_NOPE + half, :] = (x1 * cos_t - x2 * sin_t).astype(BF16)
        qt_ref[0, hh, QK_NOPE + half:hd, :] = (x1 * sin_t + x2 * cos_t).astype(BF16)
        qt_ref[0, hh, hd:QK_PAD, :] = zpad

    kvcn = (_rms(proj(wkvc_ref), -1) * kvan_ref[...]).astype(BF16)
    v_t = lax.dot_general(wuvt_ref[...], kvcn, NT_DIMS, preferred_element_type=F32)
    for hh in range(MLA_HEADS):
        vt_ref[0, hh] = v_t[hh * V_HEAD:(hh + 1) * V_HEAD].astype(BF16)

    kr = proj(wkr_ref)[:, :QK_ROPE]
    kr = _rms(kr, -1) * gkr_ref[...]
    k1, k2 = kr[:, :half], kr[:, half:]
    kpe = jnp.concatenate(
        [k1 * cos - k2 * sin, k1 * sin + k2 * cos,
         jnp.zeros((kr.shape[0], QK_PAD - hd), F32)], axis=1).astype(BF16)
    kn_all = jnp.dot(kvcn, wuk_ref[...], preferred_element_type=F32)
    for hh in range(MLA_HEADS):
        kn = kn_all[:, hh * QK_NOPE:(hh + 1) * QK_NOPE]
        k_ref[0, hh, :, 0:QK_NOPE] = (_rms(kn, -1) * gkn_ref[...]).astype(BF16)
        k_ref[0, hh, :, QK_NOPE:QK_PAD] = kpe


def _in_proj(x, mod, pos_row, pos_col, inv_row, inv_col, w, tm):
    bsz, seq, d = x.shape
    hd = QK_NOPE + QK_ROPE
    q_scale = (hd ** -0.5) * math.log2(math.e)
    consts = [w["wxr"], w["wgr"], w["wqc"], w["wkvc"], w["wkr"], w["wga"],
              w["qan"], w["kvan"], w["wuqt"], w["wuk"], w["wuvt"],
              w["gqn"], w["gqr"], w["gkn"], w["gkr"]]
    d_rnn = w["wxr"].shape[1]
    d_att = w["wga"].shape[1]
    return pl.pallas_call(
        functools.partial(_in_proj_kernel, q_scale=q_scale),
        grid=(bsz, seq // tm),
        in_specs=[
            pl.BlockSpec((1, tm, d), lambda b, i: (b, i, 0)),
            pl.BlockSpec((1, 3, d), lambda b, i: (b, 0, 0)),
            pl.BlockSpec((1, 1, tm), lambda b, i: (b, 0, i)),
            pl.BlockSpec((1, tm, 1), lambda b, i: (b, i, 0)),
            _const_spec(inv_row.shape),
            _const_spec(inv_col.shape),
        ] + [_const_spec(a.shape) for a in consts],
        out_specs=[
            pl.BlockSpec((1, tm, d_rnn), lambda b, i: (b, i, 0)),
            pl.BlockSpec((1, tm, d_rnn), lambda b, i: (b, i, 0)),
            pl.BlockSpec((1, tm, d_att), lambda b, i: (b, i, 0)),
            pl.BlockSpec((1, MLA_HEADS, QK_PAD, tm), lambda b, i: (b, 0, 0, i)),
            pl.BlockSpec((1, MLA_HEADS, tm, QK_PAD), lambda b, i: (b, 0, i, 0)),
            pl.BlockSpec((1, MLA_HEADS, V_HEAD, tm), lambda b, i: (b, 0, 0, i)),
        ],
        out_shape=[
            jax.ShapeDtypeStruct((bsz, seq, d_rnn), BF16),
            jax.ShapeDtypeStruct((bsz, seq, d_rnn), BF16),
            jax.ShapeDtypeStruct((bsz, seq, d_att), BF16),
            jax.ShapeDtypeStruct((bsz, MLA_HEADS, QK_PAD, seq), BF16),
            jax.ShapeDtypeStruct((bsz, MLA_HEADS, seq, QK_PAD), BF16),
            jax.ShapeDtypeStruct((bsz, MLA_HEADS, V_HEAD, seq), BF16),
        ],
        compiler_params=pltpu.CompilerParams(
            dimension_semantics=("arbitrary", "arbitrary"), vmem_limit_bytes=VMEM_LIMIT),
        name="in_proj",
    )(x, mod, pos_row, pos_col, inv_row, inv_col, *consts)


def _attn_kernel(qt_ref, k_ref, vt_ref, sga_ref, o_ref, acc_ref, *, tq):
    qi = pl.program_id(2)
    q_t = qt_ref[0, 0]
    acc_ref[...] = jnp.zeros_like(acc_ref)

    def block(j, m, l, masked):
        start = pl.multiple_of(j * tq, tq)
        k_blk = k_ref[0, 0, pl.ds(start, tq), :]
        s_t = jnp.dot(k_blk, q_t, preferred_element_type=F32)
        if masked:
            key = lax.broadcasted_iota(jnp.int32, s_t.shape, 0)
            qry = lax.broadcasted_iota(jnp.int32, s_t.shape, 1)
            s_t = jnp.where(key <= qry, s_t, -jnp.inf)
        m_new = jnp.maximum(m, jnp.max(s_t, axis=0, keepdims=True))
        alpha = jnp.exp2(m - m_new)
        p_t = jnp.exp2(s_t - m_new)
        l_new = alpha * l + jnp.sum(p_t, axis=0, keepdims=True)
        v_blk = vt_ref[0, 0, :, pl.ds(start, tq)]
        pv = jnp.dot(v_blk, p_t.astype(BF16), preferred_element_type=F32)
        acc_ref[...] = alpha * acc_ref[...] + pv
        return m_new, l_new

    m0 = jnp.full((1, tq), -jnp.inf, F32)
    l0 = jnp.zeros((1, tq), F32)
    m, l = lax.fori_loop(0, qi, lambda j, c: block(j, c[0], c[1], False), (m0, l0))
    m, l = block(qi, m, l, True)

    o_t = acc_ref[...] / l
    o_ref[0] = (o_t.T * sga_ref[0].astype(F32)).astype(BF16)


def _attention(q_t, k, v_t, sga, tq):
    bsz, nh, _, seq = q_t.shape
    return pl.pallas_call(
        functools.partial(_attn_kernel, tq=tq),
        grid=(bsz, nh, seq // tq),
        in_specs=[
            pl.BlockSpec((1, 1, QK_PAD, tq), lambda b, h, i: (b, h, 0, i)),
            pl.BlockSpec((1, 1, seq, QK_PAD), lambda b, h, i: (b, h, 0, 0)),
            pl.BlockSpec((1, 1, V_HEAD, seq), lambda b, h, i: (b, h, 0, 0)),
            pl.BlockSpec((1, tq, V_HEAD), lambda b, h, i: (b, i, h)),
        ],
        out_specs=pl.BlockSpec((1, tq, V_HEAD), lambda b, h, i: (b, i, h)),
        out_shape=jax.ShapeDtypeStruct((bsz, seq, nh * V_HEAD), BF16),
        scratch_shapes=[pltpu.VMEM((V_HEAD, tq), F32)],
        compiler_params=pltpu.CompilerParams(
            dimension_semantics=("arbitrary", "arbitrary", "arbitrary"),
            vmem_limit_bytes=VMEM_LIMIT),
        name="mla_attention",
    )(q_t, k, v_t, sga)


SUBLANES = 8


def _rglru_kernel(xr_ref, sgr_ref, cw_ref, cb_ref, wa_ref, ba_ref, wx_ref, bx_ref, lam_ref,
                  o_ref, xbuf_ref, a_ref, b_ref, hc_ref, *, ts):
    @pl.when(pl.program_id(1) == 0)
    def _():
        xbuf_ref[0:SUBLANES, :] = jnp.zeros((SUBLANES, xbuf_ref.shape[1]), F32)
        hc_ref[...] = jnp.zeros_like(hc_ref)

    xbuf_ref[SUBLANES:SUBLANES + ts, :] = xr_ref[0].astype(F32)
    xc = cb_ref[...]
    for kk in range(CONV_WIDTH):
        off = SUBLANES - (CONV_WIDTH - 1) + kk
        xc = xc + cw_ref[kk:kk + 1, :] * xbuf_ref[off:off + ts, :]
    xbuf_ref[0:SUBLANES, :] = xbuf_ref[ts:ts + SUBLANES, :]

    xcb = xc.astype(BF16)
    n_grp = wa_ref.shape[0]

    def gate(w_ref, bias_ref):
        parts = [jnp.dot(xcb[:, g * MXU_DIM:(g + 1) * MXU_DIM], w_ref[g],
                         preferred_element_type=F32) for g in range(n_grp)]
        return jax.nn.sigmoid(jnp.concatenate(parts, axis=1) + bias_ref[...])

    r = gate(wa_ref, ba_ref)
    i = gate(wx_ref, bx_ref)
    nl = -lam_ref[...]
    softplus = jnp.maximum(nl, 0.0) + jnp.log1p(jnp.exp(-jnp.abs(nl)))
    log_a = (-LRU_C) * r * softplus
    a_ref[...] = jnp.exp(log_a)
    t = jnp.tanh(log_a)
    b_ref[...] = jnp.sqrt(-2.0 * t / (1.0 - t)) * (i * xc)

    row = lax.broadcasted_iota(jnp.int32, (SUBLANES, a_ref.shape[1]), 0)

    def scan_group(g, h_prev):
        base = pl.multiple_of(g * SUBLANES, SUBLANES)
        a = a_ref[pl.ds(base, SUBLANES), :]
        b = b_ref[pl.ds(base, SUBLANES), :]
        for dist in (1, 2, 4):
            keep = row >= dist
            b = jnp.where(keep, a * pltpu.roll(b, dist, 0) + b, b)
            a = jnp.where(keep, a * pltpu.roll(a, dist, 0), a)
        hg = a * h_prev + b
        b_ref[pl.ds(base, SUBLANES), :] = hg
        return jnp.broadcast_to(hg[SUBLANES - 1:SUBLANES, :], hg.shape)

    hc_ref[...] = lax.fori_loop(0, ts // SUBLANES, scan_group, hc_ref[...], unroll=4)
    o_ref[0] = (b_ref[...] * sgr_ref[0].astype(F32)).astype(BF16)


def _rglru(xr, sgr, cw, cb, wa, ba, wx, bx, lam, ts):
    bsz, seq, c = xr.shape
    consts = [cw, cb, wa, ba, wx, bx, lam]
    return pl.pallas_call(
        functools.partial(_rglru_kernel, ts=ts),
        grid=(bsz, seq // ts),
        in_specs=[
            pl.BlockSpec((1, ts, c), lambda b, i: (b, i, 0)),
            pl.BlockSpec((1, ts, c), lambda b, i: (b, i, 0)),
        ] + [_const_spec(a.shape) for a in consts],
        out_specs=pl.BlockSpec((1, ts, c), lambda b, i: (b, i, 0)),
        out_shape=jax.ShapeDtypeStruct((bsz, seq, c), BF16),
        scratch_shapes=[
            pltpu.VMEM((ts + SUBLANES, c), F32),
            pltpu.VMEM((ts, c), F32),
            pltpu.VMEM((ts, c), F32),
            pltpu.VMEM((SUBLANES, c), F32),
        ],
        compiler_params=pltpu.CompilerParams(
            dimension_semantics=("arbitrary", "arbitrary"), vmem_limit_bytes=VMEM_LIMIT),
        name="rglru",
    )(xr, sgr, *consts)


def _out_proj_kernel(x_ref, yr_ref, ya_ref, wr_ref, wa_ref, mod_ref, o_ref):
    y = jnp.dot(yr_ref[0], wr_ref[...], preferred_element_type=F32)
    y = y + jnp.dot(ya_ref[0], wa_ref[...], preferred_element_type=F32)
    o_ref[0] = x_ref[0] + mod_ref[0, 2:3, :] * y


def _out_proj(x, y_rnn, y_att, w_r, w_a, mod, tm):
    bsz, seq, d = x.shape
    return pl.pallas_call(
        _out_proj_kernel,
        grid=(bsz, seq // tm),
        in_specs=[
            pl.BlockSpec((1, tm, d), lambda b, i: (b, i, 0)),
            pl.BlockSpec((1, tm, y_rnn.shape[2]), lambda b, i: (b, i, 0)),
            pl.BlockSpec((1, tm, y_att.shape[2]), lambda b, i: (b, i, 0)),
            _const_spec(w_r.shape),
            _const_spec(w_a.shape),
            pl.BlockSpec((1, 3, d), lambda b, i: (b, 0, 0)),
        ],
        out_specs=pl.BlockSpec((1, tm, d), lambda b, i: (b, i, 0)),
        out_shape=jax.ShapeDtypeStruct((bsz, seq, d), F32),
        compiler_params=pltpu.CompilerParams(
            dimension_semantics=("arbitrary", "arbitrary"), vmem_limit_bytes=VMEM_LIMIT),
        name="out_proj",
    )(x, y_rnn, y_att, w_r, w_a, mod)


def _block_diag_groups(w):
    nb, n, _ = w.shape
    per = MXU_DIM // n
    w = w.reshape(nb // per, per, n, n)
    eye = jnp.eye(per, dtype=w.dtype)
    return jnp.einsum("gpij,pq->gpiqj", w, eye).reshape(nb // per, MXU_DIM, MXU_DIM)


def kernel(x, c, positions, w_ada, b_ada, w_in, conv_w, conv_b, w_rg_a, b_rg_a, w_rg_x, b_rg_x,
           lru_lambda, q_a_norm, w_uq, kv_a_norm, w_ukv, q_norm_nope, q_norm_rope, k_norm_nope,
           k_norm_rope, w_out):
    bsz, seq, d = x.shape
    depth = w_in.shape[0]
    d_rnn = conv_w.shape[2]
    q_lora = q_a_norm.shape[1]
    kv_lora = kv_a_norm.shape[1]
    d_att = MLA_HEADS * V_HEAD
    tile = 512

    inv_freq = 1.0 / (ROPE_THETA ** (jnp.arange(0, QK_ROPE, 2, dtype=F32) / QK_ROPE))
    pos = positions.astype(F32)
    pos_row = pos.reshape(bsz, 1, seq)
    pos_col = pos.reshape(bsz, seq, 1)
    inv_row = inv_freq.reshape(1, -1)
    inv_col = inv_freq.reshape(-1, 1)
    c_pad = jnp.zeros((SUBLANES, d), F32).at[:bsz].set(c)

    for l in range(depth):
        mod = _adaln(c_pad, w_ada[l], b_ada[l].reshape(1, -1))[:bsz].reshape(bsz, 3, d)

        o = 0
        cols = {}
        for name, n in (("wxr", d_rnn), ("wgr", d_rnn), ("wqc", q_lora), ("wkvc", kv_lora),
                        ("wkr", QK_ROPE), ("wga", d_att)):
            cols[name] = w_in[l][:, o:o + n].astype(BF16)
            o += n
        cols["wkr"] = jnp.pad(cols["wkr"], ((0, 0), (0, 128 - QK_ROPE)))
        ukv = w_ukv[l].reshape(kv_lora, MLA_HEADS, QK_NOPE + V_HEAD)
        w = dict(
            cols,
            qan=q_a_norm[l].reshape(1, -1), kvan=kv_a_norm[l].reshape(1, -1),
            wuqt=w_uq[l].T.astype(BF16),
            wuk=ukv[:, :, :QK_NOPE].reshape(kv_lora, -1).astype(BF16),
            wuvt=ukv[:, :, QK_NOPE:].reshape(kv_lora, -1).T.astype(BF16),
            gqn=q_norm_nope[l].reshape(-1, 1), gqr=q_norm_rope[l].reshape(-1, 1),
            gkn=k_norm_nope[l].reshape(1, -1), gkr=k_norm_rope[l].reshape(1, -1),
        )
        xr, sgr, sga, q_t, k, v_t = _in_proj(x, mod, pos_row, pos_col, inv_row, inv_col, w, tile)

        y_att = _attention(q_t, k, v_t, sga, tile)
        y_rnn = _rglru(
            xr, sgr, conv_w[l], conv_b[l].reshape(1, -1),
            _block_diag_groups(w_rg_a[l]).astype(BF16), b_rg_a[l].reshape(1, -1),
            _block_diag_groups(w_rg_x[l]).astype(BF16), b_rg_x[l].reshape(1, -1),
            lru_lambda[l].reshape(1, -1), tile)

        w_o = w_out[l].astype(BF16)
        x = _out_proj(x, y_rnn, y_att, w_o[:d_rnn], w_o[d_rnn:], mod, tile)
    return x
```

```python
import functools
import math

import jax
import jax.numpy as jnp
from jax import lax
from jax.experimental import pallas as pl
from jax.experimental.pallas import tpu as pltpu

RNN_BLOCKS = 16
CONV_WIDTH = 4
LRU_C = 8.0
MLA_HEADS = 8
QK_NOPE = 128
QK_ROPE = 64
V_HEAD = 128
ROPE_THETA = 10000.0
EPS = 1e-6

QK_PAD = 256
MXU_DIM = 256
VMEM_LIMIT = 56 * 1024 * 1024

F32 = jnp.float32
BF16 = jnp.bfloat16

NT_DIMS = (((1,), (1,)), ((), ()))


def _const_spec(shape):
    nd = len(shape)
    return pl.BlockSpec(shape, lambda *_: (0,) * nd, pipeline_mode=pl.Buffered(1))


def _adaln_kernel(c_ref, w_ref, b_ref, o_ref):
    c = c_ref[...]
    c_act = c * jax.nn.sigmoid(c)
    o_ref[...] = jnp.dot(c_act, w_ref[...], preferred_element_type=F32,
                         precision=lax.Precision.HIGHEST) + b_ref[...]


def _adaln(c_pad, w_ada, b_ada):
    rows, d = c_pad.shape
    n = w_ada.shape[1]
    tn = 1536
    return pl.pallas_call(
        _adaln_kernel,
        grid=(n // tn,),
        in_specs=[
            pl.BlockSpec((rows, d), lambda j: (0, 0)),
            pl.BlockSpec((d, tn), lambda j: (0, j)),
            pl.BlockSpec((1, tn), lambda j: (0, j)),
        ],
        out_specs=pl.BlockSpec((rows, tn), lambda j: (0, j)),
        out_shape=jax.ShapeDtypeStruct((rows, n), F32),
        compiler_params=pltpu.CompilerParams(
            dimension_semantics=("arbitrary",), vmem_limit_bytes=VMEM_LIMIT),
        name="adaln_mod",
    )(c_pad, w_ada, b_ada)


def _rms(v, axis):
    return v * lax.rsqrt(jnp.mean(v * v, axis=axis, keepdims=True) + EPS)


def _in_proj_kernel(x_ref, mod_ref, posr_ref, posc_ref, invr_ref, invc_ref,
                    wxr_ref, wgr_ref, wqc_ref, wkvc_ref, wkr_ref, wga_ref,
                    qan_ref, kvan_ref, wuqt_ref, wuk_ref, wuvt_ref,
                    gqn_ref, gqr_ref, gkn_ref, gkr_ref,
                    xr_ref, sgr_ref, sga_ref, qt_ref, k_ref, vt_ref, bound_ref, *, q_scale):
    x = x_ref[0]
    scale = mod_ref[0, 1:2, :]
    shift = mod_ref[0, 0:1, :]
    h = (_rms(x, -1) * (1.0 + scale) + shift).astype(BF16)

    def proj(w_ref):
        return jnp.dot(h, w_ref[...], preferred_element_type=F32)

    xr_ref[0] = proj(wxr_ref).astype(BF16)
    gr = proj(wgr_ref)
    sgr_ref[0] = (gr * jax.nn.sigmoid(gr)).astype(BF16)
    ga = proj(wga_ref)
    sga_ref[0] = (ga * jax.nn.sigmoid(ga)).astype(BF16)

    ang = posc_ref[0] * invr_ref[...]
    cos, sin = jnp.cos(ang), jnp.sin(ang)
    ang_t = invc_ref[...] * posr_ref[0]
    cos_t, sin_t = jnp.cos(ang_t), jnp.sin(ang_t)
    half = QK_ROPE // 2

    qcn = (_rms(proj(wqc_ref), -1) * qan_ref[...]).astype(BF16)
    q_t = lax.dot_general(wuqt_ref[...], qcn, NT_DIMS, preferred_element_type=F32)
    hd = QK_NOPE + QK_ROPE
    gkn, gkr = gkn_ref[...], gkr_ref[...]
    k_bound = jnp.sqrt(QK_NOPE * jnp.max(gkn * gkn, axis=1, keepdims=True)
                       + QK_ROPE * jnp.max(gkr * gkr, axis=1, keepdims=True))
    pad_row = lax.broadcasted_iota(jnp.int32, (QK_PAD - hd, q_t.shape[1]), 0)
    bound_max = None
    for hh in range(MLA_HEADS):
        qn = q_t[hh * hd:hh * hd + QK_NOPE]
        qn = _rms(qn, 0) * (gqn_ref[...] * q_scale)
        qr = q_t[hh * hd + QK_NOPE:(hh + 1) * hd]
        qr = _rms(qr, 0) * (gqr_ref[...] * q_scale)
        x1, x2 = qr[:half], qr[half:]
        r1 = x1 * cos_t - x2 * sin_t
        r2 = x1 * sin_t + x2 * cos_t
        q_sq = (jnp.sum(qn * qn, axis=0, keepdims=True)
                + jnp.sum(r1 * r1 + r2 * r2, axis=0, keepdims=True))
        bound = jnp.sqrt(q_sq) * k_bound
        bound_max = bound if bound_max is None else jnp.maximum(bound_max, bound)
        qt_ref[0, hh, 0:QK_NOPE, :] = qn.astype(BF16)
        qt_ref[0, hh, QK_NOPE:QK_NOPE + half, :] = r1.astype(BF16)
        qt_ref[0, hh, QK_NOPE + half:hd, :] = r2.astype(BF16)
        qt_ref[0, hh, hd:QK_PAD, :] = jnp.where(pad_row == 0, -bound, 0.0).astype(BF16)
    bound_ref[0] = bound_max

    kvcn = (_rms(proj(wkvc_ref), -1) * kvan_ref[...]).astype(BF16)
    v_t = lax.dot_general(wuvt_ref[...], kvcn, NT_DIMS, preferred_element_type=F32)
    for hh in range(MLA_HEADS):
        vt_ref[0, hh] = v_t[hh * V_HEAD:(hh + 1) * V_HEAD].astype(BF16)

    kr = proj(wkr_ref)[:, :QK_ROPE]
    kr = _rms(kr, -1) * gkr_ref[...]
    k1, k2 = kr[:, :half], kr[:, half:]
    pad_col = lax.broadcasted_iota(jnp.int32, (kr.shape[0], QK_PAD - hd), 1)
    kpe = jnp.concatenate(
        [k1 * cos - k2 * sin, k1 * sin + k2 * cos,
         jnp.where(pad_col == 0, 1.0, 0.0)], axis=1).astype(BF16)
    kn_all = jnp.dot(kvcn, wuk_ref[...], preferred_element_type=F32)
    for hh in range(MLA_HEADS):
        kn = kn_all[:, hh * QK_NOPE:(hh + 1) * QK_NOPE]
        k_ref[0, hh, :, 0:QK_NOPE] = (_rms(kn, -1) * gkn_ref[...]).astype(BF16)
        k_ref[0, hh, :, QK_NOPE:QK_PAD] = kpe


def _in_proj(x, mod, pos_row, pos_col, inv_row, inv_col, w, tm):
    bsz, seq, d = x.shape
    hd = QK_NOPE + QK_ROPE
    q_scale = (hd ** -0.5) * math.log2(math.e)
    consts = [w["wxr"], w["wgr"], w["wqc"], w["wkvc"], w["wkr"], w["wga"],
              w["qan"], w["kvan"], w["wuqt"], w["wuk"], w["wuvt"],
              w["gqn"], w["gqr"], w["gkn"], w["gkr"]]
    d_rnn = w["wxr"].shape[1]
    d_att = w["wga"].shape[1]
    return pl.pallas_call(
        functools.partial(_in_proj_kernel, q_scale=q_scale),
        grid=(bsz, seq // tm),
        in_specs=[
            pl.BlockSpec((1, tm, d), lambda b, i: (b, i, 0)),
            pl.BlockSpec((1, 3, d), lambda b, i: (b, 0, 0)),
            pl.BlockSpec((1, 1, tm), lambda b, i: (b, 0, i)),
            pl.BlockSpec((1, tm, 1), lambda b, i: (b, i, 0)),
            _const_spec(inv_row.shape),
            _const_spec(inv_col.shape),
        ] + [_const_spec(a.shape) for a in consts],
        out_specs=[
            pl.BlockSpec((1, tm, d_rnn), lambda b, i: (b, i, 0)),
            pl.BlockSpec((1, tm, d_rnn), lambda b, i: (b, i, 0)),
            pl.BlockSpec((1, tm, d_att), lambda b, i: (b, i, 0)),
            pl.BlockSpec((1, MLA_HEADS, QK_PAD, tm), lambda b, i: (b, 0, 0, i)),
            pl.BlockSpec((1, MLA_HEADS, tm, QK_PAD), lambda b, i: (b, 0, i, 0)),
            pl.BlockSpec((1, MLA_HEADS, V_HEAD, tm), lambda b, i: (b, 0, 0, i)),
            pl.BlockSpec((1, 1, tm), lambda b, i: (b, 0, i)),
        ],
        out_shape=[
            jax.ShapeDtypeStruct((bsz, seq, d_rnn), BF16),
            jax.ShapeDtypeStruct((bsz, seq, d_rnn), BF16),
            jax.ShapeDtypeStruct((bsz, seq, d_att), BF16),
            jax.ShapeDtypeStruct((bsz, MLA_HEADS, QK_PAD, seq), BF16),
            jax.ShapeDtypeStruct((bsz, MLA_HEADS, seq, QK_PAD), BF16),
            jax.ShapeDtypeStruct((bsz, MLA_HEADS, V_HEAD, seq), BF16),
            jax.ShapeDtypeStruct((bsz, 1, seq), F32),
        ],
        compiler_params=pltpu.CompilerParams(
            dimension_semantics=("arbitrary", "arbitrary"), vmem_limit_bytes=VMEM_LIMIT),
        name="in_proj",
    )(x, mod, pos_row, pos_col, inv_row, inv_col, *consts)


ATTN_HEADS_PER_STEP = 4
ATTN_KEY_SUB = 256
MAX_SAFE_SCORE_BOUND = 56.0


def _attn_kernel(qt_ref, k_ref, vt_ref, sga_ref, o_ref, acc_ref, *, tq, running_max):
    qi = pl.program_id(2)
    nh = qt_ref.shape[1]
    acc_ref[...] = jnp.zeros_like(acc_ref)

    sub = tq if running_max else ATTN_KEY_SUB

    def scores(unit, js):
        bi, hh, c = unit
        key0 = pl.multiple_of(js[bi] * tq + c * sub, sub)
        k_sub = k_ref[0, hh, pl.ds(key0, sub), :]
        return jnp.dot(k_sub, qt_ref[0, hh], preferred_element_type=F32)

    def accumulate(unit, js, s_t, m, l, masked):
        bi, hh, c = unit
        if masked[bi]:
            key = lax.broadcasted_iota(jnp.int32, s_t.shape, 0) + c * sub
            qry = lax.broadcasted_iota(jnp.int32, s_t.shape, 1)
            s_t = jnp.where(key <= qry, s_t, -jnp.inf)
        key0 = pl.multiple_of(js[bi] * tq + c * sub, sub)
        v_sub = vt_ref[0, hh, :, pl.ds(key0, sub)]
        if running_max:
            m_new = jnp.maximum(m, jnp.max(s_t, axis=0, keepdims=True))
            alpha = jnp.exp2(m - m_new)
            p_t = jnp.exp2(s_t - m_new)
            l_new = alpha * l + jnp.sum(p_t, axis=0, keepdims=True)
            pv = jnp.dot(v_sub, p_t.astype(BF16), preferred_element_type=F32)
            acc_ref[hh] = alpha * acc_ref[hh] + pv
        else:
            m_new = m
            p_t = jnp.exp2(s_t)
            l_new = l + jnp.sum(p_t, axis=0, keepdims=True)
            acc_ref[hh] += jnp.dot(v_sub, p_t.astype(BF16), preferred_element_type=F32)
        return m_new, l_new

    def blocks(js, masked, carry):
        units = [(bi, hh, c) for bi in range(len(js)) for hh in range(nh)
                 for c in range(tq // sub)]
        carry = list(carry)
        s_next = scores(units[0], js)
        for idx, unit in enumerate(units):
            s_cur = s_next
            if idx + 1 < len(units):
                s_next = scores(units[idx + 1], js)
            hh = unit[1]
            carry[hh] = accumulate(unit, js, s_cur, carry[hh][0], carry[hh][1], masked)
        return tuple(carry)

    carry = tuple((jnp.full((1, tq), -jnp.inf, F32), jnp.zeros((1, tq), F32)) for _ in range(nh))
    if running_max:
        carry = lax.fori_loop(0, qi, lambda j, c: blocks([j], [False], c), carry)
        carry = blocks([qi], [True], carry)
    else:
        carry = lax.fori_loop(
            0, qi >> 1, lambda p, c: blocks([2 * p, 2 * p + 1], [False, False], c), carry)
        carry = lax.cond(
            (qi & 1) == 1,
            functools.partial(blocks, [qi - 1, qi], [False, True]),
            functools.partial(blocks, [qi], [True]),
            carry)

    for hh in range(nh):
        o_t = acc_ref[hh] / carry[hh][1]
        gate = sga_ref[0, :, hh * V_HEAD:(hh + 1) * V_HEAD].astype(F32)
        o_ref[0, :, hh * V_HEAD:(hh + 1) * V_HEAD] = (o_t.T * gate).astype(BF16)


def _attention(q_t, k, v_t, sga, tq, running_max):
    bsz, nh, _, seq = q_t.shape
    hb = ATTN_HEADS_PER_STEP
    return pl.pallas_call(
        functools.partial(_attn_kernel, tq=tq, running_max=running_max),
        grid=(bsz, nh // hb, seq // tq),
        in_specs=[
            pl.BlockSpec((1, hb, QK_PAD, tq), lambda b, h, i: (b, h, 0, i)),
            pl.BlockSpec((1, hb, seq, QK_PAD), lambda b, h, i: (b, h, 0, 0)),
            pl.BlockSpec((1, hb, V_HEAD, seq), lambda b, h, i: (b, h, 0, 0)),
            pl.BlockSpec((1, tq, hb * V_HEAD), lambda b, h, i: (b, i, h)),
        ],
        out_specs=pl.BlockSpec((1, tq, hb * V_HEAD), lambda b, h, i: (b, i, h)),
        out_shape=jax.ShapeDtypeStruct((bsz, seq, nh * V_HEAD), BF16),
        scratch_shapes=[pltpu.VMEM((hb, V_HEAD, tq), F32)],
        compiler_params=pltpu.CompilerParams(
            dimension_semantics=("arbitrary", "arbitrary", "arbitrary"),
            vmem_limit_bytes=VMEM_LIMIT),
        name="mla_attention_online" if running_max else "mla_attention",
    )(q_t, k, v_t, sga)


SUBLANES = 8


def _rglru_kernel(xr_ref, sgr_ref, cw_ref, cb_ref, wa_ref, ba_ref, wx_ref, bx_ref, lam_ref,
                  o_ref, xbuf_ref, a_ref, b_ref, hc_ref, *, ts):
    @pl.when(pl.program_id(1) == 0)
    def _():
        xbuf_ref[0:SUBLANES, :] = jnp.zeros((SUBLANES, xbuf_ref.shape[1]), F32)
        hc_ref[...] = jnp.zeros_like(hc_ref)

    xbuf_ref[SUBLANES:SUBLANES + ts, :] = xr_ref[0].astype(F32)
    xc = cb_ref[...]
    for kk in range(CONV_WIDTH):
        off = SUBLANES - (CONV_WIDTH - 1) + kk
        xc = xc + cw_ref[kk:kk + 1, :] * xbuf_ref[off:off + ts, :]
    xbuf_ref[0:SUBLANES, :] = xbuf_ref[ts:ts + SUBLANES, :]

    xcb = xc.astype(BF16)
    n_grp = wa_ref.shape[0]

    def gate(w_ref, bias_ref):
        parts = [jnp.dot(xcb[:, g * MXU_DIM:(g + 1) * MXU_DIM], w_ref[g],
                         preferred_element_type=F32) for g in range(n_grp)]
        return jax.nn.sigmoid(jnp.concatenate(parts, axis=1) + bias_ref[...])

    r = gate(wa_ref, ba_ref)
    i = gate(wx_ref, bx_ref)
    nl = -lam_ref[...]
    softplus = jnp.maximum(nl, 0.0) + jnp.log1p(jnp.exp(-jnp.abs(nl)))
    log_a = (-LRU_C) * r * softplus
    a_ref[...] = jnp.exp(log_a)
    t = jnp.tanh(log_a)
    b_ref[...] = jnp.sqrt(-2.0 * t / (1.0 - t)) * (i * xc)

    row = lax.broadcasted_iota(jnp.int32, (SUBLANES, a_ref.shape[1]), 0)

    def scan_group(g, h_prev):
        base = pl.multiple_of(g * SUBLANES, SUBLANES)
        a = a_ref[pl.ds(base, SUBLANES), :]
        b = b_ref[pl.ds(base, SUBLANES), :]
        for dist in (1, 2, 4):
            keep = row >= dist
            b = jnp.where(keep, a * pltpu.roll(b, dist, 0) + b, b)
            a = jnp.where(keep, a * pltpu.roll(a, dist, 0), a)
        hg = a * h_prev + b
        b_ref[pl.ds(base, SUBLANES), :] = hg
        return jnp.broadcast_to(hg[SUBLANES - 1:SUBLANES, :], hg.shape)

    hc_ref[...] = lax.fori_loop(0, ts // SUBLANES, scan_group, hc_ref[...], unroll=4)
    o_ref[0] = (b_ref[...] * sgr_ref[0].astype(F32)).astype(BF16)


def _rglru(xr, sgr, cw, cb, wa, ba, wx, bx, lam, ts):
    bsz, seq, c = xr.shape
    consts = [cw, cb, wa, ba, wx, bx, lam]
    return pl.pallas_call(
        functools.partial(_rglru_kernel, ts=ts),
        grid=(bsz, seq // ts),
        in_specs=[
            pl.BlockSpec((1, ts, c), lambda b, i: (b, i, 0)),
            pl.BlockSpec((1, ts, c), lambda b, i: (b, i, 0)),
        ] + [_const_spec(a.shape) for a in consts],
        out_specs=pl.BlockSpec((1, ts, c), lambda b, i: (b, i, 0)),
        out_shape=jax.ShapeDtypeStruct((bsz, seq, c), BF16),
        scratch_shapes=[
            pltpu.VMEM((ts + SUBLANES, c), F32),
            pltpu.VMEM((ts, c), F32),
            pltpu.VMEM((ts, c), F32),
            pltpu.VMEM((SUBLANES, c), F32),
        ],
        compiler_params=pltpu.CompilerParams(
            dimension_semantics=("arbitrary", "arbitrary"), vmem_limit_bytes=VMEM_LIMIT),
        name="rglru",
    )(xr, sgr, *consts)


def _out_proj_kernel(x_ref, yr_ref, ya_ref, wr_ref, wa_ref, mod_ref, o_ref):
    y = jnp.dot(yr_ref[0], wr_ref[...], preferred_element_type=F32)
    y = y + jnp.dot(ya_ref[0], wa_ref[...], preferred_element_type=F32)
    o_ref[0] = x_ref[0] + mod_ref[0, 2:3, :] * y


def _out_proj(x, y_rnn, y_att, w_r, w_a, mod, tm):
    bsz, seq, d = x.shape
    return pl.pallas_call(
        _out_proj_kernel,
        grid=(bsz, seq // tm),
        in_specs=[
            pl.BlockSpec((1, tm, d), lambda b, i: (b, i, 0)),
            pl.BlockSpec((1, tm, y_rnn.shape[2]), lambda b, i: (b, i, 0)),
            pl.BlockSpec((1, tm, y_att.shape[2]), lambda b, i: (b, i, 0)),
            _const_spec(w_r.shape),
            _const_spec(w_a.shape),
            pl.BlockSpec((1, 3, d), lambda b, i: (b, 0, 0)),
        ],
        out_specs=pl.BlockSpec((1, tm, d), lambda b, i: (b, i, 0)),
        out_shape=jax.ShapeDtypeStruct((bsz, seq, d), F32),
        compiler_params=pltpu.CompilerParams(
            dimension_semantics=("arbitrary", "arbitrary"), vmem_limit_bytes=VMEM_LIMIT),
        name="out_proj",
    )(x, y_rnn, y_att, w_r, w_a, mod)


def _block_diag_groups(w):
    nb, n, _ = w.shape
    per = MXU_DIM // n
    w = w.reshape(nb // per, per, n, n)
    eye = jnp.eye(per, dtype=w.dtype)
    return jnp.einsum("gpij,pq->gpiqj", w, eye).reshape(nb // per, MXU_DIM, MXU_DIM)


def kernel(x, c, positions, w_ada, b_ada, w_in, conv_w, conv_b, w_rg_a, b_rg_a, w_rg_x, b_rg_x,
           lru_lambda, q_a_norm, w_uq, kv_a_norm, w_ukv, q_norm_nope, q_norm_rope, k_norm_nope,
           k_norm_rope, w_out):
    bsz, seq, d = x.shape
    depth = w_in.shape[0]
    d_rnn = conv_w.shape[2]
    q_lora = q_a_norm.shape[1]
    kv_lora = kv_a_norm.shape[1]
    d_att = MLA_HEADS * V_HEAD
    tile = 512

    inv_freq = 1.0 / (ROPE_THETA ** (jnp.arange(0, QK_ROPE, 2, dtype=F32) / QK_ROPE))
    pos = positions.astype(F32)
    pos_row = pos.reshape(bsz, 1, seq)
    pos_col = pos.reshape(bsz, seq, 1)
    inv_row = inv_freq.reshape(1, -1)
    inv_col = inv_freq.reshape(-1, 1)
    c_pad = jnp.zeros((SUBLANES, d), F32).at[:bsz].set(c)

    for l in range(depth):
        mod = _adaln(c_pad, w_ada[l], b_ada[l].reshape(1, -1))[:bsz].reshape(bsz, 3, d)

        o = 0
        cols = {}
        for name, n in (("wxr", d_rnn), ("wgr", d_rnn), ("wqc", q_lora), ("wkvc", kv_lora),
                        ("wkr", QK_ROPE), ("wga", d_att)):
            cols[name] = w_in[l][:, o:o + n].astype(BF16)
            o += n
        cols["wkr"] = jnp.pad(cols["wkr"], ((0, 0), (0, 128 - QK_ROPE)))
        ukv = w_ukv[l].reshape(kv_lora, MLA_HEADS, QK_NOPE + V_HEAD)
        w = dict(
            cols,
            qan=q_a_norm[l].reshape(1, -1), kvan=kv_a_norm[l].reshape(1, -1),
            wuqt=w_uq[l].T.astype(BF16),
            wuk=ukv[:, :, :QK_NOPE].reshape(kv_lora, -1).astype(BF16),
            wuvt=ukv[:, :, QK_NOPE:].reshape(kv_lora, -1).T.astype(BF16),
            gqn=q_norm_nope[l].reshape(-1, 1), gqr=q_norm_rope[l].reshape(-1, 1),
            gkn=k_norm_nope[l].reshape(1, -1), gkr=k_norm_rope[l].reshape(1, -1),
        )
        xr, sgr, sga, q_t, k, v_t, bound = _in_proj(
            x, mod, pos_row, pos_col, inv_row, inv_col, w, tile)

        y_att = lax.cond(
            jnp.max(bound) < MAX_SAFE_SCORE_BOUND,
            functools.partial(_attention, tq=tile, running_max=False),
            functools.partial(_attention, tq=tile, running_max=True),
            q_t, k, v_t, sga)
        y_rnn = _rglru(
            xr, sgr, conv_w[l], conv_b[l].reshape(1, -1),
            _block_diag_groups(w_rg_a[l]).astype(BF16), b_rg_a[l].reshape(1, -1),
            _block_diag_groups(w_rg_x[l]).astype(BF16), b_rg_x[l].reshape(1, -1),
            lru_lambda[l].reshape(1, -1), tile)

        w_o = w_out[l].astype(BF16)
        x = _out_proj(x, y_rnn, y_att, w_o[:d_rnn], w_o[d_rnn:], mod, tile)
    return x
```

```python
import functools
import math

import jax
import jax.numpy as jnp
from jax import lax
from jax.experimental import pallas as pl
from jax.experimental.pallas import tpu as pltpu

RNN_BLOCKS = 16
CONV_WIDTH = 4
LRU_C = 8.0
MLA_HEADS = 8
QK_NOPE = 128
QK_ROPE = 64
V_HEAD = 128
ROPE_THETA = 10000.0
EPS = 1e-6

QK_PAD = 256
MXU_DIM = 256
VMEM_LIMIT = 56 * 1024 * 1024

F32 = jnp.float32
BF16 = jnp.bfloat16

NT_DIMS = (((1,), (1,)), ((), ()))


def _const_spec(shape):
    nd = len(shape)
    return pl.BlockSpec(shape, lambda *_: (0,) * nd, pipeline_mode=pl.Buffered(1))


def _adaln_kernel(c_ref, w_ref, b_ref, o_ref):
    c = c_ref[...]
    c_act = c * jax.nn.sigmoid(c)
    o_ref[...] = jnp.dot(c_act, w_ref[...], preferred_element_type=F32,
                         precision=lax.Precision.HIGHEST) + b_ref[...]


def _adaln(c_pad, w_ada, b_ada):
    rows, d = c_pad.shape
    n = w_ada.shape[1]
    tn = 1536
    return pl.pallas_call(
        _adaln_kernel,
        grid=(n // tn,),
        in_specs=[
            pl.BlockSpec((rows, d), lambda j: (0, 0)),
            pl.BlockSpec((d, tn), lambda j: (0, j)),
            pl.BlockSpec((1, tn), lambda j: (0, j)),
        ],
        out_specs=pl.BlockSpec((rows, tn), lambda j: (0, j)),
        out_shape=jax.ShapeDtypeStruct((rows, n), F32),
        compiler_params=pltpu.CompilerParams(
            dimension_semantics=("arbitrary",), vmem_limit_bytes=VMEM_LIMIT),
        name="adaln_mod",
    )(c_pad, w_ada, b_ada)


IN_PROJ_ROWS = 256


def _rms(v, axis):
    return v * lax.rsqrt(jnp.mean(v * v, axis=axis, keepdims=True) + EPS)


def _in_proj_kernel(x_ref, mod_ref, posr_ref, invc_ref,
                    wxr_ref, wgr_ref, wqc_ref, wkvc_ref, wkr_ref, wga_ref,
                    qan_ref, kvan_ref, wuqt_ref, wuk_ref, wuvt_ref,
                    gqn_ref, gqr_ref, gkn_ref, gkr_ref,
                    xr_ref, sgr_ref, sga_ref, qt_ref, k_ref, vt_ref, bound_ref, *, q_scale):
    scale = mod_ref[0, 1:2, :]
    shift = mod_ref[0, 0:1, :]
    hd = QK_NOPE + QK_ROPE
    half = QK_ROPE // 2
    gkn, gkr = gkn_ref[...], gkr_ref[...]
    k_bound = jnp.sqrt(QK_NOPE * jnp.max(gkn * gkn, axis=1, keepdims=True)
                       + QK_ROPE * jnp.max(gkr * gkr, axis=0, keepdims=True))
    pad_row = lax.broadcasted_iota(jnp.int32, (QK_PAD - hd, IN_PROJ_ROWS), 0)

    for r0 in range(0, x_ref.shape[1], IN_PROJ_ROWS):
        rs = slice(r0, r0 + IN_PROJ_ROWS)
        h = (_rms(x_ref[0, rs, :], -1) * (1.0 + scale) + shift).astype(BF16)

        def proj(w_ref):
            return jnp.dot(h, w_ref[...], preferred_element_type=F32)

        qcn = (_rms(proj(wqc_ref), -1) * qan_ref[...]).astype(BF16)
        kvcn = (_rms(proj(wkvc_ref), -1) * kvan_ref[...]).astype(BF16)
        kr_t = proj(wkr_ref).T[:QK_ROPE]
        q_t = lax.dot_general(wuqt_ref[...], qcn, NT_DIMS, preferred_element_type=F32)
        v_t = lax.dot_general(wuvt_ref[...], kvcn, NT_DIMS, preferred_element_type=F32)
        kn_all = jnp.dot(kvcn, wuk_ref[...], preferred_element_type=F32)

        ang_t = invc_ref[...] * posr_ref[0, :, rs]
        cos_t, sin_t = jnp.cos(ang_t), jnp.sin(ang_t)

        def rope_t(v):
            v1, v2 = v[:half], v[half:]
            return v1 * cos_t - v2 * sin_t, v1 * sin_t + v2 * cos_t

        bound_max = None
        for hh in range(MLA_HEADS):
            qn = q_t[hh * hd:hh * hd + QK_NOPE]
            qn = _rms(qn, 0) * (gqn_ref[...] * q_scale)
            qr = q_t[hh * hd + QK_NOPE:(hh + 1) * hd]
            r1, r2 = rope_t(_rms(qr, 0) * (gqr_ref[...] * q_scale))
            q_sq = (jnp.sum(qn * qn, axis=0, keepdims=True)
                    + jnp.sum(r1 * r1 + r2 * r2, axis=0, keepdims=True))
            bound = jnp.sqrt(q_sq) * k_bound
            bound_max = bound if bound_max is None else jnp.maximum(bound_max, bound)
            qt_ref[0, hh, 0:QK_NOPE, rs] = qn.astype(BF16)
            qt_ref[0, hh, QK_NOPE:QK_NOPE + half, rs] = r1.astype(BF16)
            qt_ref[0, hh, QK_NOPE + half:hd, rs] = r2.astype(BF16)
            qt_ref[0, hh, hd:QK_PAD, rs] = jnp.where(pad_row == 0, -bound, 0.0).astype(BF16)
        bound_ref[0, :, rs] = bound_max

        for hh in range(MLA_HEADS):
            vt_ref[0, hh, :, rs] = v_t[hh * V_HEAD:(hh + 1) * V_HEAD].astype(BF16)
        k1, k2 = rope_t(_rms(kr_t, 0) * gkr)
        kpe_t = jnp.concatenate([k1, k2, jnp.where(pad_row == 0, 1.0, 0.0)], axis=0)
        kpe = kpe_t.T.astype(BF16)
        for hh in range(MLA_HEADS):
            kn = kn_all[:, hh * QK_NOPE:(hh + 1) * QK_NOPE]
            k_ref[0, hh, rs, 0:QK_NOPE] = (_rms(kn, -1) * gkn).astype(BF16)
            k_ref[0, hh, rs, QK_NOPE:QK_PAD] = kpe

        xr_ref[0, rs, :] = proj(wxr_ref).astype(BF16)
        gr = proj(wgr_ref)
        sgr_ref[0, rs, :] = (gr * jax.nn.sigmoid(gr)).astype(BF16)
        ga = proj(wga_ref)
        sga_ref[0, rs, :] = (ga * jax.nn.sigmoid(ga)).astype(BF16)


def _in_proj(x, mod, pos_row, inv_col, w, tm):
    bsz, seq, d = x.shape
    hd = QK_NOPE + QK_ROPE
    q_scale = (hd ** -0.5) * math.log2(math.e)
    consts = [w["wxr"], w["wgr"], w["wqc"], w["wkvc"], w["wkr"], w["wga"],
              w["qan"], w["kvan"], w["wuqt"], w["wuk"], w["wuvt"],
              w["gqn"], w["gqr"], w["gkn"], w["gkr"]]
    d_rnn = w["wxr"].shape[1]
    d_att = w["wga"].shape[1]
    return pl.pallas_call(
        functools.partial(_in_proj_kernel, q_scale=q_scale),
        grid=(bsz, seq // tm),
        in_specs=[
            pl.BlockSpec((1, tm, d), lambda b, i: (b, i, 0)),
            pl.BlockSpec((1, 3, d), lambda b, i: (b, 0, 0)),
            pl.BlockSpec((1, 1, tm), lambda b, i: (b, 0, i)),
            _const_spec(inv_col.shape),
        ] + [_const_spec(a.shape) for a in consts],
        out_specs=[
            pl.BlockSpec((1, tm, d_rnn), lambda b, i: (b, i, 0)),
            pl.BlockSpec((1, tm, d_rnn), lambda b, i: (b, i, 0)),
            pl.BlockSpec((1, tm, d_att), lambda b, i: (b, i, 0)),
            pl.BlockSpec((1, MLA_HEADS, QK_PAD, tm), lambda b, i: (b, 0, 0, i)),
            pl.BlockSpec((1, MLA_HEADS, tm, QK_PAD), lambda b, i: (b, 0, i, 0)),
            pl.BlockSpec((1, MLA_HEADS, V_HEAD, tm), lambda b, i: (b, 0, 0, i)),
            pl.BlockSpec((1, 1, tm), lambda b, i: (b, 0, i)),
        ],
        out_shape=[
            jax.ShapeDtypeStruct((bsz, seq, d_rnn), BF16),
            jax.ShapeDtypeStruct((bsz, seq, d_rnn), BF16),
            jax.ShapeDtypeStruct((bsz, seq, d_att), BF16),
            jax.ShapeDtypeStruct((bsz, MLA_HEADS, QK_PAD, seq), BF16),
            jax.ShapeDtypeStruct((bsz, MLA_HEADS, seq, QK_PAD), BF16),
            jax.ShapeDtypeStruct((bsz, MLA_HEADS, V_HEAD, seq), BF16),
            jax.ShapeDtypeStruct((bsz, 1, seq), F32),
        ],
        compiler_params=pltpu.CompilerParams(
            dimension_semantics=("arbitrary", "arbitrary"), vmem_limit_bytes=VMEM_LIMIT),
        name="in_proj",
    )(x, mod, pos_row, inv_col, *consts)


ATTN_HEADS_PER_STEP = 4
ATTN_KEY_SUB = 256
MAX_SAFE_SCORE_BOUND = 56.0


def _attn_kernel(qt_ref, k_ref, vt_ref, sga_ref, o_ref, acc_ref, *, tq, running_max):
    qi = pl.program_id(2)
    nh = qt_ref.shape[1]
    acc_ref[...] = jnp.zeros_like(acc_ref)

    sub = tq if running_max else ATTN_KEY_SUB

    def scores(unit, js):
        bi, hh, c = unit
        key0 = pl.multiple_of(js[bi] * tq + c * sub, sub)
        k_sub = k_ref[0, hh, pl.ds(key0, sub), :]
        return jnp.dot(k_sub, qt_ref[0, hh], preferred_element_type=F32)

    def accumulate(unit, js, s_t, m, l, masked):
        bi, hh, c = unit
        if masked[bi]:
            key = lax.broadcasted_iota(jnp.int32, s_t.shape, 0) + c * sub
            qry = lax.broadcasted_iota(jnp.int32, s_t.shape, 1)
            s_t = jnp.where(key <= qry, s_t, -jnp.inf)
        key0 = pl.multiple_of(js[bi] * tq + c * sub, sub)
        v_sub = vt_ref[0, hh, :, pl.ds(key0, sub)]
        if running_max:
            m_new = jnp.maximum(m, jnp.max(s_t, axis=0, keepdims=True))
            alpha = jnp.exp2(m - m_new)
            p_t = jnp.exp2(s_t - m_new)
            l_new = alpha * l + jnp.sum(p_t, axis=0, keepdims=True)
            pv = jnp.dot(v_sub, p_t.astype(BF16), preferred_element_type=F32)
            acc_ref[hh] = alpha * acc_ref[hh] + pv
        else:
            m_new = m
            p_t = jnp.exp2(s_t)
            l_new = l + jnp.sum(p_t, axis=0, keepdims=True)
            acc_ref[hh] += jnp.dot(v_sub, p_t.astype(BF16), preferred_element_type=F32)
        return m_new, l_new

    def blocks(js, masked, carry):
        units = [(bi, hh, c) for bi in range(len(js)) for hh in range(nh)
                 for c in range(tq // sub)]
        carry = list(carry)
        s_next = scores(units[0], js)
        for idx, unit in enumerate(units):
            s_cur = s_next
            if idx + 1 < len(units):
                s_next = scores(units[idx + 1], js)
            hh = unit[1]
            carry[hh] = accumulate(unit, js, s_cur, carry[hh][0], carry[hh][1], masked)
        return tuple(carry)

    carry = tuple((jnp.full((1, tq), -jnp.inf, F32), jnp.zeros((1, tq), F32)) for _ in range(nh))
    if running_max:
        carry = lax.fori_loop(0, qi, lambda j, c: blocks([j], [False], c), carry)
        carry = blocks([qi], [True], carry)
    else:
        carry = lax.fori_loop(
            0, qi >> 1, lambda p, c: blocks([2 * p, 2 * p + 1], [False, False], c), carry)
        carry = lax.cond(
            (qi & 1) == 1,
            functools.partial(blocks, [qi - 1, qi], [False, True]),
            functools.partial(blocks, [qi], [True]),
            carry)

    for hh in range(nh):
        o_t = acc_ref[hh] / carry[hh][1]
        gate = sga_ref[0, :, hh * V_HEAD:(hh + 1) * V_HEAD].astype(F32)
        o_ref[0, :, hh * V_HEAD:(hh + 1) * V_HEAD] = (o_t.T * gate).astype(BF16)


def _attention(q_t, k, v_t, sga, tq, running_max):
    bsz, nh, _, seq = q_t.shape
    hb = ATTN_HEADS_PER_STEP
    return pl.pallas_call(
        functools.partial(_attn_kernel, tq=tq, running_max=running_max),
        grid=(bsz, nh // hb, seq // tq),
        in_specs=[
            pl.BlockSpec((1, hb, QK_PAD, tq), lambda b, h, i: (b, h, 0, i)),
            pl.BlockSpec((1, hb, seq, QK_PAD), lambda b, h, i: (b, h, 0, 0)),
            pl.BlockSpec((1, hb, V_HEAD, seq), lambda b, h, i: (b, h, 0, 0)),
            pl.BlockSpec((1, tq, hb * V_HEAD), lambda b, h, i: (b, i, h)),
        ],
        out_specs=pl.BlockSpec((1, tq, hb * V_HEAD), lambda b, h, i: (b, i, h)),
        out_shape=jax.ShapeDtypeStruct((bsz, seq, nh * V_HEAD), BF16),
        scratch_shapes=[pltpu.VMEM((hb, V_HEAD, tq), F32)],
        compiler_params=pltpu.CompilerParams(
            dimension_semantics=("arbitrary", "arbitrary", "arbitrary"),
            vmem_limit_bytes=VMEM_LIMIT),
        name="mla_attention_online" if running_max else "mla_attention",
    )(q_t, k, v_t, sga)


SUBLANES = 8


def _rglru_kernel(xr_ref, sgr_ref, cw_ref, cb_ref, wa_ref, ba_ref, wx_ref, bx_ref, lam_ref,
                  o_ref, xbuf_ref, a_ref, b_ref, hc_ref, *, ts):
    @pl.when(pl.program_id(1) == 0)
    def _():
        xbuf_ref[0:SUBLANES, :] = jnp.zeros((SUBLANES, xbuf_ref.shape[1]), F32)
        hc_ref[...] = jnp.zeros_like(hc_ref)

    xbuf_ref[SUBLANES:SUBLANES + ts, :] = xr_ref[0].astype(F32)
    xc = cb_ref[...]
    for kk in range(CONV_WIDTH):
        off = SUBLANES - (CONV_WIDTH - 1) + kk
        xc = xc + cw_ref[kk:kk + 1, :] * xbuf_ref[off:off + ts, :]
    xbuf_ref[0:SUBLANES, :] = xbuf_ref[ts:ts + SUBLANES, :]

    xcb = xc.astype(BF16)
    n_grp = wa_ref.shape[0]

    def gate(w_ref, bias_ref):
        parts = [jnp.dot(xcb[:, g * MXU_DIM:(g + 1) * MXU_DIM], w_ref[g],
                         preferred_element_type=F32) for g in range(n_grp)]
        return jax.nn.sigmoid(jnp.concatenate(parts, axis=1) + bias_ref[...])

    r = gate(wa_ref, ba_ref)
    i = gate(wx_ref, bx_ref)
    nl = -lam_ref[...]
    softplus = jnp.maximum(nl, 0.0) + jnp.log1p(jnp.exp(-jnp.abs(nl)))
    log_a = (-LRU_C) * r * softplus
    a_ref[...] = jnp.exp(log_a)
    t = jnp.tanh(log_a)
    b_ref[...] = jnp.sqrt(-2.0 * t / (1.0 - t)) * (i * xc)

    row = lax.broadcasted_iota(jnp.int32, (SUBLANES, a_ref.shape[1]), 0)

    def scan_group(g, h_prev):
        base = pl.multiple_of(g * SUBLANES, SUBLANES)
        a = a_ref[pl.ds(base, SUBLANES), :]
        b = b_ref[pl.ds(base, SUBLANES), :]
        for dist in (1, 2, 4):
            keep = row >= dist
            b = jnp.where(keep, a * pltpu.roll(b, dist, 0) + b, b)
            a = jnp.where(keep, a * pltpu.roll(a, dist, 0), a)
        hg = a * h_prev + b
        b_ref[pl.ds(base, SUBLANES), :] = hg
        return jnp.broadcast_to(hg[SUBLANES - 1:SUBLANES, :], hg.shape)

    hc_ref[...] = lax.fori_loop(0, ts // SUBLANES, scan_group, hc_ref[...], unroll=4)
    o_ref[0] = (b_ref[...] * sgr_ref[0].astype(F32)).astype(BF16)


def _rglru(xr, sgr, cw, cb, wa, ba, wx, bx, lam, ts):
    bsz, seq, c = xr.shape
    consts = [cw, cb, wa, ba, wx, bx, lam]
    return pl.pallas_call(
        functools.partial(_rglru_kernel, ts=ts),
        grid=(bsz, seq // ts),
        in_specs=[
            pl.BlockSpec((1, ts, c), lambda b, i: (b, i, 0)),
            pl.BlockSpec((1, ts, c), lambda b, i: (b, i, 0)),
        ] + [_const_spec(a.shape) for a in consts],
        out_specs=pl.BlockSpec((1, ts, c), lambda b, i: (b, i, 0)),
        out_shape=jax.ShapeDtypeStruct((bsz, seq, c), BF16),
        scratch_shapes=[
            pltpu.VMEM((ts + SUBLANES, c), F32),
            pltpu.VMEM((ts, c), F32),
            pltpu.VMEM((ts, c), F32),
            pltpu.VMEM((SUBLANES, c), F32),
        ],
        compiler_params=pltpu.CompilerParams(
            dimension_semantics=("arbitrary", "arbitrary"), vmem_limit_bytes=VMEM_LIMIT),
        name="rglru",
    )(xr, sgr, *consts)


def _out_proj_kernel(x_ref, yr_ref, ya_ref, wr_ref, wa_ref, mod_ref, o_ref):
    y = jnp.dot(yr_ref[0], wr_ref[...], preferred_element_type=F32)
    y = y + jnp.dot(ya_ref[0], wa_ref[...], preferred_element_type=F32)
    o_ref[0] = x_ref[0] + mod_ref[0, 2:3, :] * y


def _out_proj(x, y_rnn, y_att, w_r, w_a, mod, tm):
    bsz, seq, d = x.shape
    return pl.pallas_call(
        _out_proj_kernel,
        grid=(bsz, seq // tm),
        in_specs=[
            pl.BlockSpec((1, tm, d), lambda b, i: (b, i, 0)),
            pl.BlockSpec((1, tm, y_rnn.shape[2]), lambda b, i: (b, i, 0)),
            pl.BlockSpec((1, tm, y_att.shape[2]), lambda b, i: (b, i, 0)),
            _const_spec(w_r.shape),
            _const_spec(w_a.shape),
            pl.BlockSpec((1, 3, d), lambda b, i: (b, 0, 0)),
        ],
        out_specs=pl.BlockSpec((1, tm, d), lambda b, i: (b, i, 0)),
        out_shape=jax.ShapeDtypeStruct((bsz, seq, d), F32),
        compiler_params=pltpu.CompilerParams(
            dimension_semantics=("arbitrary", "arbitrary"), vmem_limit_bytes=VMEM_LIMIT),
        name="out_proj",
    )(x, y_rnn, y_att, w_r, w_a, mod)


def _block_diag_groups(w):
    nb, n, _ = w.shape
    per = MXU_DIM // n
    w = w.reshape(nb // per, per, n, n)
    eye = jnp.eye(per, dtype=w.dtype)
    return jnp.einsum("gpij,pq->gpiqj", w, eye).reshape(nb // per, MXU_DIM, MXU_DIM)


def kernel(x, c, positions, w_ada, b_ada, w_in, conv_w, conv_b, w_rg_a, b_rg_a, w_rg_x, b_rg_x,
           lru_lambda, q_a_norm, w_uq, kv_a_norm, w_ukv, q_norm_nope, q_norm_rope, k_norm_nope,
           k_norm_rope, w_out):
    bsz, seq, d = x.shape
    depth = w_in.shape[0]
    d_rnn = conv_w.shape[2]
    q_lora = q_a_norm.shape[1]
    kv_lora = kv_a_norm.shape[1]
    d_att = MLA_HEADS * V_HEAD
    tile = 512

    inv_freq = 1.0 / (ROPE_THETA ** (jnp.arange(0, QK_ROPE, 2, dtype=F32) / QK_ROPE))
    pos = positions.astype(F32)
    pos_row = pos.reshape(bsz, 1, seq)
    inv_col = inv_freq.reshape(-1, 1)
    c_pad = jnp.zeros((SUBLANES, d), F32).at[:bsz].set(c)

    for l in range(depth):
        mod = _adaln(c_pad, w_ada[l], b_ada[l].reshape(1, -1))[:bsz].reshape(bsz, 3, d)

        o = 0
        cols = {}
        for name, n in (("wxr", d_rnn), ("wgr", d_rnn), ("wqc", q_lora), ("wkvc", kv_lora),
                        ("wkr", QK_ROPE), ("wga", d_att)):
            cols[name] = w_in[l][:, o:o + n].astype(BF16)
            o += n
        cols["wkr"] = jnp.pad(cols["wkr"], ((0, 0), (0, 128 - QK_ROPE)))
        ukv = w_ukv[l].reshape(kv_lora, MLA_HEADS, QK_NOPE + V_HEAD)
        w = dict(
            cols,
            qan=q_a_norm[l].reshape(1, -1), kvan=kv_a_norm[l].reshape(1, -1),
            wuqt=w_uq[l].T.astype(BF16),
            wuk=ukv[:, :, :QK_NOPE].reshape(kv_lora, -1).astype(BF16),
            wuvt=ukv[:, :, QK_NOPE:].reshape(kv_lora, -1).T.astype(BF16),
            gqn=q_norm_nope[l].reshape(-1, 1), gqr=q_norm_rope[l].reshape(-1, 1),
            gkn=k_norm_nope[l].reshape(1, -1), gkr=k_norm_rope[l].reshape(-1, 1),
        )
        xr, sgr, sga, q_t, k, v_t, bound = _in_proj(x, mod, pos_row, inv_col, w, tile)

        y_att = lax.cond(
            jnp.max(bound) < MAX_SAFE_SCORE_BOUND,
            functools.partial(_attention, tq=tile, running_max=False),
            functools.partial(_attention, tq=tile, running_max=True),
            q_t, k, v_t, sga)
        y_rnn = _rglru(
            xr, sgr, conv_w[l], conv_b[l].reshape(1, -1),
            _block_diag_groups(w_rg_a[l]).astype(BF16), b_rg_a[l].reshape(1, -1),
            _block_diag_groups(w_rg_x[l]).astype(BF16), b_rg_x[l].reshape(1, -1),
            lru_lambda[l].reshape(1, -1), tile)

        w_o = w_out[l].astype(BF16)
        x = _out_proj(x, y_rnn, y_att, w_o[:d_rnn], w_o[d_rnn:], mod, tile)
    return x
```

```python
import functools
import math

import jax
import jax.numpy as jnp
from jax import lax
from jax.experimental import pallas as pl
from jax.experimental.pallas import tpu as pltpu

RNN_BLOCKS = 16
CONV_WIDTH = 4
LRU_C = 8.0
MLA_HEADS = 8
QK_NOPE = 128
QK_ROPE = 64
V_HEAD = 128
ROPE_THETA = 10000.0
EPS = 1e-6

QK_PAD = 256
MXU_DIM = 256
VMEM_LIMIT = 56 * 1024 * 1024

F32 = jnp.float32
BF16 = jnp.bfloat16

NT_DIMS = (((1,), (1,)), ((), ()))


def _const_spec(shape):
    nd = len(shape)
    return pl.BlockSpec(shape, lambda *_: (0,) * nd, pipeline_mode=pl.Buffered(1))


def _adaln_kernel(c_ref, w_ref, b_ref, o_ref):
    c = c_ref[...]
    c_act = c * jax.nn.sigmoid(c)
    o_ref[...] = jnp.dot(c_act, w_ref[...], preferred_element_type=F32,
                         precision=lax.Precision.HIGHEST) + b_ref[...]


def _adaln(c_pad, w_ada, b_ada):
    rows, d = c_pad.shape
    n = w_ada.shape[1]
    tn = 1536
    return pl.pallas_call(
        _adaln_kernel,
        grid=(n // tn,),
        in_specs=[
            pl.BlockSpec((rows, d), lambda j: (0, 0)),
            pl.BlockSpec((d, tn), lambda j: (0, j)),
            pl.BlockSpec((1, tn), lambda j: (0, j)),
        ],
        out_specs=pl.BlockSpec((rows, tn), lambda j: (0, j)),
        out_shape=jax.ShapeDtypeStruct((rows, n), F32),
        compiler_params=pltpu.CompilerParams(
            dimension_semantics=("arbitrary",), vmem_limit_bytes=VMEM_LIMIT),
        name="adaln_mod",
    )(c_pad, w_ada, b_ada)


IN_PROJ_ROWS = 256


def _rms(v, axis):
    return v * lax.rsqrt(jnp.mean(v * v, axis=axis, keepdims=True) + EPS)


def _in_proj_kernel(x_ref, mod_ref, posr_ref, invc_ref, wmain_ref, wtail_ref,
                    qan_ref, kvan_ref, wuqt_ref, wuk_ref, wuvt_ref,
                    gqn_ref, gqr_ref, gkn_ref, gkr_ref,
                    xr_ref, sgr_ref, sga_ref, qt_ref, k_ref, vt_ref, bound_ref, *, q_scale):
    d_rnn, d_att = xr_ref.shape[2], sga_ref.shape[2]
    q_lora, kv_lora = qan_ref.shape[1], kvan_ref.shape[1]
    qc0, kvc0 = 2 * d_rnn, 2 * d_rnn + q_lora
    scale = mod_ref[0, 1:2, :]
    shift = mod_ref[0, 0:1, :]
    hd = QK_NOPE + QK_ROPE
    half = QK_ROPE // 2
    gkn, gkr = gkn_ref[...], gkr_ref[...]
    k_bound = jnp.sqrt(QK_NOPE * jnp.max(gkn * gkn, axis=1, keepdims=True)
                       + QK_ROPE * jnp.max(gkr * gkr, axis=0, keepdims=True))
    pad_row = lax.broadcasted_iota(jnp.int32, (QK_PAD - hd, IN_PROJ_ROWS), 0)

    for r0 in range(0, x_ref.shape[1], IN_PROJ_ROWS):
        rs = slice(r0, r0 + IN_PROJ_ROWS)
        h = (_rms(x_ref[0, rs, :], -1) * (1.0 + scale) + shift).astype(BF16)

        def proj(w_ref, lo, hi):
            return jnp.dot(h, w_ref[:, lo:hi], preferred_element_type=F32)

        qcn = (_rms(proj(wmain_ref, qc0, kvc0), -1) * qan_ref[...]).astype(BF16)
        kvcn = (_rms(proj(wmain_ref, kvc0, kvc0 + kv_lora), -1) * kvan_ref[...]).astype(BF16)
        kr_t = proj(wtail_ref, d_att, wtail_ref.shape[1]).T[:QK_ROPE]
        q_t = lax.dot_general(wuqt_ref[...], qcn, NT_DIMS, preferred_element_type=F32)
        v_t = lax.dot_general(wuvt_ref[...], kvcn, NT_DIMS, preferred_element_type=F32)
        kn_all = jnp.dot(kvcn, wuk_ref[...], preferred_element_type=F32)

        ang_t = invc_ref[...] * posr_ref[0, :, rs]
        cos_t, sin_t = jnp.cos(ang_t), jnp.sin(ang_t)

        def rope_t(v):
            v1, v2 = v[:half], v[half:]
            return v1 * cos_t - v2 * sin_t, v1 * sin_t + v2 * cos_t

        bound_max = None
        for hh in range(MLA_HEADS):
            qn = q_t[hh * hd:hh * hd + QK_NOPE]
            qn = _rms(qn, 0) * (gqn_ref[...] * q_scale)
            qr = q_t[hh * hd + QK_NOPE:(hh + 1) * hd]
            r1, r2 = rope_t(_rms(qr, 0) * (gqr_ref[...] * q_scale))
            q_sq = (jnp.sum(qn * qn, axis=0, keepdims=True)
                    + jnp.sum(r1 * r1 + r2 * r2, axis=0, keepdims=True))
            bound = jnp.sqrt(q_sq) * k_bound
            bound_max = bound if bound_max is None else jnp.maximum(bound_max, bound)
            qt_ref[0, hh, 0:QK_NOPE, rs] = qn.astype(BF16)
            qt_ref[0, hh, QK_NOPE:QK_NOPE + half, rs] = r1.astype(BF16)
            qt_ref[0, hh, QK_NOPE + half:hd, rs] = r2.astype(BF16)
            qt_ref[0, hh, hd:QK_PAD, rs] = jnp.where(pad_row == 0, -bound, 0.0).astype(BF16)
        bound_ref[0, :, rs] = bound_max

        for hh in range(MLA_HEADS):
            vt_ref[0, hh, :, rs] = v_t[hh * V_HEAD:(hh + 1) * V_HEAD].astype(BF16)
        k1, k2 = rope_t(_rms(kr_t, 0) * gkr)
        kpe_t = jnp.concatenate([k1, k2, jnp.where(pad_row == 0, 1.0, 0.0)], axis=0)
        kpe = kpe_t.T.astype(BF16)
        for hh in range(MLA_HEADS):
            kn = kn_all[:, hh * QK_NOPE:(hh + 1) * QK_NOPE]
            k_ref[0, hh, rs, 0:QK_NOPE] = (_rms(kn, -1) * gkn).astype(BF16)
            k_ref[0, hh, rs, QK_NOPE:QK_PAD] = kpe

        xr_ref[0, rs, :] = proj(wmain_ref, 0, d_rnn).astype(BF16)
        gr = proj(wmain_ref, d_rnn, 2 * d_rnn)
        sgr_ref[0, rs, :] = (gr * jax.nn.sigmoid(gr)).astype(BF16)
        ga = proj(wtail_ref, 0, d_att)
        sga_ref[0, rs, :] = (ga * jax.nn.sigmoid(ga)).astype(BF16)


def _in_proj(x, mod, pos_row, inv_col, w, tm):
    bsz, seq, d = x.shape
    hd = QK_NOPE + QK_ROPE
    q_scale = (hd ** -0.5) * math.log2(math.e)
    consts = [w["wmain"], w["wtail"],
              w["qan"], w["kvan"], w["wuqt"], w["wuk"], w["wuvt"],
              w["gqn"], w["gqr"], w["gkn"], w["gkr"]]
    d_rnn, d_att = w["d_rnn"], MLA_HEADS * V_HEAD
    return pl.pallas_call(
        functools.partial(_in_proj_kernel, q_scale=q_scale),
        grid=(bsz, seq // tm),
        in_specs=[
            pl.BlockSpec((1, tm, d), lambda b, i: (b, i, 0)),
            pl.BlockSpec((1, 3, d), lambda b, i: (b, 0, 0)),
            pl.BlockSpec((1, 1, tm), lambda b, i: (b, 0, i)),
            _const_spec(inv_col.shape),
        ] + [_const_spec(a.shape) for a in consts],
        out_specs=[
            pl.BlockSpec((1, tm, d_rnn), lambda b, i: (b, i, 0)),
            pl.BlockSpec((1, tm, d_rnn), lambda b, i: (b, i, 0)),
            pl.BlockSpec((1, tm, d_att), lambda b, i: (b, i, 0)),
            pl.BlockSpec((1, MLA_HEADS, QK_PAD, tm), lambda b, i: (b, 0, 0, i)),
            pl.BlockSpec((1, MLA_HEADS, tm, QK_PAD), lambda b, i: (b, 0, i, 0)),
            pl.BlockSpec((1, MLA_HEADS, V_HEAD, tm), lambda b, i: (b, 0, 0, i)),
            pl.BlockSpec((1, 1, tm), lambda b, i: (b, 0, i)),
        ],
        out_shape=[
            jax.ShapeDtypeStruct((bsz, seq, d_rnn), BF16),
            jax.ShapeDtypeStruct((bsz, seq, d_rnn), BF16),
            jax.ShapeDtypeStruct((bsz, seq, d_att), BF16),
            jax.ShapeDtypeStruct((bsz, MLA_HEADS, QK_PAD, seq), BF16),
            jax.ShapeDtypeStruct((bsz, MLA_HEADS, seq, QK_PAD), BF16),
            jax.ShapeDtypeStruct((bsz, MLA_HEADS, V_HEAD, seq), BF16),
            jax.ShapeDtypeStruct((bsz, 1, seq), F32),
        ],
        compiler_params=pltpu.CompilerParams(
            dimension_semantics=("arbitrary", "arbitrary"), vmem_limit_bytes=VMEM_LIMIT),
        name="in_proj",
    )(x, mod, pos_row, inv_col, *consts)


ATTN_HEADS_PER_STEP = 4
ATTN_KEY_SUB = 256
MAX_SAFE_SCORE_BOUND = 56.0


def _attn_kernel(qt_ref, k_ref, vt_ref, sga_ref, o_ref, acc_ref, *, tq, running_max):
    qi = pl.program_id(2)
    nh = qt_ref.shape[1]
    acc_ref[...] = jnp.zeros_like(acc_ref)

    sub = tq if running_max else ATTN_KEY_SUB

    def first_query(unit, masked):
        return unit[2] * sub if masked[unit[0]] else 0

    def scores(unit, js, masked):
        bi, hh, c = unit
        key0 = pl.multiple_of(js[bi] * tq + c * sub, sub)
        k_sub = k_ref[0, hh, pl.ds(key0, sub), :]
        q_t = qt_ref[0, hh, :, first_query(unit, masked):]
        return jnp.dot(k_sub, q_t, preferred_element_type=F32)

    def accumulate(unit, js, s_t, m, l, masked):
        bi, hh, c = unit
        q0 = first_query(unit, masked)
        if masked[bi]:
            key = lax.broadcasted_iota(jnp.int32, s_t.shape, 0) + c * sub
            qry = lax.broadcasted_iota(jnp.int32, s_t.shape, 1) + q0
            s_t = jnp.where(key <= qry, s_t, -jnp.inf)
        key0 = pl.multiple_of(js[bi] * tq + c * sub, sub)
        v_sub = vt_ref[0, hh, :, pl.ds(key0, sub)]
        if running_max:
            m_new = jnp.maximum(m, jnp.max(s_t, axis=0, keepdims=True))
            alpha = jnp.exp2(m - m_new)
            p_t = jnp.exp2(s_t - m_new)
            l_new = alpha * l + jnp.sum(p_t, axis=0, keepdims=True)
            pv = jnp.dot(v_sub, p_t.astype(BF16), preferred_element_type=F32)
            acc_ref[hh] = alpha * acc_ref[hh] + pv
        else:
            m_new = m
            p_t = jnp.exp2(s_t)
            l_part = l[:, q0:] + jnp.sum(p_t, axis=0, keepdims=True)
            l_new = l_part if q0 == 0 else jnp.concatenate([l[:, :q0], l_part], axis=1)
            acc_ref[hh, :, q0:] += jnp.dot(v_sub, p_t.astype(BF16), preferred_element_type=F32)
        return m_new, l_new

    def blocks(js, masked, carry):
        units = [(bi, hh, c) for bi in range(len(js)) for hh in range(nh)
                 for c in range(tq // sub)]
        carry = list(carry)
        s_next = scores(units[0], js, masked)
        for idx, unit in enumerate(units):
            s_cur = s_next
            if idx + 1 < len(units):
                s_next = scores(units[idx + 1], js, masked)
            hh = unit[1]
            carry[hh] = accumulate(unit, js, s_cur, carry[hh][0], carry[hh][1], masked)
        return tuple(carry)

    carry = tuple((jnp.full((1, tq), -jnp.inf, F32), jnp.zeros((1, tq), F32)) for _ in range(nh))
    if running_max:
        carry = lax.fori_loop(0, qi, lambda j, c: blocks([j], [False], c), carry)
        carry = blocks([qi], [True], carry)
    else:
        carry = lax.fori_loop(
            0, qi >> 1, lambda p, c: blocks([2 * p, 2 * p + 1], [False, False], c), carry)
        carry = lax.cond(
            (qi & 1) == 1,
            functools.partial(blocks, [qi - 1, qi], [False, True]),
            functools.partial(blocks, [qi], [True]),
            carry)

    for hh in range(nh):
        o_t = acc_ref[hh] / carry[hh][1]
        gate = sga_ref[0, :, hh * V_HEAD:(hh + 1) * V_HEAD].astype(F32)
        o_ref[0, :, hh * V_HEAD:(hh + 1) * V_HEAD] = (o_t.T * gate).astype(BF16)


def _attention(q_t, k, v_t, sga, tq, running_max):
    bsz, nh, _, seq = q_t.shape
    hb = ATTN_HEADS_PER_STEP
    return pl.pallas_call(
        functools.partial(_attn_kernel, tq=tq, running_max=running_max),
        grid=(bsz, nh // hb, seq // tq),
        in_specs=[
            pl.BlockSpec((1, hb, QK_PAD, tq), lambda b, h, i: (b, h, 0, i)),
            pl.BlockSpec((1, hb, seq, QK_PAD), lambda b, h, i: (b, h, 0, 0)),
            pl.BlockSpec((1, hb, V_HEAD, seq), lambda b, h, i: (b, h, 0, 0)),
            pl.BlockSpec((1, tq, hb * V_HEAD), lambda b, h, i: (b, i, h)),
        ],
        out_specs=pl.BlockSpec((1, tq, hb * V_HEAD), lambda b, h, i: (b, i, h)),
        out_shape=jax.ShapeDtypeStruct((bsz, seq, nh * V_HEAD), BF16),
        scratch_shapes=[pltpu.VMEM((hb, V_HEAD, tq), F32)],
        compiler_params=pltpu.CompilerParams(
            dimension_semantics=("arbitrary", "arbitrary", "arbitrary"),
            vmem_limit_bytes=VMEM_LIMIT),
        name="mla_attention_online" if running_max else "mla_attention",
    )(q_t, k, v_t, sga)


SUBLANES = 8


def _rglru_kernel(xr_ref, sgr_ref, cw_ref, cb_ref, wa_ref, ba_ref, wx_ref, bx_ref, lam_ref,
                  o_ref, xbuf_ref, a_ref, b_ref, hc_ref, *, ts):
    @pl.when(pl.program_id(1) == 0)
    def _():
        xbuf_ref[0:SUBLANES, :] = jnp.zeros((SUBLANES, xbuf_ref.shape[1]), F32)
        hc_ref[...] = jnp.zeros_like(hc_ref)

    xbuf_ref[SUBLANES:SUBLANES + ts, :] = xr_ref[0].astype(F32)
    xc = cb_ref[...]
    for kk in range(CONV_WIDTH):
        off = SUBLANES - (CONV_WIDTH - 1) + kk
        xc = xc + cw_ref[kk:kk + 1, :] * xbuf_ref[off:off + ts, :]
    xbuf_ref[0:SUBLANES, :] = xbuf_ref[ts:ts + SUBLANES, :]

    xcb = xc.astype(BF16)
    n_grp = wa_ref.shape[0]

    def gate(w_ref, bias_ref):
        parts = [jnp.dot(xcb[:, g * MXU_DIM:(g + 1) * MXU_DIM], w_ref[g],
                         preferred_element_type=F32) for g in range(n_grp)]
        return jax.nn.sigmoid(jnp.concatenate(parts, axis=1) + bias_ref[...])

    r = gate(wa_ref, ba_ref)
    i = gate(wx_ref, bx_ref)
    nl = -lam_ref[...]
    softplus = jnp.maximum(nl, 0.0) + jnp.log1p(jnp.exp(-jnp.abs(nl)))
    log_a = (-LRU_C) * r * softplus
    a_ref[...] = jnp.exp(log_a)
    t = jnp.tanh(log_a)
    b_ref[...] = jnp.sqrt(-2.0 * t / (1.0 - t)) * (i * xc)

    row = lax.broadcasted_iota(jnp.int32, (SUBLANES, a_ref.shape[1]), 0)

    def scan_group(g, h_prev):
        base = pl.multiple_of(g * SUBLANES, SUBLANES)
        a = a_ref[pl.ds(base, SUBLANES), :]
        b = b_ref[pl.ds(base, SUBLANES), :]
        for dist in (1, 2, 4):
            keep = row >= dist
            b = jnp.where(keep, a * pltpu.roll(b, dist, 0) + b, b)
            a = jnp.where(keep, a * pltpu.roll(a, dist, 0), a)
        hg = a * h_prev + b
        b_ref[pl.ds(base, SUBLANES), :] = hg
        return jnp.broadcast_to(hg[SUBLANES - 1:SUBLANES, :], hg.shape)

    hc_ref[...] = lax.fori_loop(0, ts // SUBLANES, scan_group, hc_ref[...], unroll=4)
    o_ref[0] = (b_ref[...] * sgr_ref[0].astype(F32)).astype(BF16)


def _rglru(xr, sgr, cw, cb, wa, ba, wx, bx, lam, ts):
    bsz, seq, c = xr.shape
    consts = [cw, cb, wa, ba, wx, bx, lam]
    return pl.pallas_call(
        functools.partial(_rglru_kernel, ts=ts),
        grid=(bsz, seq // ts),
        in_specs=[
            pl.BlockSpec((1, ts, c), lambda b, i: (b, i, 0)),
            pl.BlockSpec((1, ts, c), lambda b, i: (b, i, 0)),
        ] + [_const_spec(a.shape) for a in consts],
        out_specs=pl.BlockSpec((1, ts, c), lambda b, i: (b, i, 0)),
        out_shape=jax.ShapeDtypeStruct((bsz, seq, c), BF16),
        scratch_shapes=[
            pltpu.VMEM((ts + SUBLANES, c), F32),
            pltpu.VMEM((ts, c), F32),
            pltpu.VMEM((ts, c), F32),
            pltpu.VMEM((SUBLANES, c), F32),
        ],
        compiler_params=pltpu.CompilerParams(
            dimension_semantics=("arbitrary", "arbitrary"), vmem_limit_bytes=VMEM_LIMIT),
        name="rglru",
    )(xr, sgr, *consts)


def _out_proj_kernel(x_ref, yr_ref, ya_ref, w_ref, mod_ref, o_ref):
    d_rnn = yr_ref.shape[2]
    y = jnp.dot(yr_ref[0], w_ref[0:d_rnn, :], preferred_element_type=F32)
    y = y + jnp.dot(ya_ref[0], w_ref[d_rnn:, :], preferred_element_type=F32)
    o_ref[0] = x_ref[0] + mod_ref[0, 2:3, :] * y


def _out_proj(x, y_rnn, y_att, w_o, mod, tm):
    bsz, seq, d = x.shape
    return pl.pallas_call(
        _out_proj_kernel,
        grid=(bsz, seq // tm),
        in_specs=[
            pl.BlockSpec((1, tm, d), lambda b, i: (b, i, 0)),
            pl.BlockSpec((1, tm, y_rnn.shape[2]), lambda b, i: (b, i, 0)),
            pl.BlockSpec((1, tm, y_att.shape[2]), lambda b, i: (b, i, 0)),
            _const_spec(w_o.shape),
            pl.BlockSpec((1, 3, d), lambda b, i: (b, 0, 0)),
        ],
        out_specs=pl.BlockSpec((1, tm, d), lambda b, i: (b, i, 0)),
        out_shape=jax.ShapeDtypeStruct((bsz, seq, d), F32),
        compiler_params=pltpu.CompilerParams(
            dimension_semantics=("arbitrary", "arbitrary"), vmem_limit_bytes=VMEM_LIMIT),
        name="out_proj",
    )(x, y_rnn, y_att, w_o, mod)


def _block_diag_groups(w):
    nb, n, _ = w.shape
    per = MXU_DIM // n
    w = w.reshape(nb // per, per, n, n)
    eye = jnp.eye(per, dtype=w.dtype)
    return jnp.einsum("gpij,pq->gpiqj", w, eye).reshape(nb // per, MXU_DIM, MXU_DIM)


def kernel(x, c, positions, w_ada, b_ada, w_in, conv_w, conv_b, w_rg_a, b_rg_a, w_rg_x, b_rg_x,
           lru_lambda, q_a_norm, w_uq, kv_a_norm, w_ukv, q_norm_nope, q_norm_rope, k_norm_nope,
           k_norm_rope, w_out):
    bsz, seq, d = x.shape
    depth = w_in.shape[0]
    d_rnn = conv_w.shape[2]
    q_lora = q_a_norm.shape[1]
    kv_lora = kv_a_norm.shape[1]
    d_att = MLA_HEADS * V_HEAD
    tile = 512

    inv_freq = 1.0 / (ROPE_THETA ** (jnp.arange(0, QK_ROPE, 2, dtype=F32) / QK_ROPE))
    pos = positions.astype(F32)
    pos_row = pos.reshape(bsz, 1, seq)
    inv_col = inv_freq.reshape(-1, 1)
    c_pad = jnp.zeros((SUBLANES, d), F32).at[:bsz].set(c)

    for l in range(depth):
        mod = _adaln(c_pad, w_ada[l], b_ada[l].reshape(1, -1))[:bsz].reshape(bsz, 3, d)

        n_main = 2 * d_rnn + q_lora + kv_lora
        w_tail = jnp.concatenate(
            [w_in[l][:, n_main + QK_ROPE:], w_in[l][:, n_main:n_main + QK_ROPE],
             jnp.zeros((d, 128 - QK_ROPE), w_in.dtype)], axis=1)
        ukv = w_ukv[l].reshape(kv_lora, MLA_HEADS, QK_NOPE + V_HEAD)
        w = dict(
            d_rnn=d_rnn, wmain=w_in[l][:, :n_main].astype(BF16), wtail=w_tail.astype(BF16),
            qan=q_a_norm[l].reshape(1, -1), kvan=kv_a_norm[l].reshape(1, -1),
            wuqt=w_uq[l].T.astype(BF16),
            wuk=ukv[:, :, :QK_NOPE].reshape(kv_lora, -1).astype(BF16),
            wuvt=ukv[:, :, QK_NOPE:].reshape(kv_lora, -1).T.astype(BF16),
            gqn=q_norm_nope[l].reshape(-1, 1), gqr=q_norm_rope[l].reshape(-1, 1),
            gkn=k_norm_nope[l].reshape(1, -1), gkr=k_norm_rope[l].reshape(-1, 1),
        )
        xr, sgr, sga, q_t, k, v_t, bound = _in_proj(x, mod, pos_row, inv_col, w, tile)

        y_att = lax.cond(
            jnp.max(bound) < MAX_SAFE_SCORE_BOUND,
            functools.partial(_attention, tq=tile, running_max=False),
            functools.partial(_attention, tq=tile, running_max=True),
            q_t, k, v_t, sga)
        y_rnn = _rglru(
            xr, sgr, conv_w[l], conv_b[l].reshape(1, -1),
            _block_diag_groups(w_rg_a[l]).astype(BF16), b_rg_a[l].reshape(1, -1),
            _block_diag_groups(w_rg_x[l]).astype(BF16), b_rg_x[l].reshape(1, -1),
            lru_lambda[l].reshape(1, -1), tile)

        x = _out_proj(x, y_rnn, y_att, w_out[l].astype(BF16), mod, tile)
    return x
```

```python
import functools
import math

import jax
import jax.numpy as jnp
from jax import lax
from jax.experimental import pallas as pl
from jax.experimental.pallas import tpu as pltpu

RNN_BLOCKS = 16
CONV_WIDTH = 4
LRU_C = 8.0
MLA_HEADS = 8
QK_NOPE = 128
QK_ROPE = 64
V_HEAD = 128
ROPE_THETA = 10000.0
EPS = 1e-6

QK_PAD = 256
MXU_DIM = 256
VMEM_LIMIT = 56 * 1024 * 1024

F32 = jnp.float32
BF16 = jnp.bfloat16

NT_DIMS = (((1,), (1,)), ((), ()))


def _const_spec(shape):
    nd = len(shape)
    return pl.BlockSpec(shape, lambda *_: (0,) * nd, pipeline_mode=pl.Buffered(1))


def _adaln_kernel(c_ref, w_ref, b_ref, o_ref):
    c = c_ref[...]
    c_act = (c * jax.nn.sigmoid(c)).astype(BF16)
    o_ref[...] = jnp.dot(c_act, w_ref[...].astype(BF16), preferred_element_type=F32) + b_ref[...]


def _adaln(c_pad, w_ada, b_ada):
    rows, d = c_pad.shape
    n = w_ada.shape[1]
    tn = 1536
    return pl.pallas_call(
        _adaln_kernel,
        grid=(n // tn,),
        in_specs=[
            pl.BlockSpec((rows, d), lambda j: (0, 0)),
            pl.BlockSpec((d, tn), lambda j: (0, j)),
            pl.BlockSpec((1, tn), lambda j: (0, j)),
        ],
        out_specs=pl.BlockSpec((rows, tn), lambda j: (0, j)),
        out_shape=jax.ShapeDtypeStruct((rows, n), F32),
        compiler_params=pltpu.CompilerParams(
            dimension_semantics=("arbitrary",), vmem_limit_bytes=VMEM_LIMIT),
        name="adaln_mod",
    )(c_pad, w_ada, b_ada)


def _split_w_in_kernel(w_ref, main_ref, tail_ref):
    n_main = main_ref.shape[1]
    w = w_ref[...]
    main_ref[...] = w[:, :n_main].astype(BF16)
    pad = jnp.zeros((w.shape[0], tail_ref.shape[1] - (w.shape[1] - n_main)), F32)
    tail_ref[...] = jnp.concatenate(
        [w[:, n_main + QK_ROPE:], w[:, n_main:n_main + QK_ROPE], pad], axis=1).astype(BF16)


def _split_w_in(w_in, n_main):
    d, n = w_in.shape
    n_tail = n - n_main + (128 - QK_ROPE)
    tk = 256
    return pl.pallas_call(
        _split_w_in_kernel,
        grid=(d // tk,),
        in_specs=[pl.BlockSpec((tk, n), lambda i: (i, 0))],
        out_specs=[pl.BlockSpec((tk, n_main), lambda i: (i, 0)),
                   pl.BlockSpec((tk, n_tail), lambda i: (i, 0))],
        out_shape=[jax.ShapeDtypeStruct((d, n_main), BF16),
                   jax.ShapeDtypeStruct((d, n_tail), BF16)],
        compiler_params=pltpu.CompilerParams(
            dimension_semantics=("arbitrary",), vmem_limit_bytes=VMEM_LIMIT),
        name="split_w_in",
    )(w_in)


IN_PROJ_ROWS = 256


def _rms(v, axis):
    return v * lax.rsqrt(jnp.mean(v * v, axis=axis, keepdims=True) + EPS)


def _in_proj_kernel(x_ref, mod_ref, posr_ref, invc_ref, wmain_ref, wtail_ref,
                    qan_ref, kvan_ref, wuqt_ref, wuk_ref, wuvt_ref,
                    gqn_ref, gqr_ref, gkn_ref, gkr_ref,
                    xr_ref, sgr_ref, sga_ref, qt_ref, k_ref, vt_ref, bound_ref, *, q_scale):
    d_rnn, d_att = xr_ref.shape[2], sga_ref.shape[2]
    q_lora, kv_lora = qan_ref.shape[1], kvan_ref.shape[1]
    qc0, kvc0 = 2 * d_rnn, 2 * d_rnn + q_lora
    scale = mod_ref[0, 1:2, :]
    shift = mod_ref[0, 0:1, :]
    hd = QK_NOPE + QK_ROPE
    half = QK_ROPE // 2
    gkn, gkr = gkn_ref[...], gkr_ref[...]
    k_bound = jnp.sqrt(QK_NOPE * jnp.max(gkn * gkn, axis=1, keepdims=True)
                       + QK_ROPE * jnp.max(gkr * gkr, axis=0, keepdims=True))
    pad_row = lax.broadcasted_iota(jnp.int32, (QK_PAD - hd, IN_PROJ_ROWS), 0)

    for r0 in range(0, x_ref.shape[1], IN_PROJ_ROWS):
        rs = slice(r0, r0 + IN_PROJ_ROWS)
        h = (_rms(x_ref[0, rs, :], -1) * (1.0 + scale) + shift).astype(BF16)

        def proj(w_ref, lo, hi):
            return jnp.dot(h, w_ref[:, lo:hi], preferred_element_type=F32)

        qcn = (_rms(proj(wmain_ref, qc0, kvc0), -1) * qan_ref[...]).astype(BF16)
        kvcn = (_rms(proj(wmain_ref, kvc0, kvc0 + kv_lora), -1) * kvan_ref[...]).astype(BF16)
        kr_t = proj(wtail_ref, d_att, wtail_ref.shape[1]).T[:QK_ROPE]
        q_t = lax.dot_general(wuqt_ref[...], qcn, NT_DIMS, preferred_element_type=F32)
        v_t = lax.dot_general(wuvt_ref[...], kvcn, NT_DIMS, preferred_element_type=F32)
        kn_all = jnp.dot(kvcn, wuk_ref[...], preferred_element_type=F32)

        ang_t = invc_ref[...] * posr_ref[0, :, rs]
        cos_t, sin_t = jnp.cos(ang_t), jnp.sin(ang_t)

        def rope_t(v):
            v1, v2 = v[:half], v[half:]
            return v1 * cos_t - v2 * sin_t, v1 * sin_t + v2 * cos_t

        bound_max = None
        for hh in range(MLA_HEADS):
            qn = q_t[hh * hd:hh * hd + QK_NOPE]
            qn = _rms(qn, 0) * (gqn_ref[...] * q_scale)
            qr = q_t[hh * hd + QK_NOPE:(hh + 1) * hd]
            r1, r2 = rope_t(_rms(qr, 0) * (gqr_ref[...] * q_scale))
            q_sq = (jnp.sum(qn * qn, axis=0, keepdims=True)
                    + jnp.sum(r1 * r1 + r2 * r2, axis=0, keepdims=True))
            bound = jnp.sqrt(q_sq) * k_bound
            bound_max = bound if bound_max is None else jnp.maximum(bound_max, bound)
            qt_ref[0, hh, 0:QK_NOPE, rs] = qn.astype(BF16)
            qt_ref[0, hh, QK_NOPE:QK_NOPE + half, rs] = r1.astype(BF16)
            qt_ref[0, hh, QK_NOPE + half:hd, rs] = r2.astype(BF16)
            qt_ref[0, hh, hd:QK_PAD, rs] = jnp.where(pad_row == 0, -bound, 0.0).astype(BF16)
        bound_ref[0, :, rs] = bound_max

        for hh in range(MLA_HEADS):
            vt_ref[0, hh, :, rs] = v_t[hh * V_HEAD:(hh + 1) * V_HEAD].astype(BF16)
        k1, k2 = rope_t(_rms(kr_t, 0) * gkr)
        kpe_t = jnp.concatenate([k1, k2, jnp.where(pad_row == 0, 1.0, 0.0)], axis=0)
        kpe = kpe_t.T.astype(BF16)
        for hh in range(MLA_HEADS):
            kn = kn_all[:, hh * QK_NOPE:(hh + 1) * QK_NOPE]
            k_ref[0, hh, rs, 0:QK_NOPE] = (_rms(kn, -1) * gkn).astype(BF16)
            k_ref[0, hh, rs, QK_NOPE:QK_PAD] = kpe

        xr_ref[0, rs, :] = proj(wmain_ref, 0, d_rnn).astype(BF16)
        gr = proj(wmain_ref, d_rnn, 2 * d_rnn)
        sgr_ref[0, rs, :] = (gr * jax.nn.sigmoid(gr)).astype(BF16)
        ga = proj(wtail_ref, 0, d_att)
        sga_ref[0, rs, :] = (ga * jax.nn.sigmoid(ga)).astype(BF16)


def _in_proj(x, mod, pos_row, inv_col, w, tm):
    bsz, seq, d = x.shape
    hd = QK_NOPE + QK_ROPE
    q_scale = (hd ** -0.5) * math.log2(math.e)
    consts = [w["wmain"], w["wtail"],
              w["qan"], w["kvan"], w["wuqt"], w["wuk"], w["wuvt"],
              w["gqn"], w["gqr"], w["gkn"], w["gkr"]]
    d_rnn, d_att = w["d_rnn"], MLA_HEADS * V_HEAD
    return pl.pallas_call(
        functools.partial(_in_proj_kernel, q_scale=q_scale),
        grid=(bsz, seq // tm),
        in_specs=[
            pl.BlockSpec((1, tm, d), lambda b, i: (b, i, 0)),
            pl.BlockSpec((1, 3, d), lambda b, i: (b, 0, 0)),
            pl.BlockSpec((1, 1, tm), lambda b, i: (b, 0, i)),
            _const_spec(inv_col.shape),
        ] + [_const_spec(a.shape) for a in consts],
        out_specs=[
            pl.BlockSpec((1, tm, d_rnn), lambda b, i: (b, i, 0)),
            pl.BlockSpec((1, tm, d_rnn), lambda b, i: (b, i, 0)),
            pl.BlockSpec((1, tm, d_att), lambda b, i: (b, i, 0)),
            pl.BlockSpec((1, MLA_HEADS, QK_PAD, tm), lambda b, i: (b, 0, 0, i)),
            pl.BlockSpec((1, MLA_HEADS, tm, QK_PAD), lambda b, i: (b, 0, i, 0)),
            pl.BlockSpec((1, MLA_HEADS, V_HEAD, tm), lambda b, i: (b, 0, 0, i)),
            pl.BlockSpec((1, 1, tm), lambda b, i: (b, 0, i)),
        ],
        out_shape=[
            jax.ShapeDtypeStruct((bsz, seq, d_rnn), BF16),
            jax.ShapeDtypeStruct((bsz, seq, d_rnn), BF16),
            jax.ShapeDtypeStruct((bsz, seq, d_att), BF16),
            jax.ShapeDtypeStruct((bsz, MLA_HEADS, QK_PAD, seq), BF16),
            jax.ShapeDtypeStruct((bsz, MLA_HEADS, seq, QK_PAD), BF16),
            jax.ShapeDtypeStruct((bsz, MLA_HEADS, V_HEAD, seq), BF16),
            jax.ShapeDtypeStruct((bsz, 1, seq), F32),
        ],
        compiler_params=pltpu.CompilerParams(
            dimension_semantics=("arbitrary", "arbitrary"), vmem_limit_bytes=VMEM_LIMIT),
        name="in_proj",
    )(x, mod, pos_row, inv_col, *consts)


ATTN_HEADS_PER_STEP = 4
ATTN_KEY_SUB = 256
MAX_SAFE_SCORE_BOUND = 56.0


def _attn_kernel(qt_ref, k_ref, vt_ref, sga_ref, o_ref, acc_ref, *, tq, running_max):
    qi = pl.program_id(2)
    nh = qt_ref.shape[1]
    acc_ref[...] = jnp.zeros_like(acc_ref)

    sub = tq if running_max else ATTN_KEY_SUB

    def first_query(unit, masked):
        return unit[2] * sub if masked[unit[0]] else 0

    def scores(unit, js, masked):
        bi, hh, c = unit
        key0 = pl.multiple_of(js[bi] * tq + c * sub, sub)
        k_sub = k_ref[0, hh, pl.ds(key0, sub), :]
        q_t = qt_ref[0, hh, :, first_query(unit, masked):]
        return jnp.dot(k_sub, q_t, preferred_element_type=F32)

    def accumulate(unit, js, s_t, m, l, masked):
        bi, hh, c = unit
        q0 = first_query(unit, masked)
        if masked[bi]:
            key = lax.broadcasted_iota(jnp.int32, s_t.shape, 0) + c * sub
            qry = lax.broadcasted_iota(jnp.int32, s_t.shape, 1) + q0
            s_t = jnp.where(key <= qry, s_t, -jnp.inf)
        key0 = pl.multiple_of(js[bi] * tq + c * sub, sub)
        v_sub = vt_ref[0, hh, :, pl.ds(key0, sub)]
        if running_max:
            m_new = jnp.maximum(m, jnp.max(s_t, axis=0, keepdims=True))
            alpha = jnp.exp2(m - m_new)
            p_t = jnp.exp2(s_t - m_new)
            l_new = alpha * l + jnp.sum(p_t, axis=0, keepdims=True)
            pv = jnp.dot(v_sub, p_t.astype(BF16), preferred_element_type=F32)
            acc_ref[hh] = alpha * acc_ref[hh] + pv
        else:
            m_new = m
            p_t = jnp.exp2(s_t)
            l_part = l[:, q0:] + jnp.sum(p_t, axis=0, keepdims=True)
            l_new = l_part if q0 == 0 else jnp.concatenate([l[:, :q0], l_part], axis=1)
            acc_ref[hh, :, q0:] += jnp.dot(v_sub, p_t.astype(BF16), preferred_element_type=F32)
        return m_new, l_new

    def blocks(js, masked, carry):
        units = [(bi, hh, c) for bi in range(len(js)) for hh in range(nh)
                 for c in range(tq // sub)]
        carry = list(carry)
        s_next = scores(units[0], js, masked)
        for idx, unit in enumerate(units):
            s_cur = s_next
            if idx + 1 < len(units):
                s_next = scores(units[idx + 1], js, masked)
            hh = unit[1]
            carry[hh] = accumulate(unit, js, s_cur, carry[hh][0], carry[hh][1], masked)
        return tuple(carry)

    carry = tuple((jnp.full((1, tq), -jnp.inf, F32), jnp.zeros((1, tq), F32)) for _ in range(nh))
    if running_max:
        carry = lax.fori_loop(0, qi, lambda j, c: blocks([j], [False], c), carry)
        carry = blocks([qi], [True], carry)
    else:
        carry = lax.fori_loop(
            0, qi >> 1, lambda p, c: blocks([2 * p, 2 * p + 1], [False, False], c), carry)
        carry = lax.cond(
            (qi & 1) == 1,
            functools.partial(blocks, [qi - 1, qi], [False, True]),
            functools.partial(blocks, [qi], [True]),
            carry)

    for hh in range(nh):
        o_t = acc_ref[hh] / carry[hh][1]
        gate = sga_ref[0, :, hh * V_HEAD:(hh + 1) * V_HEAD].astype(F32)
        o_ref[0, :, hh * V_HEAD:(hh + 1) * V_HEAD] = (o_t.T * gate).astype(BF16)


def _attention(q_t, k, v_t, sga, tq, running_max):
    bsz, nh, _, seq = q_t.shape
    hb = ATTN_HEADS_PER_STEP
    return pl.pallas_call(
        functools.partial(_attn_kernel, tq=tq, running_max=running_max),
        grid=(bsz, nh // hb, seq // tq),
        in_specs=[
            pl.BlockSpec((1, hb, QK_PAD, tq), lambda b, h, i: (b, h, 0, i)),
            pl.BlockSpec((1, hb, seq, QK_PAD), lambda b, h, i: (b, h, 0, 0)),
            pl.BlockSpec((1, hb, V_HEAD, seq), lambda b, h, i: (b, h, 0, 0)),
            pl.BlockSpec((1, tq, hb * V_HEAD), lambda b, h, i: (b, i, h)),
        ],
        out_specs=pl.BlockSpec((1, tq, hb * V_HEAD), lambda b, h, i: (b, i, h)),
        out_shape=jax.ShapeDtypeStruct((bsz, seq, nh * V_HEAD), BF16),
        scratch_shapes=[pltpu.VMEM((hb, V_HEAD, tq), F32)],
        compiler_params=pltpu.CompilerParams(
            dimension_semantics=("arbitrary", "arbitrary", "arbitrary"),
            vmem_limit_bytes=VMEM_LIMIT),
        name="mla_attention_online" if running_max else "mla_attention",
    )(q_t, k, v_t, sga)


SUBLANES = 8


def _rglru_kernel(xr_ref, sgr_ref, cw_ref, cb_ref, wa_ref, ba_ref, wx_ref, bx_ref, lam_ref,
                  o_ref, xbuf_ref, a_ref, b_ref, hc_ref, *, ts):
    @pl.when(pl.program_id(1) == 0)
    def _():
        xbuf_ref[0:SUBLANES, :] = jnp.zeros((SUBLANES, xbuf_ref.shape[1]), F32)
        hc_ref[...] = jnp.zeros_like(hc_ref)

    xbuf_ref[SUBLANES:SUBLANES + ts, :] = xr_ref[0].astype(F32)
    xc = cb_ref[...]
    for kk in range(CONV_WIDTH):
        off = SUBLANES - (CONV_WIDTH - 1) + kk
        xc = xc + cw_ref[kk:kk + 1, :] * xbuf_ref[off:off + ts, :]
    xbuf_ref[0:SUBLANES, :] = xbuf_ref[ts:ts + SUBLANES, :]

    xcb = xc.astype(BF16)
    n_grp = wa_ref.shape[0]

    def gate(w_ref, bias_ref):
        parts = [jnp.dot(xcb[:, g * MXU_DIM:(g + 1) * MXU_DIM], w_ref[g],
                         preferred_element_type=F32) for g in range(n_grp)]
        return jax.nn.sigmoid(jnp.concatenate(parts, axis=1) + bias_ref[...])

    r = gate(wa_ref, ba_ref)
    i = gate(wx_ref, bx_ref)
    nl = -lam_ref[...]
    log_a_unit = (-LRU_C) * (jnp.maximum(nl, 0.0) + jnp.log1p(jnp.exp(-jnp.abs(nl))))
    a_ref[...] = jnp.exp2(r * (log_a_unit * math.log2(math.e)))
    t = jnp.tanh(r * log_a_unit)
    u = -2.0 * t
    coef = jnp.where(u > 0.0, u * lax.rsqrt(u * (1.0 - t)), 0.0)
    b_ref[...] = coef * (i * xc)

    row = lax.broadcasted_iota(jnp.int32, (SUBLANES, a_ref.shape[1]), 0)

    def scan_group(g, h_prev):
        base = pl.multiple_of(g * SUBLANES, SUBLANES)
        a = a_ref[pl.ds(base, SUBLANES), :]
        b = b_ref[pl.ds(base, SUBLANES), :]
        for dist in (1, 2, 4):
            keep = row >= dist
            b = jnp.where(keep, a * pltpu.roll(b, dist, 0) + b, b)
            a = jnp.where(keep, a * pltpu.roll(a, dist, 0), a)
        hg = a * h_prev + b
        b_ref[pl.ds(base, SUBLANES), :] = hg
        return jnp.broadcast_to(hg[SUBLANES - 1:SUBLANES, :], hg.shape)

    hc_ref[...] = lax.fori_loop(0, ts // SUBLANES, scan_group, hc_ref[...], unroll=4)
    o_ref[0] = (b_ref[...] * sgr_ref[0].astype(F32)).astype(BF16)


def _rglru(xr, sgr, cw, cb, wa, ba, wx, bx, lam, ts):
    bsz, seq, c = xr.shape
    consts = [cw, cb, wa, ba, wx, bx, lam]
    return pl.pallas_call(
        functools.partial(_rglru_kernel, ts=ts),
        grid=(bsz, seq // ts),
        in_specs=[
            pl.BlockSpec((1, ts, c), lambda b, i: (b, i, 0)),
            pl.BlockSpec((1, ts, c), lambda b, i: (b, i, 0)),
        ] + [_const_spec(a.shape) for a in consts],
        out_specs=pl.BlockSpec((1, ts, c), lambda b, i: (b, i, 0)),
        out_shape=jax.ShapeDtypeStruct((bsz, seq, c), BF16),
        scratch_shapes=[
            pltpu.VMEM((ts + SUBLANES, c), F32),
            pltpu.VMEM((ts, c), F32),
            pltpu.VMEM((ts, c), F32),
            pltpu.VMEM((SUBLANES, c), F32),
        ],
        compiler_params=pltpu.CompilerParams(
            dimension_semantics=("arbitrary", "arbitrary"), vmem_limit_bytes=VMEM_LIMIT),
        name="rglru",
    )(xr, sgr, *consts)


def _out_proj_kernel(x_ref, yr_ref, ya_ref, w_ref, mod_ref, o_ref):
    d_rnn = yr_ref.shape[2]
    y = jnp.dot(yr_ref[0], w_ref[0:d_rnn, :], preferred_element_type=F32)
    y = y + jnp.dot(ya_ref[0], w_ref[d_rnn:, :], preferred_element_type=F32)
    o_ref[0] = x_ref[0] + mod_ref[0, 2:3, :] * y


def _out_proj(x, y_rnn, y_att, w_o, mod, tm):
    bsz, seq, d = x.shape
    return pl.pallas_call(
        _out_proj_kernel,
        grid=(bsz, seq // tm),
        in_specs=[
            pl.BlockSpec((1, tm, d), lambda b, i: (b, i, 0)),
            pl.BlockSpec((1, tm, y_rnn.shape[2]), lambda b, i: (b, i, 0)),
            pl.BlockSpec((1, tm, y_att.shape[2]), lambda b, i: (b, i, 0)),
            _const_spec(w_o.shape),
            pl.BlockSpec((1, 3, d), lambda b, i: (b, 0, 0)),
        ],
        out_specs=pl.BlockSpec((1, tm, d), lambda b, i: (b, i, 0)),
        out_shape=jax.ShapeDtypeStruct((bsz, seq, d), F32),
        compiler_params=pltpu.CompilerParams(
            dimension_semantics=("arbitrary", "arbitrary"), vmem_limit_bytes=VMEM_LIMIT),
        name="out_proj",
    )(x, y_rnn, y_att, w_o, mod)


def _block_diag_groups(w):
    nb, n, _ = w.shape
    per = MXU_DIM // n
    w = w.reshape(nb // per, per, n, n)
    eye = jnp.eye(per, dtype=w.dtype)
    return jnp.einsum("gpij,pq->gpiqj", w, eye).reshape(nb // per, MXU_DIM, MXU_DIM)


def kernel(x, c, positions, w_ada, b_ada, w_in, conv_w, conv_b, w_rg_a, b_rg_a, w_rg_x, b_rg_x,
           lru_lambda, q_a_norm, w_uq, kv_a_norm, w_ukv, q_norm_nope, q_norm_rope, k_norm_nope,
           k_norm_rope, w_out):
    bsz, seq, d = x.shape
    depth = w_in.shape[0]
    d_rnn = conv_w.shape[2]
    q_lora = q_a_norm.shape[1]
    kv_lora = kv_a_norm.shape[1]
    tile = 512

    inv_freq = 1.0 / (ROPE_THETA ** (jnp.arange(0, QK_ROPE, 2, dtype=F32) / QK_ROPE))
    pos = positions.astype(F32)
    pos_row = pos.reshape(bsz, 1, seq)
    inv_col = inv_freq.reshape(-1, 1)
    c_pad = jnp.zeros((SUBLANES, d), F32).at[:bsz].set(c)

    for l in range(depth):
        mod = _adaln(c_pad, w_ada[l], b_ada[l].reshape(1, -1))[:bsz].reshape(bsz, 3, d)

        n_main = 2 * d_rnn + q_lora + kv_lora
        w_main, w_tail = _split_w_in(w_in[l], n_main)
        ukv = w_ukv[l].reshape(kv_lora, MLA_HEADS, QK_NOPE + V_HEAD)
        w = dict(
            d_rnn=d_rnn, wmain=w_main, wtail=w_tail,
            qan=q_a_norm[l].reshape(1, -1), kvan=kv_a_norm[l].reshape(1, -1),
            wuqt=w_uq[l].T.astype(BF16),
            wuk=ukv[:, :, :QK_NOPE].reshape(kv_lora, -1).astype(BF16),
            wuvt=ukv[:, :, QK_NOPE:].reshape(kv_lora, -1).T.astype(BF16),
            gqn=q_norm_nope[l].reshape(-1, 1), gqr=q_norm_rope[l].reshape(-1, 1),
            gkn=k_norm_nope[l].reshape(1, -1), gkr=k_norm_rope[l].reshape(-1, 1),
        )
        xr, sgr, sga, q_t, k, v_t, bound = _in_proj(x, mod, pos_row, inv_col, w, tile)

        y_att = lax.cond(
            jnp.max(bound) < MAX_SAFE_SCORE_BOUND,
            functools.partial(_attention, tq=tile, running_max=False),
            functools.partial(_attention, tq=tile, running_max=True),
            q_t, k, v_t, sga)
        y_rnn = _rglru(
            xr, sgr, conv_w[l], conv_b[l].reshape(1, -1),
            _block_diag_groups(w_rg_a[l]).astype(BF16), b_rg_a[l].reshape(1, -1),
            _block_diag_groups(w_rg_x[l]).astype(BF16), b_rg_x[l].reshape(1, -1),
            lru_lambda[l].reshape(1, -1), tile)

        x = _out_proj(x, y_rnn, y_att, w_out[l].astype(BF16), mod, tile)
    return x
```

```python
import functools
import math

import jax
import jax.numpy as jnp
from jax import lax
from jax.experimental import pallas as pl
from jax.experimental.pallas import tpu as pltpu

RNN_BLOCKS = 16
CONV_WIDTH = 4
LRU_C = 8.0
MLA_HEADS = 8
QK_NOPE = 128
QK_ROPE = 64
V_HEAD = 128
ROPE_THETA = 10000.0
EPS = 1e-6

QK_PAD = 256
MXU_DIM = 256
VMEM_LIMIT = 56 * 1024 * 1024

F32 = jnp.float32
BF16 = jnp.bfloat16

NT_DIMS = (((1,), (1,)), ((), ()))


def _const_spec(shape):
    nd = len(shape)
    return pl.BlockSpec(shape, lambda *_: (0,) * nd, pipeline_mode=pl.Buffered(1))


def _adaln_kernel(c_ref, w_ref, b_ref, o_ref):
    c = c_ref[...]
    c_act = (c * jax.nn.sigmoid(c)).astype(BF16)
    o_ref[...] = jnp.dot(c_act, w_ref[...].astype(BF16), preferred_element_type=F32) + b_ref[...]


def _adaln(c_pad, w_ada, b_ada):
    rows, d = c_pad.shape
    n = w_ada.shape[1]
    tn = 1536
    return pl.pallas_call(
        _adaln_kernel,
        grid=(n // tn,),
        in_specs=[
            pl.BlockSpec((rows, d), lambda j: (0, 0)),
            pl.BlockSpec((d, tn), lambda j: (0, j)),
            pl.BlockSpec((1, tn), lambda j: (0, j)),
        ],
        out_specs=pl.BlockSpec((rows, tn), lambda j: (0, j)),
        out_shape=jax.ShapeDtypeStruct((rows, n), F32),
        compiler_params=pltpu.CompilerParams(
            dimension_semantics=("arbitrary",), vmem_limit_bytes=VMEM_LIMIT),
        name="adaln_mod",
    )(c_pad, w_ada, b_ada)


IN_PROJ_ROWS = 256


def _rms(v, axis):
    return v * lax.rsqrt(jnp.mean(v * v, axis=axis, keepdims=True) + EPS)


def _in_proj_kernel(x_ref, mod_ref, posr_ref, invc_ref, wt_ref,
                    qan_ref, kvan_ref, wuqt_ref, wuk_ref, wuvt_ref,
                    gqn_ref, gqr_ref, gkn_ref, gkr_ref,
                    xr_ref, sgr_ref, sga_ref, qt_ref, k_ref, vt_ref, bound_ref, *, q_scale):
    d_rnn, d_att = xr_ref.shape[2], sga_ref.shape[2]
    q_lora, kv_lora = qan_ref.shape[1], kvan_ref.shape[1]
    qc0, kvc0 = 2 * d_rnn, 2 * d_rnn + q_lora
    kr0 = kvc0 + kv_lora
    ga0 = kr0 + QK_ROPE
    scale = mod_ref[0, 1:2, :]
    shift = mod_ref[0, 0:1, :]
    hd = QK_NOPE + QK_ROPE
    half = QK_ROPE // 2
    gkn, gkr = gkn_ref[...], gkr_ref[...]
    k_bound = jnp.sqrt(QK_NOPE * jnp.max(gkn * gkn, axis=1, keepdims=True)
                       + QK_ROPE * jnp.max(gkr * gkr, axis=0, keepdims=True))
    pad_row = lax.broadcasted_iota(jnp.int32, (QK_PAD - hd, IN_PROJ_ROWS), 0)

    for r0 in range(0, x_ref.shape[1], IN_PROJ_ROWS):
        rs = slice(r0, r0 + IN_PROJ_ROWS)
        h = (_rms(x_ref[0, rs, :], -1) * (1.0 + scale) + shift).astype(BF16)

        def proj(lo, hi):
            return lax.dot_general(h, wt_ref[lo:hi, :], NT_DIMS, preferred_element_type=F32)

        qcn = (_rms(proj(qc0, kvc0), -1) * qan_ref[...]).astype(BF16)
        kvcn = (_rms(proj(kvc0, kr0), -1) * kvan_ref[...]).astype(BF16)
        kr_t = proj(kr0, kr0 + 128).T[:QK_ROPE]
        q_t = lax.dot_general(wuqt_ref[...], qcn, NT_DIMS, preferred_element_type=F32)
        v_t = lax.dot_general(wuvt_ref[...], kvcn, NT_DIMS, preferred_element_type=F32)
        kn_all = jnp.dot(kvcn, wuk_ref[...], preferred_element_type=F32)

        ang_t = invc_ref[...] * posr_ref[0, :, rs]
        cos_t, sin_t = jnp.cos(ang_t), jnp.sin(ang_t)

        def rope_t(v):
            v1, v2 = v[:half], v[half:]
            return v1 * cos_t - v2 * sin_t, v1 * sin_t + v2 * cos_t

        bound_max = None
        for hh in range(MLA_HEADS):
            qn = q_t[hh * hd:hh * hd + QK_NOPE]
            qn = _rms(qn, 0) * (gqn_ref[...] * q_scale)
            qr = q_t[hh * hd + QK_NOPE:(hh + 1) * hd]
            r1, r2 = rope_t(_rms(qr, 0) * (gqr_ref[...] * q_scale))
            q_sq = (jnp.sum(qn * qn, axis=0, keepdims=True)
                    + jnp.sum(r1 * r1 + r2 * r2, axis=0, keepdims=True))
            bound = jnp.sqrt(q_sq) * k_bound
            bound_max = bound if bound_max is None else jnp.maximum(bound_max, bound)
            qt_ref[0, hh, 0:QK_NOPE, rs] = qn.astype(BF16)
            qt_ref[0, hh, QK_NOPE:QK_NOPE + half, rs] = r1.astype(BF16)
            qt_ref[0, hh, QK_NOPE + half:hd, rs] = r2.astype(BF16)
            qt_ref[0, hh, hd:QK_PAD, rs] = jnp.where(pad_row == 0, -bound, 0.0).astype(BF16)
        bound_ref[0, :, rs] = bound_max

        for hh in range(MLA_HEADS):
            vt_ref[0, hh, :, rs] = v_t[hh * V_HEAD:(hh + 1) * V_HEAD].astype(BF16)
        k1, k2 = rope_t(_rms(kr_t, 0) * gkr)
        kpe_t = jnp.concatenate([k1, k2, jnp.where(pad_row == 0, 1.0, 0.0)], axis=0)
        kpe = kpe_t.T.astype(BF16)
        for hh in range(MLA_HEADS):
            kn = kn_all[:, hh * QK_NOPE:(hh + 1) * QK_NOPE]
            k_ref[0, hh, rs, 0:QK_NOPE] = (_rms(kn, -1) * gkn).astype(BF16)
            k_ref[0, hh, rs, QK_NOPE:QK_PAD] = kpe

        xr_ref[0, rs, :] = proj(0, d_rnn).astype(BF16)
        gr = proj(d_rnn, 2 * d_rnn)
        sgr_ref[0, rs, :] = (gr * jax.nn.sigmoid(gr)).astype(BF16)
        ga = proj(ga0, ga0 + d_att)
        sga_ref[0, rs, :] = (ga * jax.nn.sigmoid(ga)).astype(BF16)


def _in_proj(x, mod, pos_row, inv_col, w, tm):
    bsz, seq, d = x.shape
    hd = QK_NOPE + QK_ROPE
    q_scale = (hd ** -0.5) * math.log2(math.e)
    consts = [w["wt"],
              w["qan"], w["kvan"], w["wuqt"], w["wuk"], w["wuvt"],
              w["gqn"], w["gqr"], w["gkn"], w["gkr"]]
    d_rnn, d_att = w["d_rnn"], MLA_HEADS * V_HEAD
    return pl.pallas_call(
        functools.partial(_in_proj_kernel, q_scale=q_scale),
        grid=(bsz, seq // tm),
        in_specs=[
            pl.BlockSpec((1, tm, d), lambda b, i: (b, i, 0)),
            pl.BlockSpec((1, 3, d), lambda b, i: (b, 0, 0)),
            pl.BlockSpec((1, 1, tm), lambda b, i: (b, 0, i)),
            _const_spec(inv_col.shape),
        ] + [_const_spec(a.shape) for a in consts],
        out_specs=[
            pl.BlockSpec((1, tm, d_rnn), lambda b, i: (b, i, 0)),
            pl.BlockSpec((1, tm, d_rnn), lambda b, i: (b, i, 0)),
            pl.BlockSpec((1, tm, d_att), lambda b, i: (b, i, 0)),
            pl.BlockSpec((1, MLA_HEADS, QK_PAD, tm), lambda b, i: (b, 0, 0, i)),
            pl.BlockSpec((1, MLA_HEADS, tm, QK_PAD), lambda b, i: (b, 0, i, 0)),
            pl.BlockSpec((1, MLA_HEADS, V_HEAD, tm), lambda b, i: (b, 0, 0, i)),
            pl.BlockSpec((1, 1, tm), lambda b, i: (b, 0, i)),
        ],
        out_shape=[
            jax.ShapeDtypeStruct((bsz, seq, d_rnn), BF16),
            jax.ShapeDtypeStruct((bsz, seq, d_rnn), BF16),
            jax.ShapeDtypeStruct((bsz, seq, d_att), BF16),
            jax.ShapeDtypeStruct((bsz, MLA_HEADS, QK_PAD, seq), BF16),
            jax.ShapeDtypeStruct((bsz, MLA_HEADS, seq, QK_PAD), BF16),
            jax.ShapeDtypeStruct((bsz, MLA_HEADS, V_HEAD, seq), BF16),
            jax.ShapeDtypeStruct((bsz, 1, seq), F32),
        ],
        compiler_params=pltpu.CompilerParams(
            dimension_semantics=("arbitrary", "arbitrary"), vmem_limit_bytes=VMEM_LIMIT),
        name="in_proj",
    )(x, mod, pos_row, inv_col, *consts)


ATTN_HEADS_PER_STEP = 4
ATTN_KEY_SUB = 256
MAX_SAFE_SCORE_BOUND = 56.0


def _attn_kernel(qt_ref, k_ref, vt_ref, sga_ref, o_ref, acc_ref, *, tq, running_max):
    qi = pl.program_id(2)
    nh = qt_ref.shape[1]
    acc_ref[...] = jnp.zeros_like(acc_ref)

    sub = tq if running_max else ATTN_KEY_SUB

    def first_query(unit, masked):
        return unit[2] * sub if masked[unit[0]] else 0

    def scores(unit, js, masked):
        bi, hh, c = unit
        key0 = pl.multiple_of(js[bi] * tq + c * sub, sub)
        k_sub = k_ref[0, hh, pl.ds(key0, sub), :]
        q_t = qt_ref[0, hh, :, first_query(unit, masked):]
        return jnp.dot(k_sub, q_t, preferred_element_type=F32)

    def accumulate(unit, js, s_t, m, l, masked):
        bi, hh, c = unit
        q0 = first_query(unit, masked)
        if masked[bi]:
            key = lax.broadcasted_iota(jnp.int32, s_t.shape, 0) + c * sub
            qry = lax.broadcasted_iota(jnp.int32, s_t.shape, 1) + q0
            s_t = jnp.where(key <= qry, s_t, -jnp.inf)
        key0 = pl.multiple_of(js[bi] * tq + c * sub, sub)
        v_sub = vt_ref[0, hh, :, pl.ds(key0, sub)]
        if running_max:
            m_new = jnp.maximum(m, jnp.max(s_t, axis=0, keepdims=True))
            alpha = jnp.exp2(m - m_new)
            p_t = jnp.exp2(s_t - m_new)
            l_new = alpha * l + jnp.sum(p_t, axis=0, keepdims=True)
            pv = jnp.dot(v_sub, p_t.astype(BF16), preferred_element_type=F32)
            acc_ref[hh] = alpha * acc_ref[hh] + pv
        else:
            m_new = m
            p_t = jnp.exp2(s_t)
            l_part = l[:, q0:] + jnp.sum(p_t, axis=0, keepdims=True)
            l_new = l_part if q0 == 0 else jnp.concatenate([l[:, :q0], l_part], axis=1)
            acc_ref[hh, :, q0:] += jnp.dot(v_sub, p_t.astype(BF16), preferred_element_type=F32)
        return m_new, l_new

    def blocks(js, masked, carry):
        units = [(bi, hh, c) for bi in range(len(js)) for hh in range(nh)
                 for c in range(tq // sub)]
        carry = list(carry)
        s_next = scores(units[0], js, masked)
        for idx, unit in enumerate(units):
            s_cur = s_next
            if idx + 1 < len(units):
                s_next = scores(units[idx + 1], js, masked)
            hh = unit[1]
            carry[hh] = accumulate(unit, js, s_cur, carry[hh][0], carry[hh][1], masked)
        return tuple(carry)

    carry = tuple((jnp.full((1, tq), -jnp.inf, F32), jnp.zeros((1, tq), F32)) for _ in range(nh))
    if running_max:
        carry = lax.fori_loop(0, qi, lambda j, c: blocks([j], [False], c), carry)
        carry = blocks([qi], [True], carry)
    else:
        carry = lax.fori_loop(
            0, qi >> 1, lambda p, c: blocks([2 * p, 2 * p + 1], [False, False], c), carry)
        carry = lax.cond(
            (qi & 1) == 1,
            functools.partial(blocks, [qi - 1, qi], [False, True]),
            functools.partial(blocks, [qi], [True]),
            carry)

    for hh in range(nh):
        o_t = acc_ref[hh] / carry[hh][1]
        gate = sga_ref[0, :, hh * V_HEAD:(hh + 1) * V_HEAD].astype(F32)
        o_ref[0, :, hh * V_HEAD:(hh + 1) * V_HEAD] = (o_t.T * gate).astype(BF16)


def _attention(q_t, k, v_t, sga, tq, running_max):
    bsz, nh, _, seq = q_t.shape
    hb = ATTN_HEADS_PER_STEP
    return pl.pallas_call(
        functools.partial(_attn_kernel, tq=tq, running_max=running_max),
        grid=(bsz, nh // hb, seq // tq),
        in_specs=[
            pl.BlockSpec((1, hb, QK_PAD, tq), lambda b, h, i: (b, h, 0, i)),
            pl.BlockSpec((1, hb, seq, QK_PAD), lambda b, h, i: (b, h, 0, 0)),
            pl.BlockSpec((1, hb, V_HEAD, seq), lambda b, h, i: (b, h, 0, 0)),
            pl.BlockSpec((1, tq, hb * V_HEAD), lambda b, h, i: (b, i, h)),
        ],
        out_specs=pl.BlockSpec((1, tq, hb * V_HEAD), lambda b, h, i: (b, i, h)),
        out_shape=jax.ShapeDtypeStruct((bsz, seq, nh * V_HEAD), BF16),
        scratch_shapes=[pltpu.VMEM((hb, V_HEAD, tq), F32)],
        compiler_params=pltpu.CompilerParams(
            dimension_semantics=("arbitrary", "arbitrary", "arbitrary"),
            vmem_limit_bytes=VMEM_LIMIT),
        name="mla_attention_online" if running_max else "mla_attention",
    )(q_t, k, v_t, sga)


SUBLANES = 8


def _rglru_kernel(xr_ref, sgr_ref, cw_ref, cb_ref, wa_ref, ba_ref, wx_ref, bx_ref, lam_ref,
                  o_ref, xbuf_ref, a_ref, b_ref, hc_ref, *, ts):
    @pl.when(pl.program_id(1) == 0)
    def _():
        xbuf_ref[0:SUBLANES, :] = jnp.zeros((SUBLANES, xbuf_ref.shape[1]), F32)
        hc_ref[...] = jnp.zeros_like(hc_ref)

    xbuf_ref[SUBLANES:SUBLANES + ts, :] = xr_ref[0].astype(F32)
    xc = cb_ref[...]
    for kk in range(CONV_WIDTH):
        off = SUBLANES - (CONV_WIDTH - 1) + kk
        xc = xc + cw_ref[kk:kk + 1, :] * xbuf_ref[off:off + ts, :]
    xbuf_ref[0:SUBLANES, :] = xbuf_ref[ts:ts + SUBLANES, :]

    xcb = xc.astype(BF16)
    n_grp = wa_ref.shape[0]

    def gate(w_ref, bias_ref):
        parts = [jnp.dot(xcb[:, g * MXU_DIM:(g + 1) * MXU_DIM], w_ref[g],
                         preferred_element_type=F32) for g in range(n_grp)]
        return jax.nn.sigmoid(jnp.concatenate(parts, axis=1) + bias_ref[...])

    r = gate(wa_ref, ba_ref)
    i = gate(wx_ref, bx_ref)
    nl = -lam_ref[...]
    log_a_unit = (-LRU_C) * (jnp.maximum(nl, 0.0) + jnp.log1p(jnp.exp(-jnp.abs(nl))))
    a_ref[...] = jnp.exp2(r * (log_a_unit * math.log2(math.e)))
    t = jnp.tanh(r * log_a_unit)
    u = -2.0 * t
    coef = jnp.where(u > 0.0, u * lax.rsqrt(u * (1.0 - t)), 0.0)
    b_ref[...] = coef * (i * xc)

    row = lax.broadcasted_iota(jnp.int32, (SUBLANES, a_ref.shape[1]), 0)

    def scan_group(g, h_prev):
        base = pl.multiple_of(g * SUBLANES, SUBLANES)
        a = a_ref[pl.ds(base, SUBLANES), :]
        b = b_ref[pl.ds(base, SUBLANES), :]
        for dist in (1, 2, 4):
            keep = row >= dist
            b = jnp.where(keep, a * pltpu.roll(b, dist, 0) + b, b)
            a = jnp.where(keep, a * pltpu.roll(a, dist, 0), a)
        hg = a * h_prev + b
        b_ref[pl.ds(base, SUBLANES), :] = hg
        return jnp.broadcast_to(hg[SUBLANES - 1:SUBLANES, :], hg.shape)

    hc_ref[...] = lax.fori_loop(0, ts // SUBLANES, scan_group, hc_ref[...], unroll=4)
    o_ref[0] = (b_ref[...] * sgr_ref[0].astype(F32)).astype(BF16)


def _rglru(xr, sgr, cw, cb, wa, ba, wx, bx, lam, ts):
    bsz, seq, c = xr.shape
    consts = [cw, cb, wa, ba, wx, bx, lam]
    return pl.pallas_call(
        functools.partial(_rglru_kernel, ts=ts),
        grid=(bsz, seq // ts),
        in_specs=[
            pl.BlockSpec((1, ts, c), lambda b, i: (b, i, 0)),
            pl.BlockSpec((1, ts, c), lambda b, i: (b, i, 0)),
        ] + [_const_spec(a.shape) for a in consts],
        out_specs=pl.BlockSpec((1, ts, c), lambda b, i: (b, i, 0)),
        out_shape=jax.ShapeDtypeStruct((bsz, seq, c), BF16),
        scratch_shapes=[
            pltpu.VMEM((ts + SUBLANES, c), F32),
            pltpu.VMEM((ts, c), F32),
            pltpu.VMEM((ts, c), F32),
            pltpu.VMEM((SUBLANES, c), F32),
        ],
        compiler_params=pltpu.CompilerParams(
            dimension_semantics=("arbitrary", "arbitrary"), vmem_limit_bytes=VMEM_LIMIT),
        name="rglru",
    )(xr, sgr, *consts)


def _out_proj_kernel(x_ref, yr_ref, ya_ref, w_ref, mod_ref, o_ref):
    d_rnn = yr_ref.shape[2]
    y = jnp.dot(yr_ref[0], w_ref[0:d_rnn, :], preferred_element_type=F32)
    y = y + jnp.dot(ya_ref[0], w_ref[d_rnn:, :], preferred_element_type=F32)
    o_ref[0] = x_ref[0] + mod_ref[0, 2:3, :] * y


def _out_proj(x, y_rnn, y_att, w_o, mod, tm):
    bsz, seq, d = x.shape
    return pl.pallas_call(
        _out_proj_kernel,
        grid=(bsz, seq // tm),
        in_specs=[
            pl.BlockSpec((1, tm, d), lambda b, i: (b, i, 0)),
            pl.BlockSpec((1, tm, y_rnn.shape[2]), lambda b, i: (b, i, 0)),
            pl.BlockSpec((1, tm, y_att.shape[2]), lambda b, i: (b, i, 0)),
            _const_spec(w_o.shape),
            pl.BlockSpec((1, 3, d), lambda b, i: (b, 0, 0)),
        ],
        out_specs=pl.BlockSpec((1, tm, d), lambda b, i: (b, i, 0)),
        out_shape=jax.ShapeDtypeStruct((bsz, seq, d), F32),
        compiler_params=pltpu.CompilerParams(
            dimension_semantics=("arbitrary", "arbitrary"), vmem_limit_bytes=VMEM_LIMIT),
        name="out_proj",
    )(x, y_rnn, y_att, w_o, mod)


def _block_diag_groups(w):
    nb, n, _ = w.shape
    per = MXU_DIM // n
    w = w.reshape(nb // per, per, n, n)
    eye = jnp.eye(per, dtype=w.dtype)
    return jnp.einsum("gpij,pq->gpiqj", w, eye).reshape(nb // per, MXU_DIM, MXU_DIM)


def kernel(x, c, positions, w_ada, b_ada, w_in, conv_w, conv_b, w_rg_a, b_rg_a, w_rg_x, b_rg_x,
           lru_lambda, q_a_norm, w_uq, kv_a_norm, w_ukv, q_norm_nope, q_norm_rope, k_norm_nope,
           k_norm_rope, w_out):
    bsz, seq, d = x.shape
    depth = w_in.shape[0]
    d_rnn = conv_w.shape[2]
    q_lora = q_a_norm.shape[1]
    kv_lora = kv_a_norm.shape[1]
    tile = 512

    inv_freq = 1.0 / (ROPE_THETA ** (jnp.arange(0, QK_ROPE, 2, dtype=F32) / QK_ROPE))
    pos = positions.astype(F32)
    pos_row = pos.reshape(bsz, 1, seq)
    inv_col = inv_freq.reshape(-1, 1)
    c_pad = jnp.zeros((SUBLANES, d), F32).at[:bsz].set(c)

    for l in range(depth):
        mod = _adaln(c_pad, w_ada[l], b_ada[l].reshape(1, -1))[:bsz].reshape(bsz, 3, d)

        ukv = w_ukv[l].reshape(kv_lora, MLA_HEADS, QK_NOPE + V_HEAD)
        w = dict(
            d_rnn=d_rnn, wt=jnp.swapaxes(w_in[l], 0, 1).astype(BF16),
            qan=q_a_norm[l].reshape(1, -1), kvan=kv_a_norm[l].reshape(1, -1),
            wuqt=w_uq[l].T.astype(BF16),
            wuk=ukv[:, :, :QK_NOPE].reshape(kv_lora, -1).astype(BF16),
            wuvt=ukv[:, :, QK_NOPE:].reshape(kv_lora, -1).T.astype(BF16),
            gqn=q_norm_nope[l].reshape(-1, 1), gqr=q_norm_rope[l].reshape(-1, 1),
            gkn=k_norm_nope[l].reshape(1, -1), gkr=k_norm_rope[l].reshape(-1, 1),
        )
        xr, sgr, sga, q_t, k, v_t, bound = _in_proj(x, mod, pos_row, inv_col, w, tile)

        y_att = lax.cond(
            jnp.max(bound) < MAX_SAFE_SCORE_BOUND,
            functools.partial(_attention, tq=tile, running_max=False),
            functools.partial(_attention, tq=tile, running_max=True),
            q_t, k, v_t, sga)
        y_rnn = _rglru(
            xr, sgr, conv_w[l], conv_b[l].reshape(1, -1),
            _block_diag_groups(w_rg_a[l]).astype(BF16), b_rg_a[l].reshape(1, -1),
            _block_diag_groups(w_rg_x[l]).astype(BF16), b_rg_x[l].reshape(1, -1),
            lru_lambda[l].reshape(1, -1), tile)

        x = _out_proj(x, y_rnn, y_att, w_out[l].astype(BF16), mod, tile)
    return x
```

```python
import functools
import math

import jax
import jax.numpy as jnp
from jax import lax
from jax.experimental import pallas as pl
from jax.experimental.pallas import tpu as pltpu

RNN_BLOCKS = 16
CONV_WIDTH = 4
LRU_C = 8.0
MLA_HEADS = 8
QK_NOPE = 128
QK_ROPE = 64
V_HEAD = 128
ROPE_THETA = 10000.0
EPS = 1e-6

QK_PAD = 256
MXU_DIM = 256
VMEM_LIMIT = 56 * 1024 * 1024

F32 = jnp.float32
BF16 = jnp.bfloat16

NT_DIMS = (((1,), (1,)), ((), ()))


def _const_spec(shape):
    nd = len(shape)
    return pl.BlockSpec(shape, lambda *_: (0,) * nd, pipeline_mode=pl.Buffered(1))


def _adaln_kernel(c_ref, w_ref, b_ref, o_ref):
    c = c_ref[...]
    c_act = (c * jax.nn.sigmoid(c)).astype(BF16)
    o_ref[...] = jnp.dot(c_act, w_ref[...].astype(BF16), preferred_element_type=F32) + b_ref[...]


def _adaln(c_pad, w_ada, b_ada):
    rows, d = c_pad.shape
    n = w_ada.shape[1]
    tn = 1536
    return pl.pallas_call(
        _adaln_kernel,
        grid=(n // tn,),
        in_specs=[
            pl.BlockSpec((rows, d), lambda j: (0, 0)),
            pl.BlockSpec((d, tn), lambda j: (0, j)),
            pl.BlockSpec((1, tn), lambda j: (0, j)),
        ],
        out_specs=pl.BlockSpec((rows, tn), lambda j: (0, j)),
        out_shape=jax.ShapeDtypeStruct((rows, n), F32),
        compiler_params=pltpu.CompilerParams(
            dimension_semantics=("arbitrary",), vmem_limit_bytes=VMEM_LIMIT),
        name="adaln_mod",
    )(c_pad, w_ada, b_ada)


IN_PROJ_ROWS = 256


def _rms(v, axis):
    return v * lax.rsqrt(jnp.mean(v * v, axis=axis, keepdims=True) + EPS)


def _in_proj_kernel(x_ref, mod_ref, posr_ref, invc_ref, wt_ref,
                    qan_ref, kvan_ref, wuqt_ref, wuk_ref, wuvt_ref,
                    gqn_ref, gqr_ref, gkn_ref, gkr_ref,
                    xr_ref, sgr_ref, sga_ref, qt_ref, k_ref, vt_ref, bound_ref, *, q_scale):
    d_rnn, d_att = xr_ref.shape[2], sga_ref.shape[2]
    q_lora, kv_lora = qan_ref.shape[1], kvan_ref.shape[1]
    qc0, kvc0 = 2 * d_rnn, 2 * d_rnn + q_lora
    kr0 = kvc0 + kv_lora
    ga0 = kr0 + QK_ROPE
    scale = mod_ref[0, 1:2, :]
    shift = mod_ref[0, 0:1, :]
    hd = QK_NOPE + QK_ROPE
    half = QK_ROPE // 2
    gkn, gkr = gkn_ref[...], gkr_ref[...]
    k_bound = jnp.sqrt(QK_NOPE * jnp.max(gkn * gkn, axis=1, keepdims=True)
                       + QK_ROPE * jnp.max(gkr * gkr, axis=0, keepdims=True))
    pad_row = lax.broadcasted_iota(jnp.int32, (QK_PAD - hd, IN_PROJ_ROWS), 0)

    chunks = [slice(r0, r0 + IN_PROJ_ROWS) for r0 in range(0, x_ref.shape[1], IN_PROJ_ROWS)]
    hs = [(_rms(x_ref[0, rs, :], -1) * (1.0 + scale) + shift).astype(BF16) for rs in chunks]
    for rs, h in zip(chunks, hs):

        def proj(lo, hi):
            return lax.dot_general(h, wt_ref[lo:hi, :], NT_DIMS, preferred_element_type=F32)

        qcn = (_rms(proj(qc0, kvc0), -1) * qan_ref[...]).astype(BF16)
        kvcn = (_rms(proj(kvc0, kr0), -1) * kvan_ref[...]).astype(BF16)
        kr_t = proj(kr0, kr0 + 128).T[:QK_ROPE]
        q_t = lax.dot_general(wuqt_ref[...], qcn, NT_DIMS, preferred_element_type=F32)
        v_t = lax.dot_general(wuvt_ref[...], kvcn, NT_DIMS, preferred_element_type=F32)
        kn_all = jnp.dot(kvcn, wuk_ref[...], preferred_element_type=F32)

        ang_t = invc_ref[...] * posr_ref[0, :, rs]
        cos_t, sin_t = jnp.cos(ang_t), jnp.sin(ang_t)

        def rope_t(v):
            v1, v2 = v[:half], v[half:]
            return v1 * cos_t - v2 * sin_t, v1 * sin_t + v2 * cos_t

        bound_max = None
        for hh in range(MLA_HEADS):
            qn = q_t[hh * hd:hh * hd + QK_NOPE]
            qn = _rms(qn, 0) * (gqn_ref[...] * q_scale)
            qr = q_t[hh * hd + QK_NOPE:(hh + 1) * hd]
            r1, r2 = rope_t(_rms(qr, 0) * (gqr_ref[...] * q_scale))
            q_sq = (jnp.sum(qn * qn, axis=0, keepdims=True)
                    + jnp.sum(r1 * r1 + r2 * r2, axis=0, keepdims=True))
            bound = jnp.sqrt(q_sq) * k_bound
            bound_max = bound if bound_max is None else jnp.maximum(bound_max, bound)
            qt_ref[0, hh, 0:QK_NOPE, rs] = qn.astype(BF16)
            qt_ref[0, hh, QK_NOPE:QK_NOPE + half, rs] = r1.astype(BF16)
            qt_ref[0, hh, QK_NOPE + half:hd, rs] = r2.astype(BF16)
            qt_ref[0, hh, hd:QK_PAD, rs] = jnp.where(pad_row == 0, -bound, 0.0).astype(BF16)
        bound_ref[0, :, rs] = bound_max

        for hh in range(MLA_HEADS):
            vt_ref[0, hh, :, rs] = v_t[hh * V_HEAD:(hh + 1) * V_HEAD].astype(BF16)
        k1, k2 = rope_t(_rms(kr_t, 0) * gkr)
        kpe_t = jnp.concatenate([k1, k2, jnp.where(pad_row == 0, 1.0, 0.0)], axis=0)
        kpe = kpe_t.T.astype(BF16)
        for hh in range(MLA_HEADS):
            kn = kn_all[:, hh * QK_NOPE:(hh + 1) * QK_NOPE]
            k_ref[0, hh, rs, 0:QK_NOPE] = (_rms(kn, -1) * gkn).astype(BF16)
            k_ref[0, hh, rs, QK_NOPE:QK_PAD] = kpe

        xr_ref[0, rs, :] = proj(0, d_rnn).astype(BF16)
        gr = proj(d_rnn, 2 * d_rnn)
        sgr_ref[0, rs, :] = (gr * jax.nn.sigmoid(gr)).astype(BF16)
        ga = proj(ga0, ga0 + d_att)
        sga_ref[0, rs, :] = (ga * jax.nn.sigmoid(ga)).astype(BF16)


def _in_proj(x, mod, pos_row, inv_col, w, tm):
    bsz, seq, d = x.shape
    hd = QK_NOPE + QK_ROPE
    q_scale = (hd ** -0.5) * math.log2(math.e)
    consts = [w["wt"],
              w["qan"], w["kvan"], w["wuqt"], w["wuk"], w["wuvt"],
              w["gqn"], w["gqr"], w["gkn"], w["gkr"]]
    d_rnn, d_att = w["d_rnn"], MLA_HEADS * V_HEAD
    return pl.pallas_call(
        functools.partial(_in_proj_kernel, q_scale=q_scale),
        grid=(bsz, seq // tm),
        in_specs=[
            pl.BlockSpec((1, tm, d), lambda b, i: (b, i, 0)),
            pl.BlockSpec((1, 3, d), lambda b, i: (b, 0, 0)),
            pl.BlockSpec((1, 1, tm), lambda b, i: (b, 0, i)),
            _const_spec(inv_col.shape),
        ] + [_const_spec(a.shape) for a in consts],
        out_specs=[
            pl.BlockSpec((1, tm, d_rnn), lambda b, i: (b, i, 0)),
            pl.BlockSpec((1, tm, d_rnn), lambda b, i: (b, i, 0)),
            pl.BlockSpec((1, tm, d_att), lambda b, i: (b, i, 0)),
            pl.BlockSpec((1, MLA_HEADS, QK_PAD, tm), lambda b, i: (b, 0, 0, i)),
            pl.BlockSpec((1, MLA_HEADS, tm, QK_PAD), lambda b, i: (b, 0, i, 0)),
            pl.BlockSpec((1, MLA_HEADS, V_HEAD, tm), lambda b, i: (b, 0, 0, i)),
            pl.BlockSpec((1, 1, tm), lambda b, i: (b, 0, i)),
        ],
        out_shape=[
            jax.ShapeDtypeStruct((bsz, seq, d_rnn), BF16),
            jax.ShapeDtypeStruct((bsz, seq, d_rnn), BF16),
            jax.ShapeDtypeStruct((bsz, seq, d_att), BF16),
            jax.ShapeDtypeStruct((bsz, MLA_HEADS, QK_PAD, seq), BF16),
            jax.ShapeDtypeStruct((bsz, MLA_HEADS, seq, QK_PAD), BF16),
            jax.ShapeDtypeStruct((bsz, MLA_HEADS, V_HEAD, seq), BF16),
            jax.ShapeDtypeStruct((bsz, 1, seq), F32),
        ],
        compiler_params=pltpu.CompilerParams(
            dimension_semantics=("arbitrary", "arbitrary"), vmem_limit_bytes=VMEM_LIMIT),
        name="in_proj",
    )(x, mod, pos_row, inv_col, *consts)


ATTN_HEADS_PER_STEP = 4
ATTN_KEY_SUB = 256
MAX_SAFE_SCORE_BOUND = 56.0


def _attn_kernel(qt_ref, k_ref, vt_ref, sga_ref, o_ref, acc_ref, *, tq, running_max):
    qi = pl.program_id(2)
    nh = qt_ref.shape[1]
    acc_ref[...] = jnp.zeros_like(acc_ref)

    sub = tq if running_max else ATTN_KEY_SUB

    def first_query(unit, masked):
        return unit[2] * sub if masked[unit[0]] else 0

    def scores(unit, js, masked):
        bi, hh, c = unit
        key0 = pl.multiple_of(js[bi] * tq + c * sub, sub)
        k_sub = k_ref[0, hh, pl.ds(key0, sub), :]
        q_t = qt_ref[0, hh, :, first_query(unit, masked):]
        return jnp.dot(k_sub, q_t, preferred_element_type=F32)

    def accumulate(unit, js, s_t, m, l, masked):
        bi, hh, c = unit
        q0 = first_query(unit, masked)
        if masked[bi]:
            key = lax.broadcasted_iota(jnp.int32, s_t.shape, 0) + c * sub
            qry = lax.broadcasted_iota(jnp.int32, s_t.shape, 1) + q0
            s_t = jnp.where(key <= qry, s_t, -jnp.inf)
        key0 = pl.multiple_of(js[bi] * tq + c * sub, sub)
        v_sub = vt_ref[0, hh, :, pl.ds(key0, sub)]
        if running_max:
            m_new = jnp.maximum(m, jnp.max(s_t, axis=0, keepdims=True))
            alpha = jnp.exp2(m - m_new)
            p_t = jnp.exp2(s_t - m_new)
            l_new = alpha * l + jnp.sum(p_t, axis=0, keepdims=True)
            pv = jnp.dot(v_sub, p_t.astype(BF16), preferred_element_type=F32)
            acc_ref[hh] = alpha * acc_ref[hh] + pv
        else:
            m_new = m
            p_t = jnp.exp2(s_t)
            l_part = l[:, q0:] + jnp.sum(p_t, axis=0, keepdims=True)
            l_new = l_part if q0 == 0 else jnp.concatenate([l[:, :q0], l_part], axis=1)
            acc_ref[hh, :, q0:] += jnp.dot(v_sub, p_t.astype(BF16), preferred_element_type=F32)
        return m_new, l_new

    def blocks(js, masked, carry):
        units = [(bi, hh, c) for bi in range(len(js)) for hh in range(nh)
                 for c in range(tq // sub)]
        carry = list(carry)
        s_next = scores(units[0], js, masked)
        for idx, unit in enumerate(units):
            s_cur = s_next
            if idx + 1 < len(units):
                s_next = scores(units[idx + 1], js, masked)
            hh = unit[1]
            carry[hh] = accumulate(unit, js, s_cur, carry[hh][0], carry[hh][1], masked)
        return tuple(carry)

    carry = tuple((jnp.full((1, tq), -jnp.inf, F32), jnp.zeros((1, tq), F32)) for _ in range(nh))
    if running_max:
        carry = lax.fori_loop(0, qi, lambda j, c: blocks([j], [False], c), carry)
        carry = blocks([qi], [True], carry)
    else:
        carry = lax.fori_loop(
            0, qi >> 1, lambda p, c: blocks([2 * p, 2 * p + 1], [False, False], c), carry)
        carry = lax.cond(
            (qi & 1) == 1,
            functools.partial(blocks, [qi - 1, qi], [False, True]),
            functools.partial(blocks, [qi], [True]),
            carry)

    for hh in range(nh):
        o_t = acc_ref[hh] / carry[hh][1]
        gate = sga_ref[0, :, hh * V_HEAD:(hh + 1) * V_HEAD].astype(F32)
        o_ref[0, :, hh * V_HEAD:(hh + 1) * V_HEAD] = (o_t.T * gate).astype(BF16)


def _attention(q_t, k, v_t, sga, tq, running_max):
    bsz, nh, _, seq = q_t.shape
    hb = ATTN_HEADS_PER_STEP
    return pl.pallas_call(
        functools.partial(_attn_kernel, tq=tq, running_max=running_max),
        grid=(bsz, nh // hb, seq // tq),
        in_specs=[
            pl.BlockSpec((1, hb, QK_PAD, tq), lambda b, h, i: (b, h, 0, i)),
            pl.BlockSpec((1, hb, seq, QK_PAD), lambda b, h, i: (b, h, 0, 0)),
            pl.BlockSpec((1, hb, V_HEAD, seq), lambda b, h, i: (b, h, 0, 0)),
            pl.BlockSpec((1, tq, hb * V_HEAD), lambda b, h, i: (b, i, h)),
        ],
        out_specs=pl.BlockSpec((1, tq, hb * V_HEAD), lambda b, h, i: (b, i, h)),
        out_shape=jax.ShapeDtypeStruct((bsz, seq, nh * V_HEAD), BF16),
        scratch_shapes=[pltpu.VMEM((hb, V_HEAD, tq), F32)],
        compiler_params=pltpu.CompilerParams(
            dimension_semantics=("arbitrary", "arbitrary", "arbitrary"),
            vmem_limit_bytes=VMEM_LIMIT),
        name="mla_attention_online" if running_max else "mla_attention",
    )(q_t, k, v_t, sga)


SUBLANES = 8


def _rglru_kernel(xr_ref, sgr_ref, cw_ref, cb_ref, wa_ref, ba_ref, wx_ref, bx_ref, lam_ref,
                  o_ref, tail_ref, a_ref, b_ref, hc_ref, *, ts):
    @pl.when(pl.program_id(1) == 0)
    def _():
        tail_ref[...] = jnp.zeros_like(tail_ref)
        hc_ref[...] = jnp.zeros_like(hc_ref)

    blk = MXU_DIM
    n_delay = CONV_WIDTH - 1
    r_i = lax.broadcasted_iota(jnp.int32, (n_delay * blk, blk), 0)
    c_i = lax.broadcasted_iota(jnp.int32, (n_delay * blk, blk), 1)
    delay_mat = jnp.where((r_i % blk) - c_i == r_i // blk + 1, 1.0, 0.0).astype(BF16)
    row8 = lax.broadcasted_iota(jnp.int32, tail_ref.shape, 0)
    tail = tail_ref[...]
    conv = []
    for b0 in range(0, ts, blk):
        xb = xr_ref[0, b0:b0 + blk, :]
        x0 = xb.astype(F32)
        delayed = jnp.dot(delay_mat, xb, preferred_element_type=F32)
        acc = cb_ref[...] + cw_ref[n_delay:n_delay + 1, :] * x0
        for d in range(1, CONV_WIDTH):
            xd = delayed[(d - 1) * blk:d * blk]
            head = jnp.where(row8 < d, pltpu.roll(tail, d, 0), xd[0:SUBLANES])
            xd = jnp.concatenate([head, xd[SUBLANES:]], axis=0)
            acc = acc + cw_ref[n_delay - d:n_delay - d + 1, :] * xd
        tail = x0[blk - SUBLANES:blk]
        conv.append(acc)
    tail_ref[...] = tail
    xc = jnp.concatenate(conv, axis=0)

    xcb = xc.astype(BF16)
    n_grp = wa_ref.shape[0]

    def gate(w_ref, bias_ref):
        parts = [jnp.dot(xcb[:, g * MXU_DIM:(g + 1) * MXU_DIM], w_ref[g],
                         preferred_element_type=F32) for g in range(n_grp)]
        z = jnp.concatenate(parts, axis=1) + bias_ref[...]
        return 0.5 * jnp.tanh(0.5 * z) + 0.5

    r = gate(wa_ref, ba_ref)
    i = gate(wx_ref, bx_ref)
    nl = -lam_ref[...]
    log_a_unit = (-LRU_C) * (jnp.maximum(nl, 0.0) + jnp.log1p(jnp.exp(-jnp.abs(nl))))
    a_ref[...] = jnp.exp2(r * (log_a_unit * math.log2(math.e)))
    t = jnp.tanh(r * log_a_unit)
    u = -2.0 * t
    coef = jnp.where(u > 0.0, u * lax.rsqrt(u * (1.0 - t)), 0.0)
    b_ref[...] = coef * (i * xc)

    row = lax.broadcasted_iota(jnp.int32, (SUBLANES, a_ref.shape[1]), 0)

    def scan_group(g, h_prev):
        base = pl.multiple_of(g * SUBLANES, SUBLANES)
        a = a_ref[pl.ds(base, SUBLANES), :]
        b = b_ref[pl.ds(base, SUBLANES), :]
        for dist in (1, 2, 4):
            keep = row >= dist
            b = jnp.where(keep, a * pltpu.roll(b, dist, 0) + b, b)
            a = jnp.where(keep, a * pltpu.roll(a, dist, 0), a)
        hg = a * h_prev + b
        b_ref[pl.ds(base, SUBLANES), :] = hg
        return jnp.broadcast_to(hg[SUBLANES - 1:SUBLANES, :], hg.shape)

    hc_ref[...] = lax.fori_loop(0, ts // SUBLANES, scan_group, hc_ref[...], unroll=4)
    o_ref[0] = (b_ref[...] * sgr_ref[0].astype(F32)).astype(BF16)


def _rglru(xr, sgr, cw, cb, wa, ba, wx, bx, lam, ts):
    bsz, seq, c = xr.shape
    consts = [cw, cb, wa, ba, wx, bx, lam]
    return pl.pallas_call(
        functools.partial(_rglru_kernel, ts=ts),
        grid=(bsz, seq // ts),
        in_specs=[
            pl.BlockSpec((1, ts, c), lambda b, i: (b, i, 0)),
            pl.BlockSpec((1, ts, c), lambda b, i: (b, i, 0)),
        ] + [_const_spec(a.shape) for a in consts],
        out_specs=pl.BlockSpec((1, ts, c), lambda b, i: (b, i, 0)),
        out_shape=jax.ShapeDtypeStruct((bsz, seq, c), BF16),
        scratch_shapes=[
            pltpu.VMEM((SUBLANES, c), F32),
            pltpu.VMEM((ts, c), F32),
            pltpu.VMEM((ts, c), F32),
            pltpu.VMEM((SUBLANES, c), F32),
        ],
        compiler_params=pltpu.CompilerParams(
            dimension_semantics=("arbitrary", "arbitrary"), vmem_limit_bytes=VMEM_LIMIT),
        name="rglru",
    )(xr, sgr, *consts)


def _out_proj_kernel(x_ref, yr_ref, ya_ref, w32_ref, mod_ref, o_ref, w_ref):
    @pl.when((pl.program_id(0) == 0) & (pl.program_id(1) == 0))
    def _():
        w_ref[...] = w32_ref[...].astype(BF16)

    d_rnn = yr_ref.shape[2]
    y = jnp.dot(yr_ref[0], w_ref[0:d_rnn, :], preferred_element_type=F32)
    y = y + jnp.dot(ya_ref[0], w_ref[d_rnn:, :], preferred_element_type=F32)
    o_ref[0] = x_ref[0] + mod_ref[0, 2:3, :] * y


def _out_proj(x, y_rnn, y_att, w_o, mod, tm):
    bsz, seq, d = x.shape
    return pl.pallas_call(
        _out_proj_kernel,
        grid=(bsz, seq // tm),
        in_specs=[
            pl.BlockSpec((1, tm, d), lambda b, i: (b, i, 0)),
            pl.BlockSpec((1, tm, y_rnn.shape[2]), lambda b, i: (b, i, 0)),
            pl.BlockSpec((1, tm, y_att.shape[2]), lambda b, i: (b, i, 0)),
            _const_spec(w_o.shape),
            pl.BlockSpec((1, 3, d), lambda b, i: (b, 0, 0)),
        ],
        out_specs=pl.BlockSpec((1, tm, d), lambda b, i: (b, i, 0)),
        out_shape=jax.ShapeDtypeStruct((bsz, seq, d), F32),
        scratch_shapes=[pltpu.VMEM(w_o.shape, BF16)],
        compiler_params=pltpu.CompilerParams(
            dimension_semantics=("arbitrary", "arbitrary"), vmem_limit_bytes=VMEM_LIMIT),
        name="out_proj",
    )(x, y_rnn, y_att, w_o, mod)


def _block_diag_groups(w):
    nb, n, _ = w.shape
    per = MXU_DIM // n
    w = w.reshape(nb // per, per, n, n)
    eye = jnp.eye(per, dtype=w.dtype)
    return jnp.einsum("gpij,pq->gpiqj", w, eye).reshape(nb // per, MXU_DIM, MXU_DIM)


def kernel(x, c, positions, w_ada, b_ada, w_in, conv_w, conv_b, w_rg_a, b_rg_a, w_rg_x, b_rg_x,
           lru_lambda, q_a_norm, w_uq, kv_a_norm, w_ukv, q_norm_nope, q_norm_rope, k_norm_nope,
           k_norm_rope, w_out):
    bsz, seq, d = x.shape
    depth = w_in.shape[0]
    d_rnn = conv_w.shape[2]
    q_lora = q_a_norm.shape[1]
    kv_lora = kv_a_norm.shape[1]
    tile = 512

    inv_freq = 1.0 / (ROPE_THETA ** (jnp.arange(0, QK_ROPE, 2, dtype=F32) / QK_ROPE))
    pos = positions.astype(F32)
    pos_row = pos.reshape(bsz, 1, seq)
    inv_col = inv_freq.reshape(-1, 1)
    c_pad = jnp.zeros((SUBLANES, d), F32).at[:bsz].set(c)

    for l in range(depth):
        mod = _adaln(c_pad, w_ada[l], b_ada[l].reshape(1, -1))[:bsz].reshape(bsz, 3, d)

        ukv = w_ukv[l].reshape(kv_lora, MLA_HEADS, QK_NOPE + V_HEAD)
        w = dict(
            d_rnn=d_rnn, wt=jnp.swapaxes(w_in[l], 0, 1).astype(BF16),
            qan=q_a_norm[l].reshape(1, -1), kvan=kv_a_norm[l].reshape(1, -1),
            wuqt=w_uq[l].T.astype(BF16),
            wuk=ukv[:, :, :QK_NOPE].reshape(kv_lora, -1).astype(BF16),
            wuvt=ukv[:, :, QK_NOPE:].reshape(kv_lora, -1).T.astype(BF16),
            gqn=q_norm_nope[l].reshape(-1, 1), gqr=q_norm_rope[l].reshape(-1, 1),
            gkn=k_norm_nope[l].reshape(1, -1), gkr=k_norm_rope[l].reshape(-1, 1),
        )
        xr, sgr, sga, q_t, k, v_t, bound = _in_proj(x, mod, pos_row, inv_col, w, tile)

        y_att = lax.cond(
            jnp.max(bound) < MAX_SAFE_SCORE_BOUND,
            functools.partial(_attention, tq=tile, running_max=False),
            functools.partial(_attention, tq=tile, running_max=True),
            q_t, k, v_t, sga)
        y_rnn = _rglru(
            xr, sgr, conv_w[l], conv_b[l].reshape(1, -1),
            _block_diag_groups(w_rg_a[l]).astype(BF16), b_rg_a[l].reshape(1, -1),
            _block_diag_groups(w_rg_x[l]).astype(BF16), b_rg_x[l].reshape(1, -1),
            lru_lambda[l].reshape(1, -1), tile)

        x = _out_proj(x, y_rnn, y_att, w_out[l], mod, tile)
    return x
```

```python
import functools
import math

import jax
import jax.numpy as jnp
from jax import lax
from jax.experimental import pallas as pl
from jax.experimental.pallas import tpu as pltpu

RNN_BLOCKS = 16
CONV_WIDTH = 4
LRU_C = 8.0
MLA_HEADS = 8
QK_NOPE = 128
QK_ROPE = 64
V_HEAD = 128
ROPE_THETA = 10000.0
EPS = 1e-6

QK_PAD = 256
MXU_DIM = 256
VMEM_LIMIT = 56 * 1024 * 1024

F32 = jnp.float32
BF16 = jnp.bfloat16

NT_DIMS = (((1,), (1,)), ((), ()))


def _const_spec(shape):
    nd = len(shape)
    return pl.BlockSpec(shape, lambda *_: (0,) * nd, pipeline_mode=pl.Buffered(1))


def _adaln_kernel(c_ref, w_ref, b_ref, o_ref):
    c = c_ref[...]
    c_act = (c * jax.nn.sigmoid(c)).astype(BF16)
    o_ref[...] = jnp.dot(c_act, w_ref[...].astype(BF16), preferred_element_type=F32) + b_ref[...]


def _adaln(c_pad, w_ada, b_ada):
    rows, d = c_pad.shape
    n = w_ada.shape[1]
    tn = 1536
    return pl.pallas_call(
        _adaln_kernel,
        grid=(n // tn,),
        in_specs=[
            pl.BlockSpec((rows, d), lambda j: (0, 0)),
            pl.BlockSpec((d, tn), lambda j: (0, j)),
            pl.BlockSpec((1, tn), lambda j: (0, j)),
        ],
        out_specs=pl.BlockSpec((rows, tn), lambda j: (0, j)),
        out_shape=jax.ShapeDtypeStruct((rows, n), F32),
        compiler_params=pltpu.CompilerParams(
            dimension_semantics=("arbitrary",), vmem_limit_bytes=VMEM_LIMIT),
        name="adaln_mod",
    )(c_pad, w_ada, b_ada)


IN_PROJ_ROWS = 256


def _rms(v, axis):
    return v * lax.rsqrt(jnp.mean(v * v, axis=axis, keepdims=True) + EPS)


def _in_proj_kernel(x_ref, mod_ref, posr_ref, invc_ref, wt_ref,
                    qan_ref, kvan_ref, wuqt_ref, wuk_ref, wuvt_ref,
                    gqn_ref, gqr_ref, gkn_ref, gkr_ref,
                    xr_ref, sgr_ref, sga_ref, qt_ref, k_ref, vt_ref, bound_ref, *, q_scale):
    d_rnn, d_att = xr_ref.shape[2], sga_ref.shape[2]
    q_lora, kv_lora = qan_ref.shape[1], kvan_ref.shape[1]
    qc0, kvc0 = 2 * d_rnn, 2 * d_rnn + q_lora
    kr0 = kvc0 + kv_lora
    ga0 = kr0 + QK_ROPE
    scale = mod_ref[0, 1:2, :]
    shift = mod_ref[0, 0:1, :]
    hd = QK_NOPE + QK_ROPE
    half = QK_ROPE // 2
    gkn, gkr = gkn_ref[...], gkr_ref[...]
    k_bound = jnp.sqrt(QK_NOPE * jnp.max(gkn * gkn, axis=1, keepdims=True)
                       + QK_ROPE * jnp.max(gkr * gkr, axis=0, keepdims=True))
    pad_row = lax.broadcasted_iota(jnp.int32, (QK_PAD - hd, IN_PROJ_ROWS), 0)

    chunks = [slice(r0, r0 + IN_PROJ_ROWS) for r0 in range(0, x_ref.shape[1], IN_PROJ_ROWS)]
    hs = [(_rms(x_ref[0, rs, :], -1) * (1.0 + scale) + shift).astype(BF16) for rs in chunks]
    for rs, h in zip(chunks, hs):

        def proj(lo, hi):
            return lax.dot_general(h, wt_ref[lo:hi, :], NT_DIMS, preferred_element_type=F32)

        qcn = (_rms(proj(qc0, kvc0), -1) * qan_ref[...]).astype(BF16)
        kvcn = (_rms(proj(kvc0, kr0), -1) * kvan_ref[...]).astype(BF16)
        kr_t = proj(kr0, kr0 + 128).T[:QK_ROPE]
        q_t = lax.dot_general(wuqt_ref[...], qcn, NT_DIMS, preferred_element_type=F32)
        v_t = lax.dot_general(wuvt_ref[...], kvcn, NT_DIMS, preferred_element_type=F32)
        kn_all = jnp.dot(kvcn, wuk_ref[...], preferred_element_type=F32)

        ang_t = invc_ref[...] * posr_ref[0, :, rs]
        cos_t, sin_t = jnp.cos(ang_t), jnp.sin(ang_t)

        def rope_t(v):
            v1, v2 = v[:half], v[half:]
            return v1 * cos_t - v2 * sin_t, v1 * sin_t + v2 * cos_t

        bound_max = None
        for hh in range(MLA_HEADS):
            qn = q_t[hh * hd:hh * hd + QK_NOPE]
            qn = _rms(qn, 0) * (gqn_ref[...] * q_scale)
            qr = q_t[hh * hd + QK_NOPE:(hh + 1) * hd]
            r1, r2 = rope_t(_rms(qr, 0) * (gqr_ref[...] * q_scale))
            q_sq = (jnp.sum(qn * qn, axis=0, keepdims=True)
                    + jnp.sum(r1 * r1 + r2 * r2, axis=0, keepdims=True))
            bound = jnp.sqrt(q_sq) * k_bound
            bound_max = bound if bound_max is None else jnp.maximum(bound_max, bound)
            qt_ref[0, hh, 0:QK_NOPE, rs] = qn.astype(BF16)
            qt_ref[0, hh, QK_NOPE:QK_NOPE + half, rs] = r1.astype(BF16)
            qt_ref[0, hh, QK_NOPE + half:hd, rs] = r2.astype(BF16)
            qt_ref[0, hh, hd:QK_PAD, rs] = jnp.where(pad_row == 0, -bound, 0.0).astype(BF16)
        bound_ref[0, :, rs] = bound_max

        for hh in range(MLA_HEADS):
            vt_ref[0, hh, :, rs] = v_t[hh * V_HEAD:(hh + 1) * V_HEAD].astype(BF16)
        k1, k2 = rope_t(_rms(kr_t, 0) * gkr)
        kpe_t = jnp.concatenate([k1, k2, jnp.where(pad_row == 0, 1.0, 0.0)], axis=0)
        kpe = kpe_t.T.astype(BF16)
        for hh in range(MLA_HEADS):
            kn = kn_all[:, hh * QK_NOPE:(hh + 1) * QK_NOPE]
            k_ref[0, hh, rs, 0:QK_NOPE] = (_rms(kn, -1) * gkn).astype(BF16)
            k_ref[0, hh, rs, QK_NOPE:QK_PAD] = kpe

        xr_ref[0, rs, :] = proj(0, d_rnn).astype(BF16)
        gr = proj(d_rnn, 2 * d_rnn)
        sgr_ref[0, rs, :] = (gr * jax.nn.sigmoid(gr)).astype(BF16)
        ga = proj(ga0, ga0 + d_att)
        sga_ref[0, rs, :] = (ga * jax.nn.sigmoid(ga)).astype(BF16)


def _in_proj(x, mod, pos_row, inv_col, w, tm):
    bsz, seq, d = x.shape
    hd = QK_NOPE + QK_ROPE
    q_scale = (hd ** -0.5) * math.log2(math.e)
    consts = [w["wt"],
              w["qan"], w["kvan"], w["wuqt"], w["wuk"], w["wuvt"],
              w["gqn"], w["gqr"], w["gkn"], w["gkr"]]
    d_rnn, d_att = w["d_rnn"], MLA_HEADS * V_HEAD
    return pl.pallas_call(
        functools.partial(_in_proj_kernel, q_scale=q_scale),
        grid=(bsz, seq // tm),
        in_specs=[
            pl.BlockSpec((1, tm, d), lambda b, i: (b, i, 0)),
            pl.BlockSpec((1, 3, d), lambda b, i: (b, 0, 0)),
            pl.BlockSpec((1, 1, tm), lambda b, i: (b, 0, i)),
            _const_spec(inv_col.shape),
        ] + [_const_spec(a.shape) for a in consts],
        out_specs=[
            pl.BlockSpec((1, tm, d_rnn), lambda b, i: (b, i, 0)),
            pl.BlockSpec((1, tm, d_rnn), lambda b, i: (b, i, 0)),
            pl.BlockSpec((1, tm, d_att), lambda b, i: (b, i, 0)),
            pl.BlockSpec((1, MLA_HEADS, QK_PAD, tm), lambda b, i: (b, 0, 0, i)),
            pl.BlockSpec((1, MLA_HEADS, tm, QK_PAD), lambda b, i: (b, 0, i, 0)),
            pl.BlockSpec((1, MLA_HEADS, V_HEAD, tm), lambda b, i: (b, 0, 0, i)),
            pl.BlockSpec((1, 1, tm), lambda b, i: (b, 0, i)),
        ],
        out_shape=[
            jax.ShapeDtypeStruct((bsz, seq, d_rnn), BF16),
            jax.ShapeDtypeStruct((bsz, seq, d_rnn), BF16),
            jax.ShapeDtypeStruct((bsz, seq, d_att), BF16),
            jax.ShapeDtypeStruct((bsz, MLA_HEADS, QK_PAD, seq), BF16),
            jax.ShapeDtypeStruct((bsz, MLA_HEADS, seq, QK_PAD), BF16),
            jax.ShapeDtypeStruct((bsz, MLA_HEADS, V_HEAD, seq), BF16),
            jax.ShapeDtypeStruct((bsz, 1, seq), F32),
        ],
        compiler_params=pltpu.CompilerParams(
            dimension_semantics=("arbitrary", "arbitrary"), vmem_limit_bytes=VMEM_LIMIT),
        name="in_proj",
    )(x, mod, pos_row, inv_col, *consts)


ATTN_HEADS_PER_STEP = 4
ATTN_KEY_SUB = 256
MAX_SAFE_SCORE_BOUND = 56.0


def _attn_kernel(qt_ref, k_ref, vt_ref, sga_ref, o_ref, acc_ref, *, tq, running_max):
    qi = pl.program_id(2)
    nh = qt_ref.shape[1]
    acc_ref[...] = jnp.zeros_like(acc_ref)

    sub = tq if running_max else ATTN_KEY_SUB

    def first_query(unit, masked):
        return unit[2] * sub if masked[unit[0]] else 0

    def scores(unit, js, masked):
        bi, hh, c = unit
        key0 = pl.multiple_of(js[bi] * tq + c * sub, sub)
        k_sub = k_ref[0, hh, pl.ds(key0, sub), :]
        q_t = qt_ref[0, hh, :, first_query(unit, masked):]
        return jnp.dot(k_sub, q_t, preferred_element_type=F32)

    def accumulate(unit, js, s_t, m, l, masked):
        bi, hh, c = unit
        q0 = first_query(unit, masked)
        if masked[bi]:
            key = lax.broadcasted_iota(jnp.int32, s_t.shape, 0) + c * sub
            qry = lax.broadcasted_iota(jnp.int32, s_t.shape, 1) + q0
            s_t = jnp.where(key <= qry, s_t, -jnp.inf)
        key0 = pl.multiple_of(js[bi] * tq + c * sub, sub)
        v_sub = vt_ref[0, hh, :, pl.ds(key0, sub)]
        if running_max:
            m_new = jnp.maximum(m, jnp.max(s_t, axis=0, keepdims=True))
            alpha = jnp.exp2(m - m_new)
            p_t = jnp.exp2(s_t - m_new)
            l_new = alpha * l + jnp.sum(p_t, axis=0, keepdims=True)
            pv = jnp.dot(v_sub, p_t.astype(BF16), preferred_element_type=F32)
            acc_ref[hh] = alpha * acc_ref[hh] + pv
        else:
            m_new = m
            p_t = jnp.exp2(s_t)
            l_part = l[:, q0:] + jnp.sum(p_t, axis=0, keepdims=True)
            l_new = l_part if q0 == 0 else jnp.concatenate([l[:, :q0], l_part], axis=1)
            acc_ref[hh, :, q0:] += jnp.dot(v_sub, p_t.astype(BF16), preferred_element_type=F32)
        return m_new, l_new

    def blocks(js, masked, carry):
        units = [(bi, hh, c) for bi in range(len(js)) for hh in range(nh)
                 for c in range(tq // sub)]
        carry = list(carry)
        s_next = scores(units[0], js, masked)
        for idx, unit in enumerate(units):
            s_cur = s_next
            if idx + 1 < len(units):
                s_next = scores(units[idx + 1], js, masked)
            hh = unit[1]
            carry[hh] = accumulate(unit, js, s_cur, carry[hh][0], carry[hh][1], masked)
        return tuple(carry)

    carry = tuple((jnp.full((1, tq), -jnp.inf, F32), jnp.zeros((1, tq), F32)) for _ in range(nh))
    if running_max:
        carry = lax.fori_loop(0, qi, lambda j, c: blocks([j], [False], c), carry)
        carry = blocks([qi], [True], carry)
    else:
        carry = lax.fori_loop(
            0, qi >> 1, lambda p, c: blocks([2 * p, 2 * p + 1], [False, False], c), carry)
        carry = lax.cond(
            (qi & 1) == 1,
            functools.partial(blocks, [qi - 1, qi], [False, True]),
            functools.partial(blocks, [qi], [True]),
            carry)

    for hh in range(nh):
        o_t = acc_ref[hh] / carry[hh][1]
        gate = sga_ref[0, :, hh * V_HEAD:(hh + 1) * V_HEAD].astype(F32)
        o_ref[0, :, hh * V_HEAD:(hh + 1) * V_HEAD] = (o_t.T * gate).astype(BF16)


def _attention(q_t, k, v_t, sga, tq, running_max):
    bsz, nh, _, seq = q_t.shape
    hb = ATTN_HEADS_PER_STEP
    return pl.pallas_call(
        functools.partial(_attn_kernel, tq=tq, running_max=running_max),
        grid=(bsz, nh // hb, seq // tq),
        in_specs=[
            pl.BlockSpec((1, hb, QK_PAD, tq), lambda b, h, i: (b, h, 0, i)),
            pl.BlockSpec((1, hb, seq, QK_PAD), lambda b, h, i: (b, h, 0, 0)),
            pl.BlockSpec((1, hb, V_HEAD, seq), lambda b, h, i: (b, h, 0, 0)),
            pl.BlockSpec((1, tq, hb * V_HEAD), lambda b, h, i: (b, i, h)),
        ],
        out_specs=pl.BlockSpec((1, tq, hb * V_HEAD), lambda b, h, i: (b, i, h)),
        out_shape=jax.ShapeDtypeStruct((bsz, seq, nh * V_HEAD), BF16),
        scratch_shapes=[pltpu.VMEM((hb, V_HEAD, tq), F32)],
        compiler_params=pltpu.CompilerParams(
            dimension_semantics=("arbitrary", "arbitrary", "arbitrary"),
            vmem_limit_bytes=VMEM_LIMIT),
        name="mla_attention_online" if running_max else "mla_attention",
    )(q_t, k, v_t, sga)


SUBLANES = 8
LANES = 128


def _rglru_kernel(xr_ref, sgr_ref, cw_ref, cb_ref, wa_ref, ba_ref, wx_ref, bx_ref, lam_ref,
                  o_ref, tail_ref, a_ref, b_ref, hc_ref, *, ts):
    @pl.when(pl.program_id(1) == 0)
    def _():
        tail_ref[...] = jnp.zeros_like(tail_ref)
        hc_ref[...] = jnp.zeros_like(hc_ref)

    blk = MXU_DIM
    n_delay = CONV_WIDTH - 1
    r_i = lax.broadcasted_iota(jnp.int32, (n_delay * blk, blk), 0)
    c_i = lax.broadcasted_iota(jnp.int32, (n_delay * blk, blk), 1)
    delay_mat = jnp.where((r_i % blk) - c_i == r_i // blk + 1, 1.0, 0.0).astype(BF16)
    row8 = lax.broadcasted_iota(jnp.int32, tail_ref.shape, 0)
    tail = tail_ref[...]
    conv = []
    for b0 in range(0, ts, blk):
        xb = xr_ref[0, b0:b0 + blk, :]
        x0 = xb.astype(F32)
        delayed = jnp.dot(delay_mat, xb, preferred_element_type=F32)
        acc = cb_ref[...] + cw_ref[n_delay:n_delay + 1, :] * x0
        for d in range(1, CONV_WIDTH):
            xd = delayed[(d - 1) * blk:d * blk]
            head = jnp.where(row8 < d, pltpu.roll(tail, d, 0), xd[0:SUBLANES])
            xd = jnp.concatenate([head, xd[SUBLANES:]], axis=0)
            acc = acc + cw_ref[n_delay - d:n_delay - d + 1, :] * xd
        tail = x0[blk - SUBLANES:blk]
        conv.append(acc)
    tail_ref[...] = tail
    xc = jnp.concatenate(conv, axis=0)

    xcb = xc.astype(BF16)
    n_grp = wa_ref.shape[0]

    def gate(w_ref, bias_ref):
        parts = [jnp.dot(xcb[:, g * MXU_DIM:(g + 1) * MXU_DIM], w_ref[g],
                         preferred_element_type=F32) for g in range(n_grp)]
        z = jnp.concatenate(parts, axis=1) + bias_ref[...]
        return 0.5 * jnp.tanh(0.5 * z) + 0.5

    r = gate(wa_ref, ba_ref)
    i = gate(wx_ref, bx_ref)
    nl = -lam_ref[...]
    log_a_unit = (-LRU_C) * (jnp.maximum(nl, 0.0) + jnp.log1p(jnp.exp(-jnp.abs(nl))))
    a_all = jnp.exp2(r * (log_a_unit * math.log2(math.e)))
    t = jnp.tanh(r * log_a_unit)
    u = -2.0 * t
    coef = jnp.where(u > 0.0, u * lax.rsqrt(u * (1.0 - t)), 0.0)
    b_all = coef * (i * xc)
    n_slab = a_ref.shape[0]
    for s in range(n_slab):
        a_ref[s] = a_all[:, s * LANES:(s + 1) * LANES]
        b_ref[s] = b_all[:, s * LANES:(s + 1) * LANES]

    row = lax.broadcasted_iota(jnp.int32, (SUBLANES, LANES), 0)
    block_rows = SUBLANES * SUBLANES

    def scan_block(blk, carry):
        base = pl.multiple_of(blk * block_rows, block_rows)
        new_carry = []
        for s in range(n_slab):
            h_loc, a_run = [], []
            for j in range(SUBLANES):
                rows_j = pl.ds(base + j, SUBLANES, stride=SUBLANES)
                a = a_ref[s, rows_j, :]
                b = b_ref[s, rows_j, :]
                h_loc.append(b if j == 0 else a * h_loc[-1] + b)
                a_run.append(a if j == 0 else a * a_run[-1])
            pa, pb = a_run[-1], h_loc[-1]
            for dist in (1, 2, 4):
                keep = row >= dist
                pb = jnp.where(keep, pa * pltpu.roll(pb, dist, 0) + pb, pb)
                pa = jnp.where(keep, pa * pltpu.roll(pa, dist, 0), pa)
            after = pa * carry[s] + pb
            h_in = jnp.where(row == 0, carry[s], pltpu.roll(after, 1, 0))
            for j in range(SUBLANES):
                rows_j = pl.ds(base + j, SUBLANES, stride=SUBLANES)
                b_ref[s, rows_j, :] = a_run[j] * h_in + h_loc[j]
            new_carry.append(jnp.broadcast_to(after[SUBLANES - 1:SUBLANES, :], after.shape))
        return tuple(new_carry)

    carry = lax.fori_loop(0, ts // block_rows, scan_block,
                          tuple(hc_ref[s] for s in range(n_slab)), unroll=2)
    for s in range(n_slab):
        hc_ref[s] = carry[s]
    h_all = jnp.concatenate([b_ref[s] for s in range(n_slab)], axis=1)
    o_ref[0] = (h_all * sgr_ref[0].astype(F32)).astype(BF16)


def _rglru(xr, sgr, cw, cb, wa, ba, wx, bx, lam, ts):
    bsz, seq, c = xr.shape
    consts = [cw, cb, wa, ba, wx, bx, lam]
    return pl.pallas_call(
        functools.partial(_rglru_kernel, ts=ts),
        grid=(bsz, seq // ts),
        in_specs=[
            pl.BlockSpec((1, ts, c), lambda b, i: (b, i, 0)),
            pl.BlockSpec((1, ts, c), lambda b, i: (b, i, 0)),
        ] + [_const_spec(a.shape) for a in consts],
        out_specs=pl.BlockSpec((1, ts, c), lambda b, i: (b, i, 0)),
        out_shape=jax.ShapeDtypeStruct((bsz, seq, c), BF16),
        scratch_shapes=[
            pltpu.VMEM((SUBLANES, c), F32),
            pltpu.VMEM((c // LANES, ts, LANES), F32),
            pltpu.VMEM((c // LANES, ts, LANES), F32),
            pltpu.VMEM((c // LANES, SUBLANES, LANES), F32),
        ],
        compiler_params=pltpu.CompilerParams(
            dimension_semantics=("arbitrary", "arbitrary"), vmem_limit_bytes=VMEM_LIMIT),
        name="rglru",
    )(xr, sgr, *consts)


def _out_proj_kernel(x_ref, yr_ref, ya_ref, w32_ref, mod_ref, o_ref, w_ref):
    @pl.when((pl.program_id(0) == 0) & (pl.program_id(1) == 0))
    def _():
        w_ref[...] = w32_ref[...].astype(BF16)

    d_rnn = yr_ref.shape[2]
    y = jnp.dot(yr_ref[0], w_ref[0:d_rnn, :], preferred_element_type=F32)
    y = y + jnp.dot(ya_ref[0], w_ref[d_rnn:, :], preferred_element_type=F32)
    o_ref[0] = x_ref[0] + mod_ref[0, 2:3, :] * y


def _out_proj(x, y_rnn, y_att, w_o, mod, tm):
    bsz, seq, d = x.shape
    return pl.pallas_call(
        _out_proj_kernel,
        grid=(bsz, seq // tm),
        in_specs=[
            pl.BlockSpec((1, tm, d), lambda b, i: (b, i, 0)),
            pl.BlockSpec((1, tm, y_rnn.shape[2]), lambda b, i: (b, i, 0)),
            pl.BlockSpec((1, tm, y_att.shape[2]), lambda b, i: (b, i, 0)),
            _const_spec(w_o.shape),
            pl.BlockSpec((1, 3, d), lambda b, i: (b, 0, 0)),
        ],
        out_specs=pl.BlockSpec((1, tm, d), lambda b, i: (b, i, 0)),
        out_shape=jax.ShapeDtypeStruct((bsz, seq, d), F32),
        scratch_shapes=[pltpu.VMEM(w_o.shape, BF16)],
        compiler_params=pltpu.CompilerParams(
            dimension_semantics=("arbitrary", "arbitrary"), vmem_limit_bytes=VMEM_LIMIT),
        name="out_proj",
    )(x, y_rnn, y_att, w_o, mod)


def _block_diag_groups(w):
    nb, n, _ = w.shape
    per = MXU_DIM // n
    w = w.reshape(nb // per, per, n, n)
    eye = jnp.eye(per, dtype=w.dtype)
    return jnp.einsum("gpij,pq->gpiqj", w, eye).reshape(nb // per, MXU_DIM, MXU_DIM)


def kernel(x, c, positions, w_ada, b_ada, w_in, conv_w, conv_b, w_rg_a, b_rg_a, w_rg_x, b_rg_x,
           lru_lambda, q_a_norm, w_uq, kv_a_norm, w_ukv, q_norm_nope, q_norm_rope, k_norm_nope,
           k_norm_rope, w_out):
    bsz, seq, d = x.shape
    depth = w_in.shape[0]
    d_rnn = conv_w.shape[2]
    q_lora = q_a_norm.shape[1]
    kv_lora = kv_a_norm.shape[1]
    tile = 512

    inv_freq = 1.0 / (ROPE_THETA ** (jnp.arange(0, QK_ROPE, 2, dtype=F32) / QK_ROPE))
    pos = positions.astype(F32)
    pos_row = pos.reshape(bsz, 1, seq)
    inv_col = inv_freq.reshape(-1, 1)
    c_pad = jnp.zeros((SUBLANES, d), F32).at[:bsz].set(c)

    for l in range(depth):
        mod = _adaln(c_pad, w_ada[l], b_ada[l].reshape(1, -1))[:bsz].reshape(bsz, 3, d)

        ukv = w_ukv[l].reshape(kv_lora, MLA_HEADS, QK_NOPE + V_HEAD)
        w = dict(
            d_rnn=d_rnn, wt=jnp.swapaxes(w_in[l], 0, 1).astype(BF16),
            qan=q_a_norm[l].reshape(1, -1), kvan=kv_a_norm[l].reshape(1, -1),
            wuqt=w_uq[l].T.astype(BF16),
            wuk=ukv[:, :, :QK_NOPE].reshape(kv_lora, -1).astype(BF16),
            wuvt=ukv[:, :, QK_NOPE:].reshape(kv_lora, -1).T.astype(BF16),
            gqn=q_norm_nope[l].reshape(-1, 1), gqr=q_norm_rope[l].reshape(-1, 1),
            gkn=k_norm_nope[l].reshape(1, -1), gkr=k_norm_rope[l].reshape(-1, 1),
        )
        xr, sgr, sga, q_t, k, v_t, bound = _in_proj(x, mod, pos_row, inv_col, w, tile)

        y_att = lax.cond(
            jnp.max(bound) < MAX_SAFE_SCORE_BOUND,
            functools.partial(_attention, tq=tile, running_max=False),
            functools.partial(_attention, tq=tile, running_max=True),
            q_t, k, v_t, sga)
        y_rnn = _rglru(
            xr, sgr, conv_w[l], conv_b[l].reshape(1, -1),
            _block_diag_groups(w_rg_a[l]).astype(BF16), b_rg_a[l].reshape(1, -1),
            _block_diag_groups(w_rg_x[l]).astype(BF16), b_rg_x[l].reshape(1, -1),
            lru_lambda[l].reshape(1, -1), tile)

        x = _out_proj(x, y_rnn, y_att, w_out[l], mod, tile)
    return x
```

```python
import functools
import math

import jax
import jax.numpy as jnp
from jax import lax
from jax.experimental import pallas as pl
from jax.experimental.pallas import tpu as pltpu

RNN_BLOCKS = 16
CONV_WIDTH = 4
LRU_C = 8.0
MLA_HEADS = 8
QK_NOPE = 128
QK_ROPE = 64
V_HEAD = 128
ROPE_THETA = 10000.0
EPS = 1e-6

QK_PAD = 256
MXU_DIM = 256
VMEM_LIMIT = 56 * 1024 * 1024

F32 = jnp.float32
BF16 = jnp.bfloat16

NT_DIMS = (((1,), (1,)), ((), ()))


def _const_spec(shape):
    nd = len(shape)
    return pl.BlockSpec(shape, lambda *_: (0,) * nd, pipeline_mode=pl.Buffered(1))


def _adaln_kernel(c_ref, w_ref, b_ref, o_ref):
    c = c_ref[...]
    c_act = (c * jax.nn.sigmoid(c)).astype(BF16)
    o_ref[...] = jnp.dot(c_act, w_ref[...].astype(BF16), preferred_element_type=F32) + b_ref[...]


def _adaln(c_pad, w_ada, b_ada):
    rows, d = c_pad.shape
    n = w_ada.shape[1]
    tn = 1536
    return pl.pallas_call(
        _adaln_kernel,
        grid=(n // tn,),
        in_specs=[
            pl.BlockSpec((rows, d), lambda j: (0, 0)),
            pl.BlockSpec((d, tn), lambda j: (0, j)),
            pl.BlockSpec((1, tn), lambda j: (0, j)),
        ],
        out_specs=pl.BlockSpec((rows, tn), lambda j: (0, j)),
        out_shape=jax.ShapeDtypeStruct((rows, n), F32),
        compiler_params=pltpu.CompilerParams(
            dimension_semantics=("arbitrary",), vmem_limit_bytes=VMEM_LIMIT),
        name="adaln_mod",
    )(c_pad, w_ada, b_ada)


IN_PROJ_ROWS = 256


def _rms(v, axis):
    return v * lax.rsqrt(jnp.mean(v * v, axis=axis, keepdims=True) + EPS)


def _in_proj_kernel(x_ref, mod_ref, posr_ref, invc_ref, wt_ref,
                    qan_ref, kvan_ref, wuqt_ref, wuk_ref, wuvt_ref,
                    gqn_ref, gqr_ref, gkn_ref, gkr_ref,
                    xr_ref, sgr_ref, sga_ref, qt_ref, k_ref, vt_ref, bound_ref, *, q_scale):
    d_rnn, d_att = xr_ref.shape[2], sga_ref.shape[2]
    q_lora, kv_lora = qan_ref.shape[1], kvan_ref.shape[1]
    qc0, kvc0 = 2 * d_rnn, 2 * d_rnn + q_lora
    kr0 = kvc0 + kv_lora
    ga0 = kr0 + QK_ROPE
    scale = mod_ref[0, 1:2, :]
    shift = mod_ref[0, 0:1, :]
    hd = QK_NOPE + QK_ROPE
    half = QK_ROPE // 2
    gkn, gkr = gkn_ref[...], gkr_ref[...]
    k_bound = jnp.sqrt(QK_NOPE * jnp.max(gkn * gkn, axis=1, keepdims=True)
                       + QK_ROPE * jnp.max(gkr * gkr, axis=0, keepdims=True))
    pad_row = lax.broadcasted_iota(jnp.int32, (QK_PAD - hd, IN_PROJ_ROWS), 0)

    chunks = [slice(r0, r0 + IN_PROJ_ROWS) for r0 in range(0, x_ref.shape[1], IN_PROJ_ROWS)]
    hs = [(_rms(x_ref[0, rs, :], -1) * (1.0 + scale) + shift).astype(BF16) for rs in chunks]
    for rs, h in zip(chunks, hs):

        def proj(lo, hi):
            return lax.dot_general(h, wt_ref[lo:hi, :], NT_DIMS, preferred_element_type=F32)

        qcn = (_rms(proj(qc0, kvc0), -1) * qan_ref[...]).astype(BF16)
        kvcn = (_rms(proj(kvc0, kr0), -1) * kvan_ref[...]).astype(BF16)
        kr_t = proj(kr0, kr0 + 128).T[:QK_ROPE]
        q_t = lax.dot_general(wuqt_ref[...], qcn, NT_DIMS, preferred_element_type=F32)
        v_t = lax.dot_general(wuvt_ref[...], kvcn, NT_DIMS, preferred_element_type=F32)
        kn_all = jnp.dot(kvcn, wuk_ref[...], preferred_element_type=F32)

        ang_t = invc_ref[...] * posr_ref[0, :, rs]
        cos_t, sin_t = jnp.cos(ang_t), jnp.sin(ang_t)

        def rope_t(v):
            v1, v2 = v[:half], v[half:]
            return v1 * cos_t - v2 * sin_t, v1 * sin_t + v2 * cos_t

        bound_max = None
        for hh in range(MLA_HEADS):
            qn = q_t[hh * hd:hh * hd + QK_NOPE]
            qn = _rms(qn, 0) * (gqn_ref[...] * q_scale)
            qr = q_t[hh * hd + QK_NOPE:(hh + 1) * hd]
            r1, r2 = rope_t(_rms(qr, 0) * (gqr_ref[...] * q_scale))
            q_sq = (jnp.sum(qn * qn, axis=0, keepdims=True)
                    + jnp.sum(r1 * r1 + r2 * r2, axis=0, keepdims=True))
            bound = jnp.sqrt(q_sq) * k_bound
            bound_max = bound if bound_max is None else jnp.maximum(bound_max, bound)
            qt_ref[0, hh, 0:QK_NOPE, rs] = qn.astype(BF16)
            qt_ref[0, hh, QK_NOPE:QK_NOPE + half, rs] = r1.astype(BF16)
            qt_ref[0, hh, QK_NOPE + half:hd, rs] = r2.astype(BF16)
            qt_ref[0, hh, hd:QK_PAD, rs] = jnp.where(pad_row == 0, -bound, 0.0).astype(BF16)
        bound_ref[0, :, rs] = bound_max

        for hh in range(MLA_HEADS):
            vt_ref[0, hh, :, rs] = v_t[hh * V_HEAD:(hh + 1) * V_HEAD].astype(BF16)
        k1, k2 = rope_t(_rms(kr_t, 0) * gkr)
        kpe_t = jnp.concatenate([k1, k2, jnp.where(pad_row == 0, 1.0, 0.0)], axis=0)
        kpe = kpe_t.T.astype(BF16)
        for hh in range(MLA_HEADS):
            kn = kn_all[:, hh * QK_NOPE:(hh + 1) * QK_NOPE]
            k_ref[0, hh, rs, 0:QK_NOPE] = (_rms(kn, -1) * gkn).astype(BF16)
            k_ref[0, hh, rs, QK_NOPE:QK_PAD] = kpe

        xr_ref[0, rs, :] = proj(0, d_rnn).astype(BF16)
        gr = proj(d_rnn, 2 * d_rnn)
        sgr_ref[0, rs, :] = (gr * jax.nn.sigmoid(gr)).astype(BF16)
        ga = proj(ga0, ga0 + d_att)
        sga_ref[0, rs, :] = (ga * jax.nn.sigmoid(ga)).astype(BF16)


def _in_proj(x, mod, pos_row, inv_col, w, tm):
    bsz, seq, d = x.shape
    hd = QK_NOPE + QK_ROPE
    q_scale = (hd ** -0.5) * math.log2(math.e)
    consts = [w["wt"],
              w["qan"], w["kvan"], w["wuqt"], w["wuk"], w["wuvt"],
              w["gqn"], w["gqr"], w["gkn"], w["gkr"]]
    d_rnn, d_att = w["d_rnn"], MLA_HEADS * V_HEAD
    return pl.pallas_call(
        functools.partial(_in_proj_kernel, q_scale=q_scale),
        grid=(bsz, seq // tm),
        in_specs=[
            pl.BlockSpec((1, tm, d), lambda b, i: (b, i, 0)),
            pl.BlockSpec((1, 3, d), lambda b, i: (b, 0, 0)),
            pl.BlockSpec((1, 1, tm), lambda b, i: (b, 0, i)),
            _const_spec(inv_col.shape),
        ] + [_const_spec(a.shape) for a in consts],
        out_specs=[
            pl.BlockSpec((1, tm, d_rnn), lambda b, i: (b, i, 0)),
            pl.BlockSpec((1, tm, d_rnn), lambda b, i: (b, i, 0)),
            pl.BlockSpec((1, tm, d_att), lambda b, i: (b, i, 0)),
            pl.BlockSpec((1, MLA_HEADS, QK_PAD, tm), lambda b, i: (b, 0, 0, i)),
            pl.BlockSpec((1, MLA_HEADS, tm, QK_PAD), lambda b, i: (b, 0, i, 0)),
            pl.BlockSpec((1, MLA_HEADS, V_HEAD, tm), lambda b, i: (b, 0, 0, i)),
            pl.BlockSpec((1, 1, tm), lambda b, i: (b, 0, i)),
        ],
        out_shape=[
            jax.ShapeDtypeStruct((bsz, seq, d_rnn), BF16),
            jax.ShapeDtypeStruct((bsz, seq, d_rnn), BF16),
            jax.ShapeDtypeStruct((bsz, seq, d_att), BF16),
            jax.ShapeDtypeStruct((bsz, MLA_HEADS, QK_PAD, seq), BF16),
            jax.ShapeDtypeStruct((bsz, MLA_HEADS, seq, QK_PAD), BF16),
            jax.ShapeDtypeStruct((bsz, MLA_HEADS, V_HEAD, seq), BF16),
            jax.ShapeDtypeStruct((bsz, 1, seq), F32),
        ],
        compiler_params=pltpu.CompilerParams(
            dimension_semantics=("arbitrary", "arbitrary"), vmem_limit_bytes=VMEM_LIMIT),
        name="in_proj",
    )(x, mod, pos_row, inv_col, *consts)


ATTN_HEADS_PER_STEP = 4
ATTN_KEY_SUB = 256
ATTN_BLOCKS_PER_TRIP = 4
MAX_SAFE_SCORE_BOUND = 56.0


def _attn_kernel(qt_ref, k_ref, vt_ref, sga_ref, o_ref, acc_ref, *, tq, running_max):
    qi = pl.program_id(2)
    nh = qt_ref.shape[1]

    sub = tq if running_max else ATTN_KEY_SUB

    def first_query(unit, masked):
        return unit[2] * sub if masked[unit[0]] else 0

    def scores(unit, js, masked):
        bi, hh, c = unit
        key0 = pl.multiple_of(js[bi] * tq + c * sub, sub)
        k_sub = k_ref[0, hh, pl.ds(key0, sub), :]
        q_t = qt_ref[0, hh, :, first_query(unit, masked):]
        return jnp.dot(k_sub, q_t, preferred_element_type=F32)

    def accumulate(unit, js, s_t, m, l, masked, init):
        bi, hh, c = unit
        q0 = first_query(unit, masked)
        assert not (init and q0), "the unit that initialises acc must cover every query"
        if masked[bi]:
            key = lax.broadcasted_iota(jnp.int32, s_t.shape, 0) + c * sub
            qry = lax.broadcasted_iota(jnp.int32, s_t.shape, 1) + q0
            s_t = jnp.where(key <= qry, s_t, -jnp.inf)
        key0 = pl.multiple_of(js[bi] * tq + c * sub, sub)
        v_sub = vt_ref[0, hh, :, pl.ds(key0, sub)]
        if running_max:
            m_new = jnp.maximum(m, jnp.max(s_t, axis=0, keepdims=True))
            alpha = jnp.exp2(m - m_new)
            p_t = jnp.exp2(s_t - m_new)
            l_new = alpha * l + jnp.sum(p_t, axis=0, keepdims=True)
            pv = jnp.dot(v_sub, p_t.astype(BF16), preferred_element_type=F32)
            acc_ref[hh] = pv if init else alpha * acc_ref[hh] + pv
        else:
            m_new = m
            p_t = jnp.exp2(s_t)
            l_part = l[:, q0:] + jnp.sum(p_t, axis=0, keepdims=True)
            l_new = l_part if q0 == 0 else jnp.concatenate([l[:, :q0], l_part], axis=1)
            pv = jnp.dot(v_sub, p_t.astype(BF16), preferred_element_type=F32)
            if init:
                acc_ref[hh] = pv
            else:
                acc_ref[hh, :, q0:] += pv
        return m_new, l_new

    def blocks(js, masked, carry, first=False):
        units = [(bi, hh, c) for bi in range(len(js)) for hh in range(nh)
                 for c in range(tq // sub)]
        carry = list(carry)
        s_next = scores(units[0], js, masked)
        for idx, unit in enumerate(units):
            s_cur = s_next
            if idx + 1 < len(units):
                s_next = scores(units[idx + 1], js, masked)
            hh = unit[1]
            init = first and unit[0] == 0 and unit[2] == 0
            carry[hh] = accumulate(unit, js, s_cur, carry[hh][0], carry[hh][1], masked, init)
        return tuple(carry)

    carry = tuple((jnp.full((1, tq), -jnp.inf, F32), jnp.zeros((1, tq), F32)) for _ in range(nh))
    if running_max:
        carry = blocks([qi], [True], carry, first=True)
        carry = lax.fori_loop(0, qi, lambda j, c: blocks([j], [False], c), carry)
    else:
        g = ATTN_BLOCKS_PER_TRIP
        tails = [functools.partial(blocks, [qi - r + t for t in range(r + 1)], [False] * r + [True],
                                   first=True) for r in range(g)]
        carry = lax.switch(qi % g, tails, carry)
        carry = lax.fori_loop(
            0, qi // g,
            lambda p, c: blocks([g * p + t for t in range(g)], [False] * g, c), carry)

    for hh in range(nh):
        o_t = acc_ref[hh] / carry[hh][1]
        gate = sga_ref[0, :, hh * V_HEAD:(hh + 1) * V_HEAD].astype(F32)
        o_ref[0, :, hh * V_HEAD:(hh + 1) * V_HEAD] = (o_t.T * gate).astype(BF16)


def _attention(q_t, k, v_t, sga, tq, running_max):
    bsz, nh, _, seq = q_t.shape
    hb = ATTN_HEADS_PER_STEP
    return pl.pallas_call(
        functools.partial(_attn_kernel, tq=tq, running_max=running_max),
        grid=(bsz, nh // hb, seq // tq),
        in_specs=[
            pl.BlockSpec((1, hb, QK_PAD, tq), lambda b, h, i: (b, h, 0, i)),
            pl.BlockSpec((1, hb, seq, QK_PAD), lambda b, h, i: (b, h, 0, 0)),
            pl.BlockSpec((1, hb, V_HEAD, seq), lambda b, h, i: (b, h, 0, 0)),
            pl.BlockSpec((1, tq, hb * V_HEAD), lambda b, h, i: (b, i, h)),
        ],
        out_specs=pl.BlockSpec((1, tq, hb * V_HEAD), lambda b, h, i: (b, i, h)),
        out_shape=jax.ShapeDtypeStruct((bsz, seq, nh * V_HEAD), BF16),
        scratch_shapes=[pltpu.VMEM((hb, V_HEAD, tq), F32)],
        compiler_params=pltpu.CompilerParams(
            dimension_semantics=("arbitrary", "arbitrary", "arbitrary"),
            vmem_limit_bytes=VMEM_LIMIT),
        name="mla_attention_online" if running_max else "mla_attention",
    )(q_t, k, v_t, sga)


SUBLANES = 8
LANES = 128


def _rglru_kernel(xr_ref, sgr_ref, cw_ref, cb_ref, wa_ref, ba_ref, wx_ref, bx_ref, lam_ref,
                  o_ref, tail_ref, a_ref, b_ref, hc_ref, *, ts):
    @pl.when(pl.program_id(1) == 0)
    def _():
        tail_ref[...] = jnp.zeros_like(tail_ref)
        hc_ref[...] = jnp.zeros_like(hc_ref)

    blk = MXU_DIM
    n_delay = CONV_WIDTH - 1
    r_i = lax.broadcasted_iota(jnp.int32, (n_delay * blk, blk), 0)
    c_i = lax.broadcasted_iota(jnp.int32, (n_delay * blk, blk), 1)
    delay_mat = jnp.where((r_i % blk) - c_i == r_i // blk + 1, 1.0, 0.0).astype(BF16)
    row8 = lax.broadcasted_iota(jnp.int32, tail_ref.shape, 0)
    tail = tail_ref[...]
    conv = []
    for b0 in range(0, ts, blk):
        xb = xr_ref[0, b0:b0 + blk, :]
        x0 = xb.astype(F32)
        delayed = jnp.dot(delay_mat, xb, preferred_element_type=F32)
        acc = cb_ref[...] + cw_ref[n_delay:n_delay + 1, :] * x0
        for d in range(1, CONV_WIDTH):
            xd = delayed[(d - 1) * blk:d * blk]
            head = jnp.where(row8 < d, pltpu.roll(tail, d, 0), xd[0:SUBLANES])
            xd = jnp.concatenate([head, xd[SUBLANES:]], axis=0)
            acc = acc + cw_ref[n_delay - d:n_delay - d + 1, :] * xd
        tail = x0[blk - SUBLANES:blk]
        conv.append(acc)
    tail_ref[...] = tail
    xc = jnp.concatenate(conv, axis=0)

    xcb = xc.astype(BF16)
    n_grp = wa_ref.shape[0]

    def gate(w_ref, bias_ref):
        parts = [jnp.dot(xcb[:, g * MXU_DIM:(g + 1) * MXU_DIM], w_ref[g],
                         preferred_element_type=F32) for g in range(n_grp)]
        z = jnp.concatenate(parts, axis=1) + bias_ref[...]
        return 0.5 * jnp.tanh(0.5 * z) + 0.5

    r = gate(wa_ref, ba_ref)
    i = gate(wx_ref, bx_ref)
    nl = -lam_ref[...]
    log_a_unit = (-LRU_C) * (jnp.maximum(nl, 0.0) + jnp.log1p(jnp.exp(-jnp.abs(nl))))
    a_all = jnp.exp2(r * (log_a_unit * math.log2(math.e)))
    t = jnp.tanh(r * log_a_unit)
    u = -2.0 * t
    coef = jnp.where(u > 0.0, u * lax.rsqrt(u * (1.0 - t)), 0.0)
    b_all = coef * (i * xc)
    n_slab = a_ref.shape[0]
    for s in range(n_slab):
        a_ref[s] = a_all[:, s * LANES:(s + 1) * LANES]
        b_ref[s] = b_all[:, s * LANES:(s + 1) * LANES]

    row = lax.broadcasted_iota(jnp.int32, (SUBLANES, LANES), 0)
    block_rows = SUBLANES * SUBLANES

    def scan_block(blk, carry):
        base = pl.multiple_of(blk * block_rows, block_rows)
        new_carry = []
        for s in range(n_slab):
            h_loc, a_run = [], []
            for j in range(SUBLANES):
                rows_j = pl.ds(base + j, SUBLANES, stride=SUBLANES)
                a = a_ref[s, rows_j, :]
                b = b_ref[s, rows_j, :]
                h_loc.append(b if j == 0 else a * h_loc[-1] + b)
                a_run.append(a if j == 0 else a * a_run[-1])
            pa, pb = a_run[-1], h_loc[-1]
            for dist in (1, 2, 4):
                keep = row >= dist
                pb = jnp.where(keep, pa * pltpu.roll(pb, dist, 0) + pb, pb)
                pa = jnp.where(keep, pa * pltpu.roll(pa, dist, 0), pa)
            after = pa * carry[s] + pb
            h_in = jnp.where(row == 0, carry[s], pltpu.roll(after, 1, 0))
            for j in range(SUBLANES):
                rows_j = pl.ds(base + j, SUBLANES, stride=SUBLANES)
                b_ref[s, rows_j, :] = a_run[j] * h_in + h_loc[j]
            new_carry.append(jnp.broadcast_to(after[SUBLANES - 1:SUBLANES, :], after.shape))
        return tuple(new_carry)

    carry = lax.fori_loop(0, ts // block_rows, scan_block,
                          tuple(hc_ref[s] for s in range(n_slab)), unroll=2)
    for s in range(n_slab):
        hc_ref[s] = carry[s]
    h_all = jnp.concatenate([b_ref[s] for s in range(n_slab)], axis=1)
    o_ref[0] = (h_all * sgr_ref[0].astype(F32)).astype(BF16)


def _rglru(xr, sgr, cw, cb, wa, ba, wx, bx, lam, ts):
    bsz, seq, c = xr.shape
    consts = [cw, cb, wa, ba, wx, bx, lam]
    return pl.pallas_call(
        functools.partial(_rglru_kernel, ts=ts),
        grid=(bsz, seq // ts),
        in_specs=[
            pl.BlockSpec((1, ts, c), lambda b, i: (b, i, 0)),
            pl.BlockSpec((1, ts, c), lambda b, i: (b, i, 0)),
        ] + [_const_spec(a.shape) for a in consts],
        out_specs=pl.BlockSpec((1, ts, c), lambda b, i: (b, i, 0)),
        out_shape=jax.ShapeDtypeStruct((bsz, seq, c), BF16),
        scratch_shapes=[
            pltpu.VMEM((SUBLANES, c), F32),
            pltpu.VMEM((c // LANES, ts, LANES), F32),
            pltpu.VMEM((c // LANES, ts, LANES), F32),
            pltpu.VMEM((c // LANES, SUBLANES, LANES), F32),
        ],
        compiler_params=pltpu.CompilerParams(
            dimension_semantics=("arbitrary", "arbitrary"), vmem_limit_bytes=VMEM_LIMIT),
        name="rglru",
    )(xr, sgr, *consts)


def _out_proj_kernel(x_ref, yr_ref, ya_ref, w32_ref, mod_ref, o_ref, w_ref):
    @pl.when((pl.program_id(0) == 0) & (pl.program_id(1) == 0))
    def _():
        w_ref[...] = w32_ref[...].astype(BF16)

    d_rnn = yr_ref.shape[2]
    y = jnp.dot(yr_ref[0], w_ref[0:d_rnn, :], preferred_element_type=F32)
    y = y + jnp.dot(ya_ref[0], w_ref[d_rnn:, :], preferred_element_type=F32)
    o_ref[0] = x_ref[0] + mod_ref[0, 2:3, :] * y


def _out_proj(x, y_rnn, y_att, w_o, mod, tm):
    bsz, seq, d = x.shape
    return pl.pallas_call(
        _out_proj_kernel,
        grid=(bsz, seq // tm),
        in_specs=[
            pl.BlockSpec((1, tm, d), lambda b, i: (b, i, 0)),
            pl.BlockSpec((1, tm, y_rnn.shape[2]), lambda b, i: (b, i, 0)),
            pl.BlockSpec((1, tm, y_att.shape[2]), lambda b, i: (b, i, 0)),
            _const_spec(w_o.shape),
            pl.BlockSpec((1, 3, d), lambda b, i: (b, 0, 0)),
        ],
        out_specs=pl.BlockSpec((1, tm, d), lambda b, i: (b, i, 0)),
        out_shape=jax.ShapeDtypeStruct((bsz, seq, d), F32),
        scratch_shapes=[pltpu.VMEM(w_o.shape, BF16)],
        compiler_params=pltpu.CompilerParams(
            dimension_semantics=("arbitrary", "arbitrary"), vmem_limit_bytes=VMEM_LIMIT),
        name="out_proj",
    )(x, y_rnn, y_att, w_o, mod)


def _block_diag_groups(w):
    nb, n, _ = w.shape
    per = MXU_DIM // n
    w = w.reshape(nb // per, per, n, n)
    eye = jnp.eye(per, dtype=w.dtype)
    return jnp.einsum("gpij,pq->gpiqj", w, eye).reshape(nb // per, MXU_DIM, MXU_DIM)


def kernel(x, c, positions, w_ada, b_ada, w_in, conv_w, conv_b, w_rg_a, b_rg_a, w_rg_x, b_rg_x,
           lru_lambda, q_a_norm, w_uq, kv_a_norm, w_ukv, q_norm_nope, q_norm_rope, k_norm_nope,
           k_norm_rope, w_out):
    bsz, seq, d = x.shape
    depth = w_in.shape[0]
    d_rnn = conv_w.shape[2]
    q_lora = q_a_norm.shape[1]
    kv_lora = kv_a_norm.shape[1]
    tile = 512

    inv_freq = 1.0 / (ROPE_THETA ** (jnp.arange(0, QK_ROPE, 2, dtype=F32) / QK_ROPE))
    pos = positions.astype(F32)
    pos_row = pos.reshape(bsz, 1, seq)
    inv_col = inv_freq.reshape(-1, 1)
    c_pad = jnp.zeros((SUBLANES, d), F32).at[:bsz].set(c)

    for l in range(depth):
        mod = _adaln(c_pad, w_ada[l], b_ada[l].reshape(1, -1))[:bsz].reshape(bsz, 3, d)

        ukv = w_ukv[l].reshape(kv_lora, MLA_HEADS, QK_NOPE + V_HEAD)
        w = dict(
            d_rnn=d_rnn, wt=jnp.swapaxes(w_in[l], 0, 1).astype(BF16),
            qan=q_a_norm[l].reshape(1, -1), kvan=kv_a_norm[l].reshape(1, -1),
            wuqt=w_uq[l].T.astype(BF16),
            wuk=ukv[:, :, :QK_NOPE].reshape(kv_lora, -1).astype(BF16),
            wuvt=ukv[:, :, QK_NOPE:].reshape(kv_lora, -1).T.astype(BF16),
            gqn=q_norm_nope[l].reshape(-1, 1), gqr=q_norm_rope[l].reshape(-1, 1),
            gkn=k_norm_nope[l].reshape(1, -1), gkr=k_norm_rope[l].reshape(-1, 1),
        )
        xr, sgr, sga, q_t, k, v_t, bound = _in_proj(x, mod, pos_row, inv_col, w, tile)

        y_att = lax.cond(
            jnp.max(bound) < MAX_SAFE_SCORE_BOUND,
            functools.partial(_attention, tq=tile, running_max=False),
            functools.partial(_attention, tq=tile, running_max=True),
            q_t, k, v_t, sga)
        y_rnn = _rglru(
            xr, sgr, conv_w[l], conv_b[l].reshape(1, -1),
            _block_diag_groups(w_rg_a[l]).astype(BF16), b_rg_a[l].reshape(1, -1),
            _block_diag_groups(w_rg_x[l]).astype(BF16), b_rg_x[l].reshape(1, -1),
            lru_lambda[l].reshape(1, -1), tile)

        x = _out_proj(x, y_rnn, y_att, w_out[l], mod, tile)
    return x
```

```python
import functools
import math

import jax
import jax.numpy as jnp
from jax import lax
from jax.experimental import pallas as pl
from jax.experimental.pallas import tpu as pltpu

RNN_BLOCKS = 16
CONV_WIDTH = 4
LRU_C = 8.0
MLA_HEADS = 8
QK_NOPE = 128
QK_ROPE = 64
V_HEAD = 128
ROPE_THETA = 10000.0
EPS = 1e-6

QK_PAD = 256
MXU_DIM = 256
VMEM_LIMIT = 56 * 1024 * 1024

F32 = jnp.float32
BF16 = jnp.bfloat16

NT_DIMS = (((1,), (1,)), ((), ()))


def _const_spec(shape):
    nd = len(shape)
    return pl.BlockSpec(shape, lambda *_: (0,) * nd, pipeline_mode=pl.Buffered(1))


def _adaln_kernel(c_ref, w_ref, b_ref, o_ref):
    c = c_ref[...]
    c_act = (c * jax.nn.sigmoid(c)).astype(BF16)
    o_ref[...] = jnp.dot(c_act, w_ref[...].astype(BF16), preferred_element_type=F32) + b_ref[...]


def _adaln(c_pad, w_ada, b_ada):
    rows, d = c_pad.shape
    n = w_ada.shape[1]
    tn = 1536
    return pl.pallas_call(
        _adaln_kernel,
        grid=(n // tn,),
        in_specs=[
            pl.BlockSpec((rows, d), lambda j: (0, 0)),
            pl.BlockSpec((d, tn), lambda j: (0, j)),
            pl.BlockSpec((1, tn), lambda j: (0, j)),
        ],
        out_specs=pl.BlockSpec((rows, tn), lambda j: (0, j)),
        out_shape=jax.ShapeDtypeStruct((rows, n), F32),
        compiler_params=pltpu.CompilerParams(
            dimension_semantics=("arbitrary",), vmem_limit_bytes=VMEM_LIMIT),
        name="adaln_mod",
    )(c_pad, w_ada, b_ada)


IN_PROJ_ROWS = 256


def _rms(v, axis):
    return v * lax.rsqrt(jnp.mean(v * v, axis=axis, keepdims=True) + EPS)


def _silu(v):
    half = 0.5 * v
    return half * jnp.tanh(half) + half


def _in_proj_kernel(x_ref, mod_ref, posr_ref, invc_ref, wt_ref,
                    qan_ref, kvan_ref, wuqt_ref, wuk_ref, wuvt_ref,
                    gqn_ref, gqr_ref, gkn_ref, gkr_ref,
                    xr_ref, sgr_ref, sga_ref, qt_ref, k_ref, vt_ref, bound_ref, *, q_scale):
    d_rnn, d_att = xr_ref.shape[2], sga_ref.shape[2]
    q_lora, kv_lora = qan_ref.shape[1], kvan_ref.shape[1]
    qc0, kvc0 = 2 * d_rnn, 2 * d_rnn + q_lora
    kr0 = kvc0 + kv_lora
    ga0 = kr0 + QK_ROPE
    scale = mod_ref[0, 1:2, :]
    shift = mod_ref[0, 0:1, :]
    hd = QK_NOPE + QK_ROPE
    half = QK_ROPE // 2
    gkn, gkr = gkn_ref[...], gkr_ref[...]
    k_bound = jnp.sqrt(QK_NOPE * jnp.max(gkn * gkn, axis=1, keepdims=True)
                       + QK_ROPE * jnp.max(gkr * gkr, axis=0, keepdims=True))
    pad_row = lax.broadcasted_iota(jnp.int32, (QK_PAD - hd, IN_PROJ_ROWS), 0)

    chunks = [slice(r0, r0 + IN_PROJ_ROWS) for r0 in range(0, x_ref.shape[1], IN_PROJ_ROWS)]
    hs = [(_rms(x_ref[0, rs, :], -1) * (1.0 + scale) + shift).astype(BF16) for rs in chunks]
    for rs, h in zip(chunks, hs):

        def proj(lo, hi):
            return lax.dot_general(h, wt_ref[lo:hi, :], NT_DIMS, preferred_element_type=F32)

        qcn = (_rms(proj(qc0, kvc0), -1) * qan_ref[...]).astype(BF16)
        kvcn = (_rms(proj(kvc0, kr0), -1) * kvan_ref[...]).astype(BF16)
        kr_t = proj(kr0, kr0 + 128).T[:QK_ROPE]
        q_t = lax.dot_general(wuqt_ref[...], qcn, NT_DIMS, preferred_element_type=F32)
        v_t = lax.dot_general(wuvt_ref[...], kvcn, NT_DIMS, preferred_element_type=F32)
        kn_all = jnp.dot(kvcn, wuk_ref[...], preferred_element_type=F32)

        ang_t = invc_ref[...] * posr_ref[0, :, rs]
        cos_t, sin_t = jnp.cos(ang_t), jnp.sin(ang_t)

        def rope_t(v):
            v1, v2 = v[:half], v[half:]
            return v1 * cos_t - v2 * sin_t, v1 * sin_t + v2 * cos_t

        bound_max = None
        for hh in range(MLA_HEADS):
            qn = q_t[hh * hd:hh * hd + QK_NOPE]
            qn = _rms(qn, 0) * (gqn_ref[...] * q_scale)
            qr = q_t[hh * hd + QK_NOPE:(hh + 1) * hd]
            r1, r2 = rope_t(_rms(qr, 0) * (gqr_ref[...] * q_scale))
            q_sq = (jnp.sum(qn * qn, axis=0, keepdims=True)
                    + jnp.sum(r1 * r1 + r2 * r2, axis=0, keepdims=True))
            bound = jnp.sqrt(q_sq) * k_bound
            bound_max = bound if bound_max is None else jnp.maximum(bound_max, bound)
            qt_ref[0, hh, 0:QK_NOPE, rs] = qn.astype(BF16)
            qt_ref[0, hh, QK_NOPE:QK_NOPE + half, rs] = r1.astype(BF16)
            qt_ref[0, hh, QK_NOPE + half:hd, rs] = r2.astype(BF16)
            qt_ref[0, hh, hd:QK_PAD, rs] = jnp.where(pad_row == 0, -bound, 0.0).astype(BF16)
        bound_ref[0, :, rs] = bound_max

        for hh in range(MLA_HEADS):
            vt_ref[0, hh, :, rs] = v_t[hh * V_HEAD:(hh + 1) * V_HEAD].astype(BF16)
        k1, k2 = rope_t(_rms(kr_t, 0) * gkr)
        kpe_t = jnp.concatenate([k1, k2, jnp.where(pad_row == 0, 1.0, 0.0)], axis=0)
        kpe = kpe_t.T.astype(BF16)
        for hh in range(MLA_HEADS):
            kn = kn_all[:, hh * QK_NOPE:(hh + 1) * QK_NOPE]
            k_ref[0, hh, rs, 0:QK_NOPE] = (_rms(kn, -1) * gkn).astype(BF16)
            k_ref[0, hh, rs, QK_NOPE:QK_PAD] = kpe

        xr_ref[0, rs, :] = proj(0, d_rnn).astype(BF16)
        gr = proj(d_rnn, 2 * d_rnn)
        sgr_ref[0, rs, :] = _silu(gr).astype(BF16)
        ga = proj(ga0, ga0 + d_att)
        sga_ref[0, rs, :] = _silu(ga).astype(BF16)


def _in_proj(x, mod, pos_row, inv_col, w, tm):
    bsz, seq, d = x.shape
    hd = QK_NOPE + QK_ROPE
    q_scale = (hd ** -0.5) * math.log2(math.e)
    consts = [w["wt"],
              w["qan"], w["kvan"], w["wuqt"], w["wuk"], w["wuvt"],
              w["gqn"], w["gqr"], w["gkn"], w["gkr"]]
    d_rnn, d_att = w["d_rnn"], MLA_HEADS * V_HEAD
    return pl.pallas_call(
        functools.partial(_in_proj_kernel, q_scale=q_scale),
        grid=(bsz, seq // tm),
        in_specs=[
            pl.BlockSpec((1, tm, d), lambda b, i: (b, i, 0)),
            pl.BlockSpec((1, 3, d), lambda b, i: (b, 0, 0)),
            pl.BlockSpec((1, 1, tm), lambda b, i: (b, 0, i)),
            _const_spec(inv_col.shape),
        ] + [_const_spec(a.shape) for a in consts],
        out_specs=[
            pl.BlockSpec((1, tm, d_rnn), lambda b, i: (b, i, 0)),
            pl.BlockSpec((1, tm, d_rnn), lambda b, i: (b, i, 0)),
            pl.BlockSpec((1, tm, d_att), lambda b, i: (b, i, 0)),
            pl.BlockSpec((1, MLA_HEADS, QK_PAD, tm), lambda b, i: (b, 0, 0, i)),
            pl.BlockSpec((1, MLA_HEADS, tm, QK_PAD), lambda b, i: (b, 0, i, 0)),
            pl.BlockSpec((1, MLA_HEADS, V_HEAD, tm), lambda b, i: (b, 0, 0, i)),
            pl.BlockSpec((1, 1, tm), lambda b, i: (b, 0, i)),
        ],
        out_shape=[
            jax.ShapeDtypeStruct((bsz, seq, d_rnn), BF16),
            jax.ShapeDtypeStruct((bsz, seq, d_rnn), BF16),
            jax.ShapeDtypeStruct((bsz, seq, d_att), BF16),
            jax.ShapeDtypeStruct((bsz, MLA_HEADS, QK_PAD, seq), BF16),
            jax.ShapeDtypeStruct((bsz, MLA_HEADS, seq, QK_PAD), BF16),
            jax.ShapeDtypeStruct((bsz, MLA_HEADS, V_HEAD, seq), BF16),
            jax.ShapeDtypeStruct((bsz, 1, seq), F32),
        ],
        compiler_params=pltpu.CompilerParams(
            dimension_semantics=("arbitrary", "arbitrary"), vmem_limit_bytes=VMEM_LIMIT),
        name="in_proj",
    )(x, mod, pos_row, inv_col, *consts)


ATTN_HEADS_PER_STEP = 4
ATTN_KEY_SUB = 256
ATTN_BLOCKS_PER_TRIP = 8
MAX_SAFE_SCORE_BOUND = 56.0


def _attn_kernel(qt_ref, k_ref, vt_ref, sga_ref, o_ref, acc_ref, *, tq, running_max):
    qi = pl.program_id(2)
    nh = qt_ref.shape[1]

    sub = tq if running_max else ATTN_KEY_SUB

    def first_query(unit, masked):
        return unit[2] * sub if masked[unit[0]] else 0

    def scores(unit, js, masked):
        bi, hh, c = unit
        key0 = pl.multiple_of(js[bi] * tq + c * sub, sub)
        k_sub = k_ref[0, hh, pl.ds(key0, sub), :]
        q_t = qt_ref[0, hh, :, first_query(unit, masked):]
        return jnp.dot(k_sub, q_t, preferred_element_type=F32)

    def accumulate(unit, js, s_t, m, l, masked, init):
        bi, hh, c = unit
        q0 = first_query(unit, masked)
        assert not (init and q0), "the unit that initialises acc must cover every query"
        if masked[bi]:
            key = lax.broadcasted_iota(jnp.int32, s_t.shape, 0) + c * sub
            qry = lax.broadcasted_iota(jnp.int32, s_t.shape, 1) + q0
            s_t = jnp.where(key <= qry, s_t, -jnp.inf)
        key0 = pl.multiple_of(js[bi] * tq + c * sub, sub)
        v_sub = vt_ref[0, hh, :, pl.ds(key0, sub)]
        if running_max:
            m_new = jnp.maximum(m, jnp.max(s_t, axis=0, keepdims=True))
            alpha = jnp.exp2(m - m_new)
            p_t = jnp.exp2(s_t - m_new)
            l_new = alpha * l + jnp.sum(p_t, axis=0, keepdims=True)
            pv = jnp.dot(v_sub, p_t.astype(BF16), preferred_element_type=F32)
            acc_ref[hh] = pv if init else alpha * acc_ref[hh] + pv
        else:
            m_new = m
            p_t = jnp.exp2(s_t)
            l_part = l[:, q0:] + jnp.sum(p_t, axis=0, keepdims=True)
            l_new = l_part if q0 == 0 else jnp.concatenate([l[:, :q0], l_part], axis=1)
            pv = jnp.dot(v_sub, p_t.astype(BF16), preferred_element_type=F32)
            if init:
                acc_ref[hh] = pv
            else:
                acc_ref[hh, :, q0:] += pv
        return m_new, l_new

    def blocks(js, masked, carry, first=False):
        units = [(bi, hh, c) for bi in range(len(js)) for hh in range(nh)
                 for c in range(tq // sub)]
        carry = list(carry)
        s_next = scores(units[0], js, masked)
        for idx, unit in enumerate(units):
            s_cur = s_next
            if idx + 1 < len(units):
                s_next = scores(units[idx + 1], js, masked)
            hh = unit[1]
            init = first and unit[0] == 0 and unit[2] == 0
            carry[hh] = accumulate(unit, js, s_cur, carry[hh][0], carry[hh][1], masked, init)
        return tuple(carry)

    carry = tuple((jnp.full((1, tq), -jnp.inf, F32), jnp.zeros((1, tq), F32)) for _ in range(nh))
    if running_max:
        carry = blocks([qi], [True], carry, first=True)
        carry = lax.fori_loop(0, qi, lambda j, c: blocks([j], [False], c), carry)
    else:
        g = ATTN_BLOCKS_PER_TRIP
        tails = [functools.partial(blocks, [qi - r + t for t in range(r + 1)], [False] * r + [True],
                                   first=True) for r in range(g)]
        carry = lax.switch(qi % g, tails, carry)
        if k_ref.shape[2] // tq > g:
            carry = lax.fori_loop(
                0, qi // g,
                lambda p, c: blocks([g * p + t for t in range(g)], [False] * g, c), carry)

    for hh in range(nh):
        o_t = acc_ref[hh] / carry[hh][1]
        gate = sga_ref[0, :, hh * V_HEAD:(hh + 1) * V_HEAD].astype(F32)
        o_ref[0, :, hh * V_HEAD:(hh + 1) * V_HEAD] = (o_t.T * gate).astype(BF16)


def _attention(q_t, k, v_t, sga, tq, running_max):
    bsz, nh, _, seq = q_t.shape
    hb = ATTN_HEADS_PER_STEP
    return pl.pallas_call(
        functools.partial(_attn_kernel, tq=tq, running_max=running_max),
        grid=(bsz, nh // hb, seq // tq),
        in_specs=[
            pl.BlockSpec((1, hb, QK_PAD, tq), lambda b, h, i: (b, h, 0, i)),
            pl.BlockSpec((1, hb, seq, QK_PAD), lambda b, h, i: (b, h, 0, 0)),
            pl.BlockSpec((1, hb, V_HEAD, seq), lambda b, h, i: (b, h, 0, 0)),
            pl.BlockSpec((1, tq, hb * V_HEAD), lambda b, h, i: (b, i, h)),
        ],
        out_specs=pl.BlockSpec((1, tq, hb * V_HEAD), lambda b, h, i: (b, i, h)),
        out_shape=jax.ShapeDtypeStruct((bsz, seq, nh * V_HEAD), BF16),
        scratch_shapes=[pltpu.VMEM((hb, V_HEAD, tq), F32)],
        compiler_params=pltpu.CompilerParams(
            dimension_semantics=("arbitrary", "arbitrary", "arbitrary"),
            vmem_limit_bytes=VMEM_LIMIT),
        name="mla_attention_online" if running_max else "mla_attention",
    )(q_t, k, v_t, sga)


SUBLANES = 8
LANES = 128


def _rglru_kernel(xr_ref, sgr_ref, cw_ref, cb_ref, wa_ref, ba_ref, wx_ref, bx_ref, lam_ref,
                  o_ref, tail_ref, a_ref, b_ref, hc_ref, *, ts):
    @pl.when(pl.program_id(1) == 0)
    def _():
        tail_ref[...] = jnp.zeros_like(tail_ref)
        hc_ref[...] = jnp.zeros_like(hc_ref)

    blk = MXU_DIM
    n_delay = CONV_WIDTH - 1
    r_i = lax.broadcasted_iota(jnp.int32, (n_delay * blk, blk), 0)
    c_i = lax.broadcasted_iota(jnp.int32, (n_delay * blk, blk), 1)
    delay_mat = jnp.where((r_i % blk) - c_i == r_i // blk + 1, 1.0, 0.0).astype(BF16)
    row8 = lax.broadcasted_iota(jnp.int32, tail_ref.shape, 0)
    tail = tail_ref[...]
    conv = []
    for b0 in range(0, ts, blk):
        xb = xr_ref[0, b0:b0 + blk, :]
        x0 = xb.astype(F32)
        delayed = jnp.dot(delay_mat, xb, preferred_element_type=F32)
        acc = cb_ref[...] + cw_ref[n_delay:n_delay + 1, :] * x0
        for d in range(1, CONV_WIDTH):
            xd = delayed[(d - 1) * blk:d * blk]
            head = jnp.where(row8 < d, pltpu.roll(tail, d, 0), xd[0:SUBLANES])
            xd = jnp.concatenate([head, xd[SUBLANES:]], axis=0)
            acc = acc + cw_ref[n_delay - d:n_delay - d + 1, :] * xd
        tail = x0[blk - SUBLANES:blk]
        conv.append(acc)
    tail_ref[...] = tail
    xc = jnp.concatenate(conv, axis=0)

    xcb = xc.astype(BF16)
    n_grp = wa_ref.shape[0]

    def gate(w_ref, bias_ref):
        parts = [jnp.dot(xcb[:, g * MXU_DIM:(g + 1) * MXU_DIM], w_ref[g],
                         preferred_element_type=F32) for g in range(n_grp)]
        z = jnp.concatenate(parts, axis=1) + bias_ref[...]
        return 0.5 * jnp.tanh(0.5 * z) + 0.5

    r = gate(wa_ref, ba_ref)
    i = gate(wx_ref, bx_ref)
    nl = -lam_ref[...]
    log_a_unit = (-LRU_C) * (jnp.maximum(nl, 0.0) + jnp.log1p(jnp.exp(-jnp.abs(nl))))
    a_all = jnp.exp2(r * (log_a_unit * math.log2(math.e)))
    t = jnp.tanh(r * log_a_unit)
    u = -2.0 * t
    coef = jnp.where(u > 0.0, u * lax.rsqrt(u * (1.0 - t)), 0.0)
    b_all = coef * (i * xc)
    n_slab = a_ref.shape[0]
    for s in range(n_slab):
        a_ref[s] = a_all[:, s * LANES:(s + 1) * LANES]
        b_ref[s] = b_all[:, s * LANES:(s + 1) * LANES]

    row = lax.broadcasted_iota(jnp.int32, (SUBLANES, LANES), 0)
    block_rows = SUBLANES * SUBLANES

    def scan_block(blk, carry):
        base = pl.multiple_of(blk * block_rows, block_rows)
        new_carry = []
        for s in range(n_slab):
            h_loc, a_run = [], []
            for j in range(SUBLANES):
                rows_j = pl.ds(base + j, SUBLANES, stride=SUBLANES)
                a = a_ref[s, rows_j, :]
                b = b_ref[s, rows_j, :]
                h_loc.append(b if j == 0 else a * h_loc[-1] + b)
                a_run.append(a if j == 0 else a * a_run[-1])
            pa, pb = a_run[-1], h_loc[-1]
            for dist in (1, 2, 4):
                keep = row >= dist
                pb = jnp.where(keep, pa * pltpu.roll(pb, dist, 0) + pb, pb)
                pa = jnp.where(keep, pa * pltpu.roll(pa, dist, 0), pa)
            after = pa * carry[s] + pb
            h_in = jnp.where(row == 0, carry[s], pltpu.roll(after, 1, 0))
            for j in range(SUBLANES):
                rows_j = pl.ds(base + j, SUBLANES, stride=SUBLANES)
                b_ref[s, rows_j, :] = a_run[j] * h_in + h_loc[j]
            new_carry.append(jnp.broadcast_to(after[SUBLANES - 1:SUBLANES, :], after.shape))
        return tuple(new_carry)

    carry = lax.fori_loop(0, ts // block_rows, scan_block,
                          tuple(hc_ref[s] for s in range(n_slab)), unroll=2)
    for s in range(n_slab):
        hc_ref[s] = carry[s]
    h_all = jnp.concatenate([b_ref[s] for s in range(n_slab)], axis=1)
    o_ref[0] = (h_all * sgr_ref[0].astype(F32)).astype(BF16)


def _rglru(xr, sgr, cw, cb, wa, ba, wx, bx, lam, ts):
    bsz, seq, c = xr.shape
    consts = [cw, cb, wa, ba, wx, bx, lam]
    return pl.pallas_call(
        functools.partial(_rglru_kernel, ts=ts),
        grid=(bsz, seq // ts),
        in_specs=[
            pl.BlockSpec((1, ts, c), lambda b, i: (b, i, 0)),
            pl.BlockSpec((1, ts, c), lambda b, i: (b, i, 0)),
        ] + [_const_spec(a.shape) for a in consts],
        out_specs=pl.BlockSpec((1, ts, c), lambda b, i: (b, i, 0)),
        out_shape=jax.ShapeDtypeStruct((bsz, seq, c), BF16),
        scratch_shapes=[
            pltpu.VMEM((SUBLANES, c), F32),
            pltpu.VMEM((c // LANES, ts, LANES), F32),
            pltpu.VMEM((c // LANES, ts, LANES), F32),
            pltpu.VMEM((c // LANES, SUBLANES, LANES), F32),
        ],
        compiler_params=pltpu.CompilerParams(
            dimension_semantics=("arbitrary", "arbitrary"), vmem_limit_bytes=VMEM_LIMIT),
        name="rglru",
    )(xr, sgr, *consts)


def _out_proj_kernel(x_ref, yr_ref, ya_ref, w32_ref, mod_ref, o_ref, w_ref):
    @pl.when((pl.program_id(0) == 0) & (pl.program_id(1) == 0))
    def _():
        w_ref[...] = w32_ref[...].astype(BF16)

    d_rnn = yr_ref.shape[2]
    y = jnp.dot(yr_ref[0], w_ref[0:d_rnn, :], preferred_element_type=F32)
    y = y + jnp.dot(ya_ref[0], w_ref[d_rnn:, :], preferred_element_type=F32)
    o_ref[0] = x_ref[0] + mod_ref[0, 2:3, :] * y


def _out_proj(x, y_rnn, y_att, w_o, mod, tm):
    bsz, seq, d = x.shape
    return pl.pallas_call(
        _out_proj_kernel,
        grid=(bsz, seq // tm),
        in_specs=[
            pl.BlockSpec((1, tm, d), lambda b, i: (b, i, 0)),
            pl.BlockSpec((1, tm, y_rnn.shape[2]), lambda b, i: (b, i, 0)),
            pl.BlockSpec((1, tm, y_att.shape[2]), lambda b, i: (b, i, 0)),
            _const_spec(w_o.shape),
            pl.BlockSpec((1, 3, d), lambda b, i: (b, 0, 0)),
        ],
        out_specs=pl.BlockSpec((1, tm, d), lambda b, i: (b, i, 0)),
        out_shape=jax.ShapeDtypeStruct((bsz, seq, d), F32),
        scratch_shapes=[pltpu.VMEM(w_o.shape, BF16)],
        compiler_params=pltpu.CompilerParams(
            dimension_semantics=("arbitrary", "arbitrary"), vmem_limit_bytes=VMEM_LIMIT),
        name="out_proj",
    )(x, y_rnn, y_att, w_o, mod)


def _block_diag_groups(w):
    nb, n, _ = w.shape
    per = MXU_DIM // n
    w = w.reshape(nb // per, per, n, n)
    eye = jnp.eye(per, dtype=w.dtype)
    return jnp.einsum("gpij,pq->gpiqj", w, eye).reshape(nb // per, MXU_DIM, MXU_DIM)


def kernel(x, c, positions, w_ada, b_ada, w_in, conv_w, conv_b, w_rg_a, b_rg_a, w_rg_x, b_rg_x,
           lru_lambda, q_a_norm, w_uq, kv_a_norm, w_ukv, q_norm_nope, q_norm_rope, k_norm_nope,
           k_norm_rope, w_out):
    bsz, seq, d = x.shape
    depth = w_in.shape[0]
    d_rnn = conv_w.shape[2]
    q_lora = q_a_norm.shape[1]
    kv_lora = kv_a_norm.shape[1]
    tile = 512

    inv_freq = 1.0 / (ROPE_THETA ** (jnp.arange(0, QK_ROPE, 2, dtype=F32) / QK_ROPE))
    pos = positions.astype(F32)
    pos_row = pos.reshape(bsz, 1, seq)
    inv_col = inv_freq.reshape(-1, 1)
    c_pad = jnp.zeros((SUBLANES, d), F32).at[:bsz].set(c)

    for l in range(depth):
        mod = _adaln(c_pad, w_ada[l], b_ada[l].reshape(1, -1))[:bsz].reshape(bsz, 3, d)

        ukv = w_ukv[l].reshape(kv_lora, MLA_HEADS, QK_NOPE + V_HEAD)
        w = dict(
            d_rnn=d_rnn, wt=jnp.swapaxes(w_in[l], 0, 1).astype(BF16),
            qan=q_a_norm[l].reshape(1, -1), kvan=kv_a_norm[l].reshape(1, -1),
            wuqt=w_uq[l].T.astype(BF16),
            wuk=ukv[:, :, :QK_NOPE].reshape(kv_lora, -1).astype(BF16),
            wuvt=ukv[:, :, QK_NOPE:].reshape(kv_lora, -1).T.astype(BF16),
            gqn=q_norm_nope[l].reshape(-1, 1), gqr=q_norm_rope[l].reshape(-1, 1),
            gkn=k_norm_nope[l].reshape(1, -1), gkr=k_norm_rope[l].reshape(-1, 1),
        )
        xr, sgr, sga, q_t, k, v_t, bound = _in_proj(x, mod, pos_row, inv_col, w, tile)

        y_att = lax.cond(
            jnp.max(bound) < MAX_SAFE_SCORE_BOUND,
            functools.partial(_attention, tq=tile, running_max=False),
            functools.partial(_attention, tq=tile, running_max=True),
            q_t, k, v_t, sga)
        y_rnn = _rglru(
            xr, sgr, conv_w[l], conv_b[l].reshape(1, -1),
            _block_diag_groups(w_rg_a[l]).astype(BF16), b_rg_a[l].reshape(1, -1),
            _block_diag_groups(w_rg_x[l]).astype(BF16), b_rg_x[l].reshape(1, -1),
            lru_lambda[l].reshape(1, -1), tile)

        x = _out_proj(x, y_rnn, y_att, w_out[l], mod, tile)
    return x
```

```python
import functools
import math

import jax
import jax.numpy as jnp
from jax import lax
from jax.experimental import pallas as pl
from jax.experimental.pallas import tpu as pltpu

RNN_BLOCKS = 16
CONV_WIDTH = 4
LRU_C = 8.0
MLA_HEADS = 8
QK_NOPE = 128
QK_ROPE = 64
V_HEAD = 128
ROPE_THETA = 10000.0
EPS = 1e-6

QK_PAD = 256
MXU_DIM = 256
VMEM_LIMIT = 56 * 1024 * 1024

F32 = jnp.float32
BF16 = jnp.bfloat16

NT_DIMS = (((1,), (1,)), ((), ()))


def _const_spec(shape):
    nd = len(shape)
    return pl.BlockSpec(shape, lambda *_: (0,) * nd, pipeline_mode=pl.Buffered(1))


def _adaln_kernel(c_ref, w_ref, b_ref, o_ref):
    c = c_ref[...]
    c_act = (c * jax.nn.sigmoid(c)).astype(BF16)
    o_ref[...] = jnp.dot(c_act, w_ref[...].astype(BF16), preferred_element_type=F32) + b_ref[...]


def _adaln(c_pad, w_ada, b_ada):
    rows, d = c_pad.shape
    n = w_ada.shape[1]
    tn = 768
    return pl.pallas_call(
        _adaln_kernel,
        grid=(n // tn,),
        in_specs=[
            pl.BlockSpec((rows, d), lambda j: (0, 0)),
            pl.BlockSpec((d, tn), lambda j: (0, j)),
            pl.BlockSpec((1, tn), lambda j: (0, j)),
        ],
        out_specs=pl.BlockSpec((rows, tn), lambda j: (0, j)),
        out_shape=jax.ShapeDtypeStruct((rows, n), F32),
        compiler_params=pltpu.CompilerParams(
            dimension_semantics=("arbitrary",), vmem_limit_bytes=VMEM_LIMIT),
        name="adaln_mod",
    )(c_pad, w_ada, b_ada)


IN_PROJ_ROWS = 256


def _rms(v, axis):
    return v * lax.rsqrt(jnp.mean(v * v, axis=axis, keepdims=True) + EPS)


def _silu(v):
    half = 0.5 * v
    return half * jnp.tanh(half) + half


def _in_proj_kernel(x_ref, mod_ref, posr_ref, invc_ref, wt_ref,
                    qan_ref, kvan_ref, wuqt_ref, wuk_ref, wuvt_ref,
                    gqn_ref, gqr_ref, gkn_ref, gkr_ref,
                    xr_ref, sgr_ref, sga_ref, qt_ref, k_ref, vt_ref, bound_ref, *, q_scale):
    d_rnn, d_att = xr_ref.shape[2], sga_ref.shape[2]
    q_lora, kv_lora = qan_ref.shape[1], kvan_ref.shape[1]
    qc0, kvc0 = 2 * d_rnn, 2 * d_rnn + q_lora
    kr0 = kvc0 + kv_lora
    ga0 = kr0 + QK_ROPE
    scale = mod_ref[0, 1:2, :]
    shift = mod_ref[0, 0:1, :]
    hd = QK_NOPE + QK_ROPE
    half = QK_ROPE // 2
    gkn, gkr = gkn_ref[...], gkr_ref[...]
    k_bound = jnp.sqrt(QK_NOPE * jnp.max(gkn * gkn, axis=1, keepdims=True)
                       + QK_ROPE * jnp.max(gkr * gkr, axis=0, keepdims=True))
    pad_row = lax.broadcasted_iota(jnp.int32, (QK_PAD - hd, IN_PROJ_ROWS), 0)

    chunks = [slice(r0, r0 + IN_PROJ_ROWS) for r0 in range(0, x_ref.shape[1], IN_PROJ_ROWS)]
    hs = [(_rms(x_ref[0, rs, :], -1) * (1.0 + scale) + shift).astype(BF16) for rs in chunks]
    for rs, h in zip(chunks, hs):

        def proj(lo, hi):
            return lax.dot_general(h, wt_ref[lo:hi, :], NT_DIMS, preferred_element_type=F32)

        qcn = (_rms(proj(qc0, kvc0), -1) * qan_ref[...]).astype(BF16)
        kvcn = (_rms(proj(kvc0, kr0), -1) * kvan_ref[...]).astype(BF16)
        kr_t = proj(kr0, kr0 + 128).T[:QK_ROPE]
        q_t = lax.dot_general(wuqt_ref[...], qcn, NT_DIMS, preferred_element_type=F32)
        v_t = lax.dot_general(wuvt_ref[...], kvcn, NT_DIMS, preferred_element_type=F32)
        kn_all = jnp.dot(kvcn, wuk_ref[...], preferred_element_type=F32)

        ang_t = invc_ref[...] * posr_ref[0, :, rs]
        cos_t, sin_t = jnp.cos(ang_t), jnp.sin(ang_t)

        def rope_t(v):
            v1, v2 = v[:half], v[half:]
            return v1 * cos_t - v2 * sin_t, v1 * sin_t + v2 * cos_t

        bound_max = None
        for hh in range(MLA_HEADS):
            qn = q_t[hh * hd:hh * hd + QK_NOPE]
            qn = _rms(qn, 0) * (gqn_ref[...] * q_scale)
            qr = q_t[hh * hd + QK_NOPE:(hh + 1) * hd]
            r1, r2 = rope_t(_rms(qr, 0) * (gqr_ref[...] * q_scale))
            q_sq = (jnp.sum(qn * qn, axis=0, keepdims=True)
                    + jnp.sum(r1 * r1 + r2 * r2, axis=0, keepdims=True))
            bound = jnp.sqrt(q_sq) * k_bound
            bound_max = bound if bound_max is None else jnp.maximum(bound_max, bound)
            qt_ref[0, hh, 0:QK_NOPE, rs] = qn.astype(BF16)
            qt_ref[0, hh, QK_NOPE:QK_NOPE + half, rs] = r1.astype(BF16)
            qt_ref[0, hh, QK_NOPE + half:hd, rs] = r2.astype(BF16)
            qt_ref[0, hh, hd:QK_PAD, rs] = jnp.where(pad_row == 0, -bound, 0.0).astype(BF16)
        bound_ref[0, :, rs] = bound_max

        for hh in range(MLA_HEADS):
            vt_ref[0, hh, :, rs] = v_t[hh * V_HEAD:(hh + 1) * V_HEAD].astype(BF16)
        k1, k2 = rope_t(_rms(kr_t, 0) * gkr)
        kpe_t = jnp.concatenate([k1, k2, jnp.where(pad_row == 0, 1.0, 0.0)], axis=0)
        kpe = kpe_t.T.astype(BF16)
        for hh in range(MLA_HEADS):
            kn = kn_all[:, hh * QK_NOPE:(hh + 1) * QK_NOPE]
            k_ref[0, hh, rs, 0:QK_NOPE] = (_rms(kn, -1) * gkn).astype(BF16)
            k_ref[0, hh, rs, QK_NOPE:QK_PAD] = kpe

        xr_ref[0, rs, :] = proj(0, d_rnn).astype(BF16)
        gr = proj(d_rnn, 2 * d_rnn)
        sgr_ref[0, rs, :] = _silu(gr).astype(BF16)
        ga = proj(ga0, ga0 + d_att)
        sga_ref[0, rs, :] = _silu(ga).astype(BF16)


def _in_proj(x, mod, pos_row, inv_col, w, tm):
    bsz, seq, d = x.shape
    hd = QK_NOPE + QK_ROPE
    q_scale = (hd ** -0.5) * math.log2(math.e)
    consts = [w["wt"],
              w["qan"], w["kvan"], w["wuqt"], w["wuk"], w["wuvt"],
              w["gqn"], w["gqr"], w["gkn"], w["gkr"]]
    d_rnn, d_att = w["d_rnn"], MLA_HEADS * V_HEAD
    return pl.pallas_call(
        functools.partial(_in_proj_kernel, q_scale=q_scale),
        grid=(bsz, seq // tm),
        in_specs=[
            pl.BlockSpec((1, tm, d), lambda b, i: (b, i, 0)),
            pl.BlockSpec((1, 3, d), lambda b, i: (b, 0, 0)),
            pl.BlockSpec((1, 1, tm), lambda b, i: (b, 0, i)),
            _const_spec(inv_col.shape),
        ] + [_const_spec(a.shape) for a in consts],
        out_specs=[
            pl.BlockSpec((1, tm, d_rnn), lambda b, i: (b, i, 0)),
            pl.BlockSpec((1, tm, d_rnn), lambda b, i: (b, i, 0)),
            pl.BlockSpec((1, tm, d_att), lambda b, i: (b, i, 0)),
            pl.BlockSpec((1, MLA_HEADS, QK_PAD, tm), lambda b, i: (b, 0, 0, i)),
            pl.BlockSpec((1, MLA_HEADS, tm, QK_PAD), lambda b, i: (b, 0, i, 0)),
            pl.BlockSpec((1, MLA_HEADS, V_HEAD, tm), lambda b, i: (b, 0, 0, i)),
            pl.BlockSpec((1, 1, tm), lambda b, i: (b, 0, i)),
        ],
        out_shape=[
            jax.ShapeDtypeStruct((bsz, seq, d_rnn), BF16),
            jax.ShapeDtypeStruct((bsz, seq, d_rnn), BF16),
            jax.ShapeDtypeStruct((bsz, seq, d_att), BF16),
            jax.ShapeDtypeStruct((bsz, MLA_HEADS, QK_PAD, seq), BF16),
            jax.ShapeDtypeStruct((bsz, MLA_HEADS, seq, QK_PAD), BF16),
            jax.ShapeDtypeStruct((bsz, MLA_HEADS, V_HEAD, seq), BF16),
            jax.ShapeDtypeStruct((bsz, 1, seq), F32),
        ],
        compiler_params=pltpu.CompilerParams(
            dimension_semantics=("arbitrary", "arbitrary"), vmem_limit_bytes=VMEM_LIMIT),
        name="in_proj",
    )(x, mod, pos_row, inv_col, *consts)


ATTN_HEADS_PER_STEP = 4
ATTN_KEY_SUB = 256
ATTN_BLOCKS_PER_TRIP = 4
MAX_SAFE_SCORE_BOUND = 56.0


def _attn_kernel(qt_ref, k_ref, vt_ref, sga_ref, o_ref, acc_ref, *, tq, running_max):
    qi = pl.program_id(2)
    nh = qt_ref.shape[1]

    sub = tq if running_max else ATTN_KEY_SUB

    def first_query(unit, masked):
        return unit[2] * sub if masked[unit[0]] else 0

    def scores(unit, js, masked):
        bi, hh, c = unit
        key0 = pl.multiple_of(js[bi] * tq + c * sub, sub)
        k_sub = k_ref[0, hh, pl.ds(key0, sub), :]
        q_t = qt_ref[0, hh, :, first_query(unit, masked):]
        return jnp.dot(k_sub, q_t, preferred_element_type=F32)

    def accumulate(unit, js, s_t, m, l, masked, init):
        bi, hh, c = unit
        q0 = first_query(unit, masked)
        assert not (init and q0), "the unit that initialises acc must cover every query"
        if masked[bi]:
            key = lax.broadcasted_iota(jnp.int32, s_t.shape, 0) + c * sub
            qry = lax.broadcasted_iota(jnp.int32, s_t.shape, 1) + q0
            s_t = jnp.where(key <= qry, s_t, -jnp.inf)
        key0 = pl.multiple_of(js[bi] * tq + c * sub, sub)
        v_sub = vt_ref[0, hh, :, pl.ds(key0, sub)]
        if running_max:
            m_new = jnp.maximum(m, jnp.max(s_t, axis=0, keepdims=True))
            alpha = jnp.exp2(m - m_new)
            p_t = jnp.exp2(s_t - m_new)
            l_new = alpha * l + jnp.sum(p_t, axis=0, keepdims=True)
            pv = jnp.dot(v_sub, p_t.astype(BF16), preferred_element_type=F32)
            acc_ref[hh] = pv if init else alpha * acc_ref[hh] + pv
        else:
            m_new = m
            p_t = jnp.exp2(s_t)
            l_part = l[:, q0:] + jnp.sum(p_t, axis=0, keepdims=True)
            l_new = l_part if q0 == 0 else jnp.concatenate([l[:, :q0], l_part], axis=1)
            pv = jnp.dot(v_sub, p_t.astype(BF16), preferred_element_type=F32)
            if init:
                acc_ref[hh] = pv
            else:
                acc_ref[hh, :, q0:] += pv
        return m_new, l_new

    def blocks(js, masked, carry, first=False):
        units = [(bi, hh, c) for bi in range(len(js)) for hh in range(nh)
                 for c in range(tq // sub)]
        carry = list(carry)
        s_next = scores(units[0], js, masked)
        for idx, unit in enumerate(units):
            s_cur = s_next
            if idx + 1 < len(units):
                s_next = scores(units[idx + 1], js, masked)
            hh = unit[1]
            init = first and unit[0] == 0 and unit[2] == 0
            carry[hh] = accumulate(unit, js, s_cur, carry[hh][0], carry[hh][1], masked, init)
        return tuple(carry)

    carry = tuple((jnp.full((1, tq), -jnp.inf, F32), jnp.zeros((1, tq), F32)) for _ in range(nh))
    if running_max:
        carry = blocks([qi], [True], carry, first=True)
        carry = lax.fori_loop(0, qi, lambda j, c: blocks([j], [False], c), carry)
    else:
        g = ATTN_BLOCKS_PER_TRIP
        tails = [functools.partial(blocks, [qi - r + t for t in range(r + 1)], [False] * r + [True],
                                   first=True) for r in range(g)]
        carry = lax.switch(qi % g, tails, carry)
        if k_ref.shape[2] // tq > g:
            carry = lax.fori_loop(
                0, qi // g,
                lambda p, c: blocks([g * p + t for t in range(g)], [False] * g, c), carry)

    for hh in range(nh):
        o_t = acc_ref[hh] / carry[hh][1]
        gate = sga_ref[0, :, hh * V_HEAD:(hh + 1) * V_HEAD].astype(F32)
        o_ref[0, :, hh * V_HEAD:(hh + 1) * V_HEAD] = (o_t.T * gate).astype(BF16)


def _attention(q_t, k, v_t, sga, tq, running_max):
    bsz, nh, _, seq = q_t.shape
    hb = ATTN_HEADS_PER_STEP
    return pl.pallas_call(
        functools.partial(_attn_kernel, tq=tq, running_max=running_max),
        grid=(bsz, nh // hb, seq // tq),
        in_specs=[
            pl.BlockSpec((1, hb, QK_PAD, tq), lambda b, h, i: (b, h, 0, i)),
            pl.BlockSpec((1, hb, seq, QK_PAD), lambda b, h, i: (b, h, 0, 0)),
            pl.BlockSpec((1, hb, V_HEAD, seq), lambda b, h, i: (b, h, 0, 0)),
            pl.BlockSpec((1, tq, hb * V_HEAD), lambda b, h, i: (b, i, h)),
        ],
        out_specs=pl.BlockSpec((1, tq, hb * V_HEAD), lambda b, h, i: (b, i, h)),
        out_shape=jax.ShapeDtypeStruct((bsz, seq, nh * V_HEAD), BF16),
        scratch_shapes=[pltpu.VMEM((hb, V_HEAD, tq), F32)],
        compiler_params=pltpu.CompilerParams(
            dimension_semantics=("arbitrary", "arbitrary", "arbitrary"),
            vmem_limit_bytes=VMEM_LIMIT),
        name="mla_attention_online" if running_max else "mla_attention",
    )(q_t, k, v_t, sga)


SUBLANES = 8
LANES = 128


def _rglru_kernel(xr_ref, sgr_ref, cw_ref, cb_ref, wa_ref, ba_ref, wx_ref, bx_ref, lam_ref,
                  o_ref, tail_ref, a_ref, b_ref, hc_ref, *, ts):
    @pl.when(pl.program_id(1) == 0)
    def _():
        tail_ref[...] = jnp.zeros_like(tail_ref)
        hc_ref[...] = jnp.zeros_like(hc_ref)

    blk = MXU_DIM
    n_delay = CONV_WIDTH - 1
    r_i = lax.broadcasted_iota(jnp.int32, (n_delay * blk, blk), 0)
    c_i = lax.broadcasted_iota(jnp.int32, (n_delay * blk, blk), 1)
    delay_mat = jnp.where((r_i % blk) - c_i == r_i // blk + 1, 1.0, 0.0).astype(BF16)
    row8 = lax.broadcasted_iota(jnp.int32, tail_ref.shape, 0)
    tail = tail_ref[...]
    conv = []
    for b0 in range(0, ts, blk):
        xb = xr_ref[0, b0:b0 + blk, :]
        x0 = xb.astype(F32)
        delayed = jnp.dot(delay_mat, xb, preferred_element_type=F32)
        acc = cb_ref[...] + cw_ref[n_delay:n_delay + 1, :] * x0
        for d in range(1, CONV_WIDTH):
            xd = delayed[(d - 1) * blk:d * blk]
            head = jnp.where(row8 < d, pltpu.roll(tail, d, 0), xd[0:SUBLANES])
            xd = jnp.concatenate([head, xd[SUBLANES:]], axis=0)
            acc = acc + cw_ref[n_delay - d:n_delay - d + 1, :] * xd
        tail = x0[blk - SUBLANES:blk]
        conv.append(acc)
    tail_ref[...] = tail
    xc = jnp.concatenate(conv, axis=0)

    xcb = xc.astype(BF16)
    n_grp = wa_ref.shape[0]

    def gate(w_ref, bias_ref):
        parts = [jnp.dot(xcb[:, g * MXU_DIM:(g + 1) * MXU_DIM], w_ref[g],
                         preferred_element_type=F32) for g in range(n_grp)]
        z = jnp.concatenate(parts, axis=1) + bias_ref[...]
        return 0.5 * jnp.tanh(0.5 * z) + 0.5

    r = gate(wa_ref, ba_ref)
    i = gate(wx_ref, bx_ref)
    nl = -lam_ref[...]
    log_a_unit = (-LRU_C) * (jnp.maximum(nl, 0.0) + jnp.log1p(jnp.exp(-jnp.abs(nl))))
    a_all = jnp.exp2(r * (log_a_unit * math.log2(math.e)))
    t = jnp.tanh(r * log_a_unit)
    u = -2.0 * t
    coef = jnp.where(u > 0.0, u * lax.rsqrt(u * (1.0 - t)), 0.0)
    b_all = coef * (i * xc)
    n_slab = a_ref.shape[0]
    for s in range(n_slab):
        a_ref[s] = a_all[:, s * LANES:(s + 1) * LANES]
        b_ref[s] = b_all[:, s * LANES:(s + 1) * LANES]

    row = lax.broadcasted_iota(jnp.int32, (SUBLANES, LANES), 0)
    block_rows = SUBLANES * SUBLANES

    def scan_block(blk, carry):
        base = pl.multiple_of(blk * block_rows, block_rows)
        new_carry = []
        for s in range(n_slab):
            h_loc, a_run = [], []
            for j in range(SUBLANES):
                rows_j = pl.ds(base + j, SUBLANES, stride=SUBLANES)
                a = a_ref[s, rows_j, :]
                b = b_ref[s, rows_j, :]
                h_loc.append(b if j == 0 else a * h_loc[-1] + b)
                a_run.append(a if j == 0 else a * a_run[-1])
            pa, pb = a_run[-1], h_loc[-1]
            for dist in (1, 2, 4):
                keep = row >= dist
                pb = jnp.where(keep, pa * pltpu.roll(pb, dist, 0) + pb, pb)
                pa = jnp.where(keep, pa * pltpu.roll(pa, dist, 0), pa)
            after = pa * carry[s] + pb
            h_in = jnp.where(row == 0, carry[s], pltpu.roll(after, 1, 0))
            for j in range(SUBLANES):
                rows_j = pl.ds(base + j, SUBLANES, stride=SUBLANES)
                b_ref[s, rows_j, :] = a_run[j] * h_in + h_loc[j]
            new_carry.append(jnp.broadcast_to(after[SUBLANES - 1:SUBLANES, :], after.shape))
        return tuple(new_carry)

    carry = lax.fori_loop(0, ts // block_rows, scan_block,
                          tuple(hc_ref[s] for s in range(n_slab)), unroll=2)
    for s in range(n_slab):
        hc_ref[s] = carry[s]
    h_all = jnp.concatenate([b_ref[s] for s in range(n_slab)], axis=1)
    o_ref[0] = (h_all * sgr_ref[0].astype(F32)).astype(BF16)


def _rglru(xr, sgr, cw, cb, wa, ba, wx, bx, lam, ts):
    bsz, seq, c = xr.shape
    consts = [cw, cb, wa, ba, wx, bx, lam]
    return pl.pallas_call(
        functools.partial(_rglru_kernel, ts=ts),
        grid=(bsz, seq // ts),
        in_specs=[
            pl.BlockSpec((1, ts, c), lambda b, i: (b, i, 0)),
            pl.BlockSpec((1, ts, c), lambda b, i: (b, i, 0)),
        ] + [_const_spec(a.shape) for a in consts],
        out_specs=pl.BlockSpec((1, ts, c), lambda b, i: (b, i, 0)),
        out_shape=jax.ShapeDtypeStruct((bsz, seq, c), BF16),
        scratch_shapes=[
            pltpu.VMEM((SUBLANES, c), F32),
            pltpu.VMEM((c // LANES, ts, LANES), F32),
            pltpu.VMEM((c // LANES, ts, LANES), F32),
            pltpu.VMEM((c // LANES, SUBLANES, LANES), F32),
        ],
        compiler_params=pltpu.CompilerParams(
            dimension_semantics=("arbitrary", "arbitrary"), vmem_limit_bytes=VMEM_LIMIT),
        name="rglru",
    )(xr, sgr, *consts)


def _out_proj_kernel(x_ref, yr_ref, ya_ref, w32_ref, mod_ref, o_ref, w_ref):
    @pl.when((pl.program_id(0) == 0) & (pl.program_id(1) == 0))
    def _():
        w_ref[...] = w32_ref[...].astype(BF16)

    d_rnn = yr_ref.shape[2]
    y = jnp.dot(yr_ref[0], w_ref[0:d_rnn, :], preferred_element_type=F32)
    y = y + jnp.dot(ya_ref[0], w_ref[d_rnn:, :], preferred_element_type=F32)
    o_ref[0] = x_ref[0] + mod_ref[0, 2:3, :] * y


def _out_proj(x, y_rnn, y_att, w_o, mod, tm):
    bsz, seq, d = x.shape
    return pl.pallas_call(
        _out_proj_kernel,
        grid=(bsz, seq // tm),
        in_specs=[
            pl.BlockSpec((1, tm, d), lambda b, i: (b, i, 0)),
            pl.BlockSpec((1, tm, y_rnn.shape[2]), lambda b, i: (b, i, 0)),
            pl.BlockSpec((1, tm, y_att.shape[2]), lambda b, i: (b, i, 0)),
            _const_spec(w_o.shape),
            pl.BlockSpec((1, 3, d), lambda b, i: (b, 0, 0)),
        ],
        out_specs=pl.BlockSpec((1, tm, d), lambda b, i: (b, i, 0)),
        out_shape=jax.ShapeDtypeStruct((bsz, seq, d), F32),
        scratch_shapes=[pltpu.VMEM(w_o.shape, BF16)],
        compiler_params=pltpu.CompilerParams(
            dimension_semantics=("arbitrary", "arbitrary"), vmem_limit_bytes=VMEM_LIMIT),
        name="out_proj",
    )(x, y_rnn, y_att, w_o, mod)


def _block_diag_groups(w):
    nb, n, _ = w.shape
    per = MXU_DIM // n
    w = w.reshape(nb // per, per, n, n)
    eye = jnp.eye(per, dtype=w.dtype)
    return jnp.einsum("gpij,pq->gpiqj", w, eye).reshape(nb // per, MXU_DIM, MXU_DIM)


def kernel(x, c, positions, w_ada, b_ada, w_in, conv_w, conv_b, w_rg_a, b_rg_a, w_rg_x, b_rg_x,
           lru_lambda, q_a_norm, w_uq, kv_a_norm, w_ukv, q_norm_nope, q_norm_rope, k_norm_nope,
           k_norm_rope, w_out):
    bsz, seq, d = x.shape
    depth = w_in.shape[0]
    d_rnn = conv_w.shape[2]
    q_lora = q_a_norm.shape[1]
    kv_lora = kv_a_norm.shape[1]
    tile = 512

    inv_freq = 1.0 / (ROPE_THETA ** (jnp.arange(0, QK_ROPE, 2, dtype=F32) / QK_ROPE))
    pos = positions.astype(F32)
    pos_row = pos.reshape(bsz, 1, seq)
    inv_col = inv_freq.reshape(-1, 1)
    c_pad = jnp.zeros((SUBLANES, d), F32).at[:bsz].set(c)

    for l in range(depth):
        mod = _adaln(c_pad, w_ada[l], b_ada[l].reshape(1, -1))[:bsz].reshape(bsz, 3, d)

        ukv = w_ukv[l].reshape(kv_lora, MLA_HEADS, QK_NOPE + V_HEAD)
        w = dict(
            d_rnn=d_rnn, wt=jnp.swapaxes(w_in[l], 0, 1).astype(BF16),
            qan=q_a_norm[l].reshape(1, -1), kvan=kv_a_norm[l].reshape(1, -1),
            wuqt=w_uq[l].T.astype(BF16),
            wuk=ukv[:, :, :QK_NOPE].reshape(kv_lora, -1).astype(BF16),
            wuvt=ukv[:, :, QK_NOPE:].reshape(kv_lora, -1).T.astype(BF16),
            gqn=q_norm_nope[l].reshape(-1, 1), gqr=q_norm_rope[l].reshape(-1, 1),
            gkn=k_norm_nope[l].reshape(1, -1), gkr=k_norm_rope[l].reshape(-1, 1),
        )
        xr, sgr, sga, q_t, k, v_t, bound = _in_proj(x, mod, pos_row, inv_col, w, tile)

        y_att = lax.cond(
            jnp.max(bound) < MAX_SAFE_SCORE_BOUND,
            functools.partial(_attention, tq=tile, running_max=False),
            functools.partial(_attention, tq=tile, running_max=True),
            q_t, k, v_t, sga)
        y_rnn = _rglru(
            xr, sgr, conv_w[l], conv_b[l].reshape(1, -1),
            _block_diag_groups(w_rg_a[l]).astype(BF16), b_rg_a[l].reshape(1, -1),
            _block_diag_groups(w_rg_x[l]).astype(BF16), b_rg_x[l].reshape(1, -1),
            lru_lambda[l].reshape(1, -1), tile)

        x = _out_proj(x, y_rnn, y_att, w_out[l], mod, tile)
    return x
```

```python
import functools
import math

import jax
import jax.numpy as jnp
from jax import lax
from jax.experimental import pallas as pl
from jax.experimental.pallas import tpu as pltpu

RNN_BLOCKS = 16
CONV_WIDTH = 4
LRU_C = 8.0
MLA_HEADS = 8
QK_NOPE = 128
QK_ROPE = 64
V_HEAD = 128
ROPE_THETA = 10000.0
EPS = 1e-6

QK_PAD = 256
MXU_DIM = 256
VMEM_LIMIT = 56 * 1024 * 1024

F32 = jnp.float32
BF16 = jnp.bfloat16

NT_DIMS = (((1,), (1,)), ((), ()))
TN_DIMS = (((0,), (0,)), ((), ()))


def _const_spec(shape):
    nd = len(shape)
    return pl.BlockSpec(shape, lambda *_: (0,) * nd, pipeline_mode=pl.Buffered(1))


def _adaln_kernel(c_ref, w_ref, b_ref, o_ref):
    c = c_ref[...]
    c_act = (c * jax.nn.sigmoid(c)).astype(BF16)
    o_ref[...] = jnp.dot(c_act, w_ref[...].astype(BF16), preferred_element_type=F32) + b_ref[...]


def _adaln(c_pad, w_ada, b_ada):
    rows, d = c_pad.shape
    n = w_ada.shape[1]
    tn = 768
    return pl.pallas_call(
        _adaln_kernel,
        grid=(n // tn,),
        in_specs=[
            pl.BlockSpec((rows, d), lambda j: (0, 0)),
            pl.BlockSpec((d, tn), lambda j: (0, j)),
            pl.BlockSpec((1, tn), lambda j: (0, j)),
        ],
        out_specs=pl.BlockSpec((rows, tn), lambda j: (0, j)),
        out_shape=jax.ShapeDtypeStruct((rows, n), F32),
        compiler_params=pltpu.CompilerParams(
            dimension_semantics=("arbitrary",), vmem_limit_bytes=VMEM_LIMIT),
        name="adaln_mod",
    )(c_pad, w_ada, b_ada)


IN_PROJ_ROWS = 256


def _rms(v, axis):
    return v * lax.rsqrt(jnp.mean(v * v, axis=axis, keepdims=True) + EPS)


def _silu(v):
    half = 0.5 * v
    return half * jnp.tanh(half) + half


def _in_proj_kernel(x_ref, mod_ref, posr_ref, invc_ref, wt_ref,
                    qan_ref, kvan_ref, wuqt_ref, wuk_ref, wuvt_ref,
                    gqn_ref, gqr_ref, gkn_ref, gkr_ref,
                    xr_ref, sgr_ref, sga_ref, qt_ref, k_ref, vt_ref, bound_ref, *, q_scale):
    d_rnn, d_att = xr_ref.shape[2], sga_ref.shape[1]
    q_lora, kv_lora = qan_ref.shape[1], kvan_ref.shape[1]
    qc0, kvc0 = 2 * d_rnn, 2 * d_rnn + q_lora
    kr0 = kvc0 + kv_lora
    ga0 = kr0 + QK_ROPE
    scale = mod_ref[0, 1:2, :]
    shift = mod_ref[0, 0:1, :]
    hd = QK_NOPE + QK_ROPE
    half = QK_ROPE // 2
    gkn, gkr = gkn_ref[...], gkr_ref[...]
    k_bound = jnp.sqrt(QK_NOPE * jnp.max(gkn * gkn, axis=1, keepdims=True)
                       + QK_ROPE * jnp.max(gkr * gkr, axis=0, keepdims=True))
    pad_row = lax.broadcasted_iota(jnp.int32, (QK_PAD - hd, IN_PROJ_ROWS), 0)

    chunks = [slice(r0, r0 + IN_PROJ_ROWS) for r0 in range(0, x_ref.shape[1], IN_PROJ_ROWS)]
    hs = [(_rms(x_ref[0, rs, :], -1) * (1.0 + scale) + shift).astype(BF16) for rs in chunks]
    for rs, h in zip(chunks, hs):

        def proj(lo, hi):
            return lax.dot_general(h, wt_ref[lo:hi, :], NT_DIMS, preferred_element_type=F32)

        qcn = (_rms(proj(qc0, kvc0), -1) * qan_ref[...]).astype(BF16)
        kvcn = (_rms(proj(kvc0, kr0), -1) * kvan_ref[...]).astype(BF16)
        kr_t = proj(kr0, kr0 + 128).T[:QK_ROPE]
        q_t = lax.dot_general(wuqt_ref[...], qcn, NT_DIMS, preferred_element_type=F32)
        v_t = lax.dot_general(wuvt_ref[...], kvcn, NT_DIMS, preferred_element_type=F32)
        kn_all = jnp.dot(kvcn, wuk_ref[...], preferred_element_type=F32)

        ang_t = invc_ref[...] * posr_ref[0, :, rs]
        cos_t, sin_t = jnp.cos(ang_t), jnp.sin(ang_t)

        def rope_t(v):
            v1, v2 = v[:half], v[half:]
            return v1 * cos_t - v2 * sin_t, v1 * sin_t + v2 * cos_t

        bound_max = None
        for hh in range(MLA_HEADS):
            qn = q_t[hh * hd:hh * hd + QK_NOPE]
            qn = _rms(qn, 0) * (gqn_ref[...] * q_scale)
            qr = q_t[hh * hd + QK_NOPE:(hh + 1) * hd]
            r1, r2 = rope_t(_rms(qr, 0) * (gqr_ref[...] * q_scale))
            q_sq = (jnp.sum(qn * qn, axis=0, keepdims=True)
                    + jnp.sum(r1 * r1 + r2 * r2, axis=0, keepdims=True))
            bound = jnp.sqrt(q_sq) * k_bound
            bound_max = bound if bound_max is None else jnp.maximum(bound_max, bound)
            qt_ref[0, hh, 0:QK_NOPE, rs] = qn.astype(BF16)
            qt_ref[0, hh, QK_NOPE:QK_NOPE + half, rs] = r1.astype(BF16)
            qt_ref[0, hh, QK_NOPE + half:hd, rs] = r2.astype(BF16)
            qt_ref[0, hh, hd:QK_PAD, rs] = jnp.where(pad_row == 0, -bound, 0.0).astype(BF16)
        bound_ref[0, :, rs] = bound_max

        for hh in range(MLA_HEADS):
            vt_ref[0, hh, :, rs] = v_t[hh * V_HEAD:(hh + 1) * V_HEAD].astype(BF16)
        k1, k2 = rope_t(_rms(kr_t, 0) * gkr)
        kpe_t = jnp.concatenate([k1, k2, jnp.where(pad_row == 0, 1.0, 0.0)], axis=0)
        kpe = kpe_t.T.astype(BF16)
        for hh in range(MLA_HEADS):
            kn = kn_all[:, hh * QK_NOPE:(hh + 1) * QK_NOPE]
            k_ref[0, hh, rs, 0:QK_NOPE] = (_rms(kn, -1) * gkn).astype(BF16)
            k_ref[0, hh, rs, QK_NOPE:QK_PAD] = kpe

        xr_ref[0, rs, :] = proj(0, d_rnn).astype(BF16)
        gr = proj(d_rnn, 2 * d_rnn)
        sgr_ref[0, rs, :] = _silu(gr).astype(BF16)
        for g0 in range(0, d_att, d_att // 2):
            ga_t = lax.dot_general(wt_ref[ga0 + g0:ga0 + g0 + d_att // 2, :], h, NT_DIMS,
                                   preferred_element_type=F32)
            sga_ref[0, g0:g0 + d_att // 2, rs] = _silu(ga_t).astype(BF16)


def _in_proj(x, mod, pos_row, inv_col, w, tm):
    bsz, seq, d = x.shape
    hd = QK_NOPE + QK_ROPE
    q_scale = (hd ** -0.5) * math.log2(math.e)
    consts = [w["wt"],
              w["qan"], w["kvan"], w["wuqt"], w["wuk"], w["wuvt"],
              w["gqn"], w["gqr"], w["gkn"], w["gkr"]]
    d_rnn, d_att = w["d_rnn"], MLA_HEADS * V_HEAD
    return pl.pallas_call(
        functools.partial(_in_proj_kernel, q_scale=q_scale),
        grid=(bsz, seq // tm),
        in_specs=[
            pl.BlockSpec((1, tm, d), lambda b, i: (b, i, 0)),
            pl.BlockSpec((1, 3, d), lambda b, i: (b, 0, 0)),
            pl.BlockSpec((1, 1, tm), lambda b, i: (b, 0, i)),
            _const_spec(inv_col.shape),
        ] + [_const_spec(a.shape) for a in consts],
        out_specs=[
            pl.BlockSpec((1, tm, d_rnn), lambda b, i: (b, i, 0)),
            pl.BlockSpec((1, tm, d_rnn), lambda b, i: (b, i, 0)),
            pl.BlockSpec((1, d_att, tm), lambda b, i: (b, 0, i)),
            pl.BlockSpec((1, MLA_HEADS, QK_PAD, tm), lambda b, i: (b, 0, 0, i)),
            pl.BlockSpec((1, MLA_HEADS, tm, QK_PAD), lambda b, i: (b, 0, i, 0)),
            pl.BlockSpec((1, MLA_HEADS, V_HEAD, tm), lambda b, i: (b, 0, 0, i)),
            pl.BlockSpec((1, 1, tm), lambda b, i: (b, 0, i)),
        ],
        out_shape=[
            jax.ShapeDtypeStruct((bsz, seq, d_rnn), BF16),
            jax.ShapeDtypeStruct((bsz, seq, d_rnn), BF16),
            jax.ShapeDtypeStruct((bsz, d_att, seq), BF16),
            jax.ShapeDtypeStruct((bsz, MLA_HEADS, QK_PAD, seq), BF16),
            jax.ShapeDtypeStruct((bsz, MLA_HEADS, seq, QK_PAD), BF16),
            jax.ShapeDtypeStruct((bsz, MLA_HEADS, V_HEAD, seq), BF16),
            jax.ShapeDtypeStruct((bsz, 1, seq), F32),
        ],
        compiler_params=pltpu.CompilerParams(
            dimension_semantics=("arbitrary", "arbitrary"), vmem_limit_bytes=VMEM_LIMIT),
        name="in_proj",
    )(x, mod, pos_row, inv_col, *consts)


ATTN_HEADS_PER_STEP = 4
ATTN_KEY_SUB = 256
ATTN_BLOCKS_PER_TRIP = 4
MAX_SAFE_SCORE_BOUND = 56.0


def _attn_kernel(qt_ref, k_ref, vt_ref, sga_ref, o_ref, acc_ref, *, tq, running_max):
    qi = pl.program_id(2)
    nh = qt_ref.shape[1]

    sub = tq if running_max else ATTN_KEY_SUB

    def first_query(unit, masked):
        return unit[2] * sub if masked[unit[0]] else 0

    def scores(unit, js, masked):
        bi, hh, c = unit
        key0 = pl.multiple_of(js[bi] * tq + c * sub, sub)
        k_sub = k_ref[0, hh, pl.ds(key0, sub), :]
        q_t = qt_ref[0, hh, :, first_query(unit, masked):]
        return jnp.dot(k_sub, q_t, preferred_element_type=F32)

    def accumulate(unit, js, s_t, m, l, masked, init):
        bi, hh, c = unit
        q0 = first_query(unit, masked)
        assert not (init and q0), "the unit that initialises acc must cover every query"
        if masked[bi]:
            key = lax.broadcasted_iota(jnp.int32, s_t.shape, 0) + c * sub
            qry = lax.broadcasted_iota(jnp.int32, s_t.shape, 1) + q0
            s_t = jnp.where(key <= qry, s_t, -jnp.inf)
        key0 = pl.multiple_of(js[bi] * tq + c * sub, sub)
        v_sub = vt_ref[0, hh, :, pl.ds(key0, sub)]
        if running_max:
            m_new = jnp.maximum(m, jnp.max(s_t, axis=0, keepdims=True))
            alpha = jnp.exp2(m - m_new)
            p_t = jnp.exp2(s_t - m_new)
            l_new = alpha * l + jnp.sum(p_t, axis=0, keepdims=True)
            pv = jnp.dot(v_sub, p_t.astype(BF16), preferred_element_type=F32)
            acc_ref[hh] = pv if init else alpha * acc_ref[hh] + pv
        else:
            m_new = m
            p_t = jnp.exp2(s_t)
            l_part = l[:, q0:] + jnp.sum(p_t, axis=0, keepdims=True)
            l_new = l_part if q0 == 0 else jnp.concatenate([l[:, :q0], l_part], axis=1)
            pv = jnp.dot(v_sub, p_t.astype(BF16), preferred_element_type=F32)
            if init:
                acc_ref[hh] = pv
            else:
                acc_ref[hh, :, q0:] += pv
        return m_new, l_new

    def blocks(js, masked, carry, first=False):
        units = [(bi, hh, c) for bi in range(len(js)) for hh in range(nh)
                 for c in range(tq // sub)]
        carry = list(carry)
        s_next = scores(units[0], js, masked)
        for idx, unit in enumerate(units):
            s_cur = s_next
            if idx + 1 < len(units):
                s_next = scores(units[idx + 1], js, masked)
            hh = unit[1]
            init = first and unit[0] == 0 and unit[2] == 0
            carry[hh] = accumulate(unit, js, s_cur, carry[hh][0], carry[hh][1], masked, init)
        return tuple(carry)

    carry = tuple((jnp.full((1, tq), -jnp.inf, F32), jnp.zeros((1, tq), F32)) for _ in range(nh))
    if running_max:
        carry = blocks([qi], [True], carry, first=True)
        carry = lax.fori_loop(0, qi, lambda j, c: blocks([j], [False], c), carry)
    else:
        g = ATTN_BLOCKS_PER_TRIP
        tails = [functools.partial(blocks, [qi - r + t for t in range(r + 1)], [False] * r + [True],
                                   first=True) for r in range(g)]
        carry = lax.switch(qi % g, tails, carry)
        if k_ref.shape[2] // tq > g:
            carry = lax.fori_loop(
                0, qi // g,
                lambda p, c: blocks([g * p + t for t in range(g)], [False] * g, c), carry)

    for hh in range(nh):
        o_t = acc_ref[hh] / carry[hh][1]
        rows = slice(hh * V_HEAD, (hh + 1) * V_HEAD)
        o_ref[0, rows, :] = (o_t * sga_ref[0, rows, :].astype(F32)).astype(BF16)


def _attention(q_t, k, v_t, sga, tq, running_max):
    bsz, nh, _, seq = q_t.shape
    hb = ATTN_HEADS_PER_STEP
    return pl.pallas_call(
        functools.partial(_attn_kernel, tq=tq, running_max=running_max),
        grid=(bsz, nh // hb, seq // tq),
        in_specs=[
            pl.BlockSpec((1, hb, QK_PAD, tq), lambda b, h, i: (b, h, 0, i)),
            pl.BlockSpec((1, hb, seq, QK_PAD), lambda b, h, i: (b, h, 0, 0)),
            pl.BlockSpec((1, hb, V_HEAD, seq), lambda b, h, i: (b, h, 0, 0)),
            pl.BlockSpec((1, hb * V_HEAD, tq), lambda b, h, i: (b, h, i)),
        ],
        out_specs=pl.BlockSpec((1, hb * V_HEAD, tq), lambda b, h, i: (b, h, i)),
        out_shape=jax.ShapeDtypeStruct((bsz, nh * V_HEAD, seq), BF16),
        scratch_shapes=[pltpu.VMEM((hb, V_HEAD, tq), F32)],
        compiler_params=pltpu.CompilerParams(
            dimension_semantics=("arbitrary", "arbitrary", "arbitrary"),
            vmem_limit_bytes=VMEM_LIMIT),
        name="mla_attention_online" if running_max else "mla_attention",
    )(q_t, k, v_t, sga)


SUBLANES = 8
LANES = 128


def _rglru_kernel(xr_ref, sgr_ref, cw_ref, cb_ref, wa_ref, ba_ref, wx_ref, bx_ref, lam_ref,
                  o_ref, tail_ref, a_ref, b_ref, hc_ref, *, ts):
    @pl.when(pl.program_id(1) == 0)
    def _():
        tail_ref[...] = jnp.zeros_like(tail_ref)
        hc_ref[...] = jnp.zeros_like(hc_ref)

    blk = MXU_DIM
    n_delay = CONV_WIDTH - 1
    r_i = lax.broadcasted_iota(jnp.int32, (n_delay * blk, blk), 0)
    c_i = lax.broadcasted_iota(jnp.int32, (n_delay * blk, blk), 1)
    delay_mat = jnp.where((r_i % blk) - c_i == r_i // blk + 1, 1.0, 0.0).astype(BF16)
    row8 = lax.broadcasted_iota(jnp.int32, tail_ref.shape, 0)
    tail = tail_ref[...]
    conv = []
    for b0 in range(0, ts, blk):
        xb = xr_ref[0, b0:b0 + blk, :]
        x0 = xb.astype(F32)
        delayed = jnp.dot(delay_mat, xb, preferred_element_type=F32)
        acc = cb_ref[...] + cw_ref[n_delay:n_delay + 1, :] * x0
        for d in range(1, CONV_WIDTH):
            xd = delayed[(d - 1) * blk:d * blk]
            head = jnp.where(row8 < d, pltpu.roll(tail, d, 0), xd[0:SUBLANES])
            xd = jnp.concatenate([head, xd[SUBLANES:]], axis=0)
            acc = acc + cw_ref[n_delay - d:n_delay - d + 1, :] * xd
        tail = x0[blk - SUBLANES:blk]
        conv.append(acc)
    tail_ref[...] = tail
    xc = jnp.concatenate(conv, axis=0)

    xcb = xc.astype(BF16)
    n_grp = wa_ref.shape[0]

    def gate(w_ref, bias_ref):
        parts = [jnp.dot(xcb[:, g * MXU_DIM:(g + 1) * MXU_DIM], w_ref[g],
                         preferred_element_type=F32) for g in range(n_grp)]
        z = jnp.concatenate(parts, axis=1) + bias_ref[...]
        return 0.5 * jnp.tanh(0.5 * z) + 0.5

    r = gate(wa_ref, ba_ref)
    i = gate(wx_ref, bx_ref)
    nl = -lam_ref[...]
    log_a_unit = (-LRU_C) * (jnp.maximum(nl, 0.0) + jnp.log1p(jnp.exp(-jnp.abs(nl))))
    a_all = jnp.exp2(r * (log_a_unit * math.log2(math.e)))
    t = jnp.tanh(r * log_a_unit)
    u = -2.0 * t
    coef = jnp.where(u > 0.0, u * lax.rsqrt(u * (1.0 - t)), 0.0)
    b_all = coef * (i * xc)
    n_slab = a_ref.shape[0]
    for s in range(n_slab):
        a_ref[s] = a_all[:, s * LANES:(s + 1) * LANES]
        b_ref[s] = b_all[:, s * LANES:(s + 1) * LANES]

    row = lax.broadcasted_iota(jnp.int32, (SUBLANES, LANES), 0)
    block_rows = SUBLANES * SUBLANES

    def scan_block(blk, carry):
        base = pl.multiple_of(blk * block_rows, block_rows)
        new_carry = []
        for s in range(n_slab):
            h_loc, a_run = [], []
            for j in range(SUBLANES):
                rows_j = pl.ds(base + j, SUBLANES, stride=SUBLANES)
                a = a_ref[s, rows_j, :]
                b = b_ref[s, rows_j, :]
                h_loc.append(b if j == 0 else a * h_loc[-1] + b)
                a_run.append(a if j == 0 else a * a_run[-1])
            pa, pb = a_run[-1], h_loc[-1]
            for dist in (1, 2, 4):
                keep = row >= dist
                pb = jnp.where(keep, pa * pltpu.roll(pb, dist, 0) + pb, pb)
                pa = jnp.where(keep, pa * pltpu.roll(pa, dist, 0), pa)
            after = pa * carry[s] + pb
            h_in = jnp.where(row == 0, carry[s], pltpu.roll(after, 1, 0))
            for j in range(SUBLANES):
                rows_j = pl.ds(base + j, SUBLANES, stride=SUBLANES)
                b_ref[s, rows_j, :] = a_run[j] * h_in + h_loc[j]
            new_carry.append(jnp.broadcast_to(after[SUBLANES - 1:SUBLANES, :], after.shape))
        return tuple(new_carry)

    carry = lax.fori_loop(0, ts // block_rows, scan_block,
                          tuple(hc_ref[s] for s in range(n_slab)), unroll=2)
    for s in range(n_slab):
        hc_ref[s] = carry[s]
    h_all = jnp.concatenate([b_ref[s] for s in range(n_slab)], axis=1)
    o_ref[0] = (h_all * sgr_ref[0].astype(F32)).astype(BF16)


def _rglru(xr, sgr, cw, cb, wa, ba, wx, bx, lam, ts):
    bsz, seq, c = xr.shape
    consts = [cw, cb, wa, ba, wx, bx, lam]
    return pl.pallas_call(
        functools.partial(_rglru_kernel, ts=ts),
        grid=(bsz, seq // ts),
        in_specs=[
            pl.BlockSpec((1, ts, c), lambda b, i: (b, i, 0)),
            pl.BlockSpec((1, ts, c), lambda b, i: (b, i, 0)),
        ] + [_const_spec(a.shape) for a in consts],
        out_specs=pl.BlockSpec((1, ts, c), lambda b, i: (b, i, 0)),
        out_shape=jax.ShapeDtypeStruct((bsz, seq, c), BF16),
        scratch_shapes=[
            pltpu.VMEM((SUBLANES, c), F32),
            pltpu.VMEM((c // LANES, ts, LANES), F32),
            pltpu.VMEM((c // LANES, ts, LANES), F32),
            pltpu.VMEM((c // LANES, SUBLANES, LANES), F32),
        ],
        compiler_params=pltpu.CompilerParams(
            dimension_semantics=("arbitrary", "arbitrary"), vmem_limit_bytes=VMEM_LIMIT),
        name="rglru",
    )(xr, sgr, *consts)


def _out_proj_kernel(x_ref, yr_ref, ya_ref, w32_ref, mod_ref, o_ref, w_ref):
    @pl.when((pl.program_id(0) == 0) & (pl.program_id(1) == 0))
    def _():
        w_ref[...] = w32_ref[...].astype(BF16)

    d_rnn = yr_ref.shape[2]
    y = jnp.dot(yr_ref[0], w_ref[0:d_rnn, :], preferred_element_type=F32)
    y = y + lax.dot_general(ya_ref[0], w_ref[d_rnn:, :], TN_DIMS,
                            preferred_element_type=F32)
    o_ref[0] = x_ref[0] + mod_ref[0, 2:3, :] * y


def _out_proj(x, y_rnn, y_att, w_o, mod, tm):
    bsz, seq, d = x.shape
    return pl.pallas_call(
        _out_proj_kernel,
        grid=(bsz, seq // tm),
        in_specs=[
            pl.BlockSpec((1, tm, d), lambda b, i: (b, i, 0)),
            pl.BlockSpec((1, tm, y_rnn.shape[2]), lambda b, i: (b, i, 0)),
            pl.BlockSpec((1, y_att.shape[1], tm), lambda b, i: (b, 0, i)),
            _const_spec(w_o.shape),
            pl.BlockSpec((1, 3, d), lambda b, i: (b, 0, 0)),
        ],
        out_specs=pl.BlockSpec((1, tm, d), lambda b, i: (b, i, 0)),
        out_shape=jax.ShapeDtypeStruct((bsz, seq, d), F32),
        scratch_shapes=[pltpu.VMEM(w_o.shape, BF16)],
        compiler_params=pltpu.CompilerParams(
            dimension_semantics=("arbitrary", "arbitrary"), vmem_limit_bytes=VMEM_LIMIT),
        name="out_proj",
    )(x, y_rnn, y_att, w_o, mod)


def _block_diag_groups(w):
    nb, n, _ = w.shape
    per = MXU_DIM // n
    w = w.reshape(nb // per, per, n, n)
    eye = jnp.eye(per, dtype=w.dtype)
    return jnp.einsum("gpij,pq->gpiqj", w, eye).reshape(nb // per, MXU_DIM, MXU_DIM)


def kernel(x, c, positions, w_ada, b_ada, w_in, conv_w, conv_b, w_rg_a, b_rg_a, w_rg_x, b_rg_x,
           lru_lambda, q_a_norm, w_uq, kv_a_norm, w_ukv, q_norm_nope, q_norm_rope, k_norm_nope,
           k_norm_rope, w_out):
    bsz, seq, d = x.shape
    depth = w_in.shape[0]
    d_rnn = conv_w.shape[2]
    q_lora = q_a_norm.shape[1]
    kv_lora = kv_a_norm.shape[1]
    tile = 512

    inv_freq = 1.0 / (ROPE_THETA ** (jnp.arange(0, QK_ROPE, 2, dtype=F32) / QK_ROPE))
    pos = positions.astype(F32)
    pos_row = pos.reshape(bsz, 1, seq)
    inv_col = inv_freq.reshape(-1, 1)
    c_pad = jnp.zeros((SUBLANES, d), F32).at[:bsz].set(c)

    for l in range(depth):
        mod = _adaln(c_pad, w_ada[l], b_ada[l].reshape(1, -1))[:bsz].reshape(bsz, 3, d)

        ukv = w_ukv[l].reshape(kv_lora, MLA_HEADS, QK_NOPE + V_HEAD)
        w = dict(
            d_rnn=d_rnn, wt=jnp.swapaxes(w_in[l], 0, 1).astype(BF16),
            qan=q_a_norm[l].reshape(1, -1), kvan=kv_a_norm[l].reshape(1, -1),
            wuqt=w_uq[l].T.astype(BF16),
            wuk=ukv[:, :, :QK_NOPE].reshape(kv_lora, -1).astype(BF16),
            wuvt=ukv[:, :, QK_NOPE:].reshape(kv_lora, -1).T.astype(BF16),
            gqn=q_norm_nope[l].reshape(-1, 1), gqr=q_norm_rope[l].reshape(-1, 1),
            gkn=k_norm_nope[l].reshape(1, -1), gkr=k_norm_rope[l].reshape(-1, 1),
        )
        xr, sgr, sga, q_t, k, v_t, bound = _in_proj(x, mod, pos_row, inv_col, w, tile)

        y_att = lax.cond(
            jnp.max(bound) < MAX_SAFE_SCORE_BOUND,
            functools.partial(_attention, tq=tile, running_max=False),
            functools.partial(_attention, tq=tile, running_max=True),
            q_t, k, v_t, sga)
        y_rnn = _rglru(
            xr, sgr, conv_w[l], conv_b[l].reshape(1, -1),
            _block_diag_groups(w_rg_a[l]).astype(BF16), b_rg_a[l].reshape(1, -1),
            _block_diag_groups(w_rg_x[l]).astype(BF16), b_rg_x[l].reshape(1, -1),
            lru_lambda[l].reshape(1, -1), tile)

        x = _out_proj(x, y_rnn, y_att, w_out[l], mod, tile)
    return x
```

```python
import functools
import math

import jax
import jax.numpy as jnp
from jax import lax
from jax.experimental import pallas as pl
from jax.experimental.pallas import tpu as pltpu

RNN_BLOCKS = 16
CONV_WIDTH = 4
LRU_C = 8.0
MLA_HEADS = 8
QK_NOPE = 128
QK_ROPE = 64
V_HEAD = 128
ROPE_THETA = 10000.0
EPS = 1e-6

QK_PAD = 256
MXU_DIM = 256
VMEM_LIMIT = 56 * 1024 * 1024

F32 = jnp.float32
BF16 = jnp.bfloat16

NT_DIMS = (((1,), (1,)), ((), ()))
TN_DIMS = (((0,), (0,)), ((), ()))


def _const_spec(shape):
    nd = len(shape)
    return pl.BlockSpec(shape, lambda *_: (0,) * nd, pipeline_mode=pl.Buffered(1))


def _adaln_kernel(c_ref, w_ref, b_ref, o_ref):
    c = c_ref[...]
    c_act = (c * jax.nn.sigmoid(c)).astype(BF16)
    o_ref[...] = jnp.dot(c_act, w_ref[...].astype(BF16), preferred_element_type=F32) + b_ref[...]


def _adaln(c_pad, w_ada, b_ada):
    rows, d = c_pad.shape
    n = w_ada.shape[1]
    tn = 768
    return pl.pallas_call(
        _adaln_kernel,
        grid=(n // tn,),
        in_specs=[
            pl.BlockSpec((rows, d), lambda j: (0, 0)),
            pl.BlockSpec((d, tn), lambda j: (0, j)),
            pl.BlockSpec((1, tn), lambda j: (0, j)),
        ],
        out_specs=pl.BlockSpec((rows, tn), lambda j: (0, j)),
        out_shape=jax.ShapeDtypeStruct((rows, n), F32),
        compiler_params=pltpu.CompilerParams(
            dimension_semantics=("arbitrary",), vmem_limit_bytes=VMEM_LIMIT),
        name="adaln_mod",
    )(c_pad, w_ada, b_ada)


IN_PROJ_ROWS = 256


def _rms(v, axis):
    return v * lax.rsqrt(jnp.mean(v * v, axis=axis, keepdims=True) + EPS)


def _silu(v):
    half = 0.5 * v
    return half * jnp.tanh(half) + half


def _in_proj_kernel(x_ref, mod_ref, posr_ref, invc_ref, wt_ref,
                    qan_ref, kvan_ref, wuqt_ref, wuk_ref, wuvt_ref,
                    gqn_ref, gqr_ref, gkn_ref, gkr_ref,
                    xr_ref, sgr_ref, sga_ref, qt_ref, k_ref, vt_ref, bound_ref, *, q_scale):
    d_rnn, d_att = xr_ref.shape[2], sga_ref.shape[1]
    q_lora, kv_lora = qan_ref.shape[1], kvan_ref.shape[1]
    qc0, kvc0 = 2 * d_rnn, 2 * d_rnn + q_lora
    kr0 = kvc0 + kv_lora
    ga0 = kr0 + QK_ROPE
    scale = mod_ref[0, 1:2, :]
    shift = mod_ref[0, 0:1, :]
    hd = QK_NOPE + QK_ROPE
    half = QK_ROPE // 2
    gkn, gkr = gkn_ref[...], gkr_ref[...]
    k_bound = jnp.sqrt(QK_NOPE * jnp.max(gkn * gkn, axis=1, keepdims=True)
                       + QK_ROPE * jnp.max(gkr * gkr, axis=0, keepdims=True))
    pad_row = lax.broadcasted_iota(jnp.int32, (QK_PAD - hd, IN_PROJ_ROWS), 0)

    chunks = [slice(r0, r0 + IN_PROJ_ROWS) for r0 in range(0, x_ref.shape[1], IN_PROJ_ROWS)]
    hs = [(_rms(x_ref[0, rs, :], -1) * (1.0 + scale) + shift).astype(BF16) for rs in chunks]
    for rs, h in zip(chunks, hs):

        def proj(lo, hi):
            return lax.dot_general(h, wt_ref[lo:hi, :], NT_DIMS, preferred_element_type=F32)

        latent = proj(qc0, kr0)
        qcn = (_rms(latent[:, :q_lora], -1) * qan_ref[...]).astype(BF16)
        kvcn = (_rms(latent[:, q_lora:], -1) * kvan_ref[...]).astype(BF16)
        kg_t = lax.dot_general(wt_ref[kr0:ga0 + d_att // 2, :], h, NT_DIMS,
                               preferred_element_type=F32)
        kr_t = kg_t[:QK_ROPE]
        sga_ref[0, 0:d_att // 2, rs] = _silu(kg_t[QK_ROPE:]).astype(BF16)
        q_t = lax.dot_general(wuqt_ref[...], qcn, NT_DIMS, preferred_element_type=F32)
        v_t = lax.dot_general(wuvt_ref[...], kvcn, NT_DIMS, preferred_element_type=F32)
        kn_all = jnp.dot(kvcn, wuk_ref[...], preferred_element_type=F32)

        ang_t = invc_ref[...] * posr_ref[0, :, rs]
        cos_t, sin_t = jnp.cos(ang_t), jnp.sin(ang_t)

        def rope_t(v):
            v1, v2 = v[:half], v[half:]
            return v1 * cos_t - v2 * sin_t, v1 * sin_t + v2 * cos_t

        bound_max = None
        for hh in range(MLA_HEADS):
            qn = q_t[hh * hd:hh * hd + QK_NOPE]
            qn = _rms(qn, 0) * (gqn_ref[...] * q_scale)
            qr = q_t[hh * hd + QK_NOPE:(hh + 1) * hd]
            r1, r2 = rope_t(_rms(qr, 0) * (gqr_ref[...] * q_scale))
            q_sq = (jnp.sum(qn * qn, axis=0, keepdims=True)
                    + jnp.sum(r1 * r1 + r2 * r2, axis=0, keepdims=True))
            bound = jnp.sqrt(q_sq) * k_bound
            bound_max = bound if bound_max is None else jnp.maximum(bound_max, bound)
            qt_ref[0, hh, 0:QK_NOPE, rs] = qn.astype(BF16)
            qt_ref[0, hh, QK_NOPE:QK_NOPE + half, rs] = r1.astype(BF16)
            qt_ref[0, hh, QK_NOPE + half:hd, rs] = r2.astype(BF16)
            qt_ref[0, hh, hd:QK_PAD, rs] = jnp.where(pad_row == 0, -bound, 0.0).astype(BF16)
        bound_ref[0, :, rs] = bound_max

        for hh in range(MLA_HEADS):
            vt_ref[0, hh, :, rs] = v_t[hh * V_HEAD:(hh + 1) * V_HEAD].astype(BF16)
        k1, k2 = rope_t(_rms(kr_t, 0) * gkr)
        kpe_t = jnp.concatenate([k1, k2, jnp.where(pad_row == 0, 1.0, 0.0)], axis=0)
        kpe = kpe_t.T.astype(BF16)
        for hh in range(MLA_HEADS):
            kn = kn_all[:, hh * QK_NOPE:(hh + 1) * QK_NOPE]
            k_ref[0, hh, rs, 0:QK_NOPE] = (_rms(kn, -1) * gkn).astype(BF16)
            k_ref[0, hh, rs, QK_NOPE:QK_PAD] = kpe

        xr_ref[0, rs, :] = proj(0, d_rnn).astype(BF16)
        gr = proj(d_rnn, 2 * d_rnn)
        sgr_ref[0, rs, :] = _silu(gr).astype(BF16)
        ga_t = lax.dot_general(wt_ref[ga0 + d_att // 2:ga0 + d_att, :], h, NT_DIMS,
                               preferred_element_type=F32)
        sga_ref[0, d_att // 2:d_att, rs] = _silu(ga_t).astype(BF16)


def _in_proj(x, mod, pos_row, inv_col, w, tm):
    bsz, seq, d = x.shape
    hd = QK_NOPE + QK_ROPE
    q_scale = (hd ** -0.5) * math.log2(math.e)
    consts = [w["wt"],
              w["qan"], w["kvan"], w["wuqt"], w["wuk"], w["wuvt"],
              w["gqn"], w["gqr"], w["gkn"], w["gkr"]]
    d_rnn, d_att = w["d_rnn"], MLA_HEADS * V_HEAD
    return pl.pallas_call(
        functools.partial(_in_proj_kernel, q_scale=q_scale),
        grid=(bsz, seq // tm),
        in_specs=[
            pl.BlockSpec((1, tm, d), lambda b, i: (b, i, 0)),
            pl.BlockSpec((1, 3, d), lambda b, i: (b, 0, 0)),
            pl.BlockSpec((1, 1, tm), lambda b, i: (b, 0, i)),
            _const_spec(inv_col.shape),
        ] + [_const_spec(a.shape) for a in consts],
        out_specs=[
            pl.BlockSpec((1, tm, d_rnn), lambda b, i: (b, i, 0)),
            pl.BlockSpec((1, tm, d_rnn), lambda b, i: (b, i, 0)),
            pl.BlockSpec((1, d_att, tm), lambda b, i: (b, 0, i)),
            pl.BlockSpec((1, MLA_HEADS, QK_PAD, tm), lambda b, i: (b, 0, 0, i)),
            pl.BlockSpec((1, MLA_HEADS, tm, QK_PAD), lambda b, i: (b, 0, i, 0)),
            pl.BlockSpec((1, MLA_HEADS, V_HEAD, tm), lambda b, i: (b, 0, 0, i)),
            pl.BlockSpec((1, 1, tm), lambda b, i: (b, 0, i)),
        ],
        out_shape=[
            jax.ShapeDtypeStruct((bsz, seq, d_rnn), BF16),
            jax.ShapeDtypeStruct((bsz, seq, d_rnn), BF16),
            jax.ShapeDtypeStruct((bsz, d_att, seq), BF16),
            jax.ShapeDtypeStruct((bsz, MLA_HEADS, QK_PAD, seq), BF16),
            jax.ShapeDtypeStruct((bsz, MLA_HEADS, seq, QK_PAD), BF16),
            jax.ShapeDtypeStruct((bsz, MLA_HEADS, V_HEAD, seq), BF16),
            jax.ShapeDtypeStruct((bsz, 1, seq), F32),
        ],
        compiler_params=pltpu.CompilerParams(
            dimension_semantics=("arbitrary", "arbitrary"), vmem_limit_bytes=VMEM_LIMIT),
        name="in_proj",
    )(x, mod, pos_row, inv_col, *consts)


ATTN_HEADS_PER_STEP = 4
ATTN_KEY_SUB = 256
ATTN_BLOCKS_PER_TRIP = 4
MAX_SAFE_SCORE_BOUND = 56.0


def _attn_kernel(qt_ref, k_ref, vt_ref, sga_ref, o_ref, acc_ref, *, tq, running_max):
    qi = pl.program_id(2)
    nh = qt_ref.shape[1]

    sub = tq if running_max else ATTN_KEY_SUB

    def first_query(unit, masked):
        return unit[2] * sub if masked[unit[0]] else 0

    def scores(unit, js, masked):
        bi, hh, c = unit
        key0 = pl.multiple_of(js[bi] * tq + c * sub, sub)
        k_sub = k_ref[0, hh, pl.ds(key0, sub), :]
        q_t = qt_ref[0, hh, :, first_query(unit, masked):]
        return jnp.dot(k_sub, q_t, preferred_element_type=F32)

    def accumulate(unit, js, s_t, m, l, masked, init):
        bi, hh, c = unit
        q0 = first_query(unit, masked)
        assert not (init and q0), "the unit that initialises acc must cover every query"
        if masked[bi]:
            key = lax.broadcasted_iota(jnp.int32, s_t.shape, 0) + c * sub
            qry = lax.broadcasted_iota(jnp.int32, s_t.shape, 1) + q0
            s_t = jnp.where(key <= qry, s_t, -jnp.inf)
        key0 = pl.multiple_of(js[bi] * tq + c * sub, sub)
        v_sub = vt_ref[0, hh, :, pl.ds(key0, sub)]
        if running_max:
            m_new = jnp.maximum(m, jnp.max(s_t, axis=0, keepdims=True))
            alpha = jnp.exp2(m - m_new)
            p_t = jnp.exp2(s_t - m_new)
            l_new = alpha * l + jnp.sum(p_t, axis=0, keepdims=True)
            pv = jnp.dot(v_sub, p_t.astype(BF16), preferred_element_type=F32)
            acc_ref[hh] = pv if init else alpha * acc_ref[hh] + pv
        else:
            m_new = m
            p_t = jnp.exp2(s_t)
            l_part = l[:, q0:] + jnp.sum(p_t, axis=0, keepdims=True)
            l_new = l_part if q0 == 0 else jnp.concatenate([l[:, :q0], l_part], axis=1)
            pv = jnp.dot(v_sub, p_t.astype(BF16), preferred_element_type=F32)
            if init:
                acc_ref[hh] = pv
            else:
                acc_ref[hh, :, q0:] += pv
        return m_new, l_new

    def blocks(js, masked, carry, first=False):
        units = [(bi, hh, c) for bi in range(len(js)) for hh in range(nh)
                 for c in range(tq // sub)]
        carry = list(carry)
        s_next = scores(units[0], js, masked)
        for idx, unit in enumerate(units):
            s_cur = s_next
            if idx + 1 < len(units):
                s_next = scores(units[idx + 1], js, masked)
            hh = unit[1]
            init = first and unit[0] == 0 and unit[2] == 0
            carry[hh] = accumulate(unit, js, s_cur, carry[hh][0], carry[hh][1], masked, init)
        return tuple(carry)

    carry = tuple((jnp.full((1, tq), -jnp.inf, F32), jnp.zeros((1, tq), F32)) for _ in range(nh))
    if running_max:
        carry = blocks([qi], [True], carry, first=True)
        carry = lax.fori_loop(0, qi, lambda j, c: blocks([j], [False], c), carry)
    else:
        g = ATTN_BLOCKS_PER_TRIP
        tails = [functools.partial(blocks, [qi - r + t for t in range(r + 1)], [False] * r + [True],
                                   first=True) for r in range(g)]
        carry = lax.switch(qi % g, tails, carry)
        if k_ref.shape[2] // tq > g:
            carry = lax.fori_loop(
                0, qi // g,
                lambda p, c: blocks([g * p + t for t in range(g)], [False] * g, c), carry)

    for hh in range(nh):
        o_t = acc_ref[hh] / carry[hh][1]
        rows = slice(hh * V_HEAD, (hh + 1) * V_HEAD)
        o_ref[0, rows, :] = (o_t * sga_ref[0, rows, :].astype(F32)).astype(BF16)


def _attention(q_t, k, v_t, sga, tq, running_max):
    bsz, nh, _, seq = q_t.shape
    hb = ATTN_HEADS_PER_STEP
    return pl.pallas_call(
        functools.partial(_attn_kernel, tq=tq, running_max=running_max),
        grid=(bsz, nh // hb, seq // tq),
        in_specs=[
            pl.BlockSpec((1, hb, QK_PAD, tq), lambda b, h, i: (b, h, 0, i)),
            pl.BlockSpec((1, hb, seq, QK_PAD), lambda b, h, i: (b, h, 0, 0)),
            pl.BlockSpec((1, hb, V_HEAD, seq), lambda b, h, i: (b, h, 0, 0)),
            pl.BlockSpec((1, hb * V_HEAD, tq), lambda b, h, i: (b, h, i)),
        ],
        out_specs=pl.BlockSpec((1, hb * V_HEAD, tq), lambda b, h, i: (b, h, i)),
        out_shape=jax.ShapeDtypeStruct((bsz, nh * V_HEAD, seq), BF16),
        scratch_shapes=[pltpu.VMEM((hb, V_HEAD, tq), F32)],
        compiler_params=pltpu.CompilerParams(
            dimension_semantics=("arbitrary", "arbitrary", "arbitrary"),
            vmem_limit_bytes=VMEM_LIMIT),
        name="mla_attention_online" if running_max else "mla_attention",
    )(q_t, k, v_t, sga)


SUBLANES = 8
LANES = 128


def _rglru_kernel(xr_ref, sgr_ref, cw_ref, cb_ref, wa_ref, ba_ref, wx_ref, bx_ref, lam_ref,
                  o_ref, tail_ref, a_ref, b_ref, hc_ref, *, ts):
    @pl.when(pl.program_id(1) == 0)
    def _():
        tail_ref[...] = jnp.zeros_like(tail_ref)
        hc_ref[...] = jnp.zeros_like(hc_ref)

    blk = MXU_DIM
    n_delay = CONV_WIDTH - 1
    r_i = lax.broadcasted_iota(jnp.int32, (n_delay * blk, blk), 0)
    c_i = lax.broadcasted_iota(jnp.int32, (n_delay * blk, blk), 1)
    delay_mat = jnp.where((r_i % blk) - c_i == r_i // blk + 1, 1.0, 0.0).astype(BF16)
    row8 = lax.broadcasted_iota(jnp.int32, tail_ref.shape, 0)
    tail = tail_ref[...]
    conv = []
    for b0 in range(0, ts, blk):
        xb = xr_ref[0, b0:b0 + blk, :]
        x0 = xb.astype(F32)
        delayed = jnp.dot(delay_mat, xb, preferred_element_type=F32)
        acc = cb_ref[...] + cw_ref[n_delay:n_delay + 1, :] * x0
        for d in range(1, CONV_WIDTH):
            xd = delayed[(d - 1) * blk:d * blk]
            head = jnp.where(row8 < d, pltpu.roll(tail, d, 0), xd[0:SUBLANES])
            xd = jnp.concatenate([head, xd[SUBLANES:]], axis=0)
            acc = acc + cw_ref[n_delay - d:n_delay - d + 1, :] * xd
        tail = x0[blk - SUBLANES:blk]
        conv.append(acc)
    tail_ref[...] = tail
    xc = jnp.concatenate(conv, axis=0)

    xcb = xc.astype(BF16)
    n_grp = wa_ref.shape[0]

    def gate(w_ref, bias_ref):
        parts = [jnp.dot(xcb[:, g * MXU_DIM:(g + 1) * MXU_DIM], w_ref[g],
                         preferred_element_type=F32) for g in range(n_grp)]
        z = jnp.concatenate(parts, axis=1) + bias_ref[...]
        return 0.5 * jnp.tanh(0.5 * z) + 0.5

    r = gate(wa_ref, ba_ref)
    i = gate(wx_ref, bx_ref)
    nl = -lam_ref[...]
    log_a_unit = (-LRU_C) * (jnp.maximum(nl, 0.0) + jnp.log1p(jnp.exp(-jnp.abs(nl))))
    a_all = jnp.exp2(r * (log_a_unit * math.log2(math.e)))
    t = jnp.tanh(r * log_a_unit)
    u = -2.0 * t
    coef = jnp.where(u > 0.0, u * lax.rsqrt(u * (1.0 - t)), 0.0)
    b_all = coef * (i * xc)
    n_slab = a_ref.shape[0]
    for s in range(n_slab):
        a_ref[s] = a_all[:, s * LANES:(s + 1) * LANES]
        b_ref[s] = b_all[:, s * LANES:(s + 1) * LANES]

    row = lax.broadcasted_iota(jnp.int32, (SUBLANES, LANES), 0)
    block_rows = SUBLANES * SUBLANES

    def scan_block(blk, carry):
        base = pl.multiple_of(blk * block_rows, block_rows)
        new_carry = []
        for s in range(n_slab):
            h_loc, a_run = [], []
            for j in range(SUBLANES):
                rows_j = pl.ds(base + j, SUBLANES, stride=SUBLANES)
                a = a_ref[s, rows_j, :]
                b = b_ref[s, rows_j, :]
                h_loc.append(b if j == 0 else a * h_loc[-1] + b)
                a_run.append(a if j == 0 else a * a_run[-1])
            pa, pb = a_run[-1], h_loc[-1]
            for dist in (1, 2, 4):
                keep = row >= dist
                pb = jnp.where(keep, pa * pltpu.roll(pb, dist, 0) + pb, pb)
                pa = jnp.where(keep, pa * pltpu.roll(pa, dist, 0), pa)
            after = pa * carry[s] + pb
            h_in = jnp.where(row == 0, carry[s], pltpu.roll(after, 1, 0))
            for j in range(SUBLANES):
                rows_j = pl.ds(base + j, SUBLANES, stride=SUBLANES)
                b_ref[s, rows_j, :] = a_run[j] * h_in + h_loc[j]
            new_carry.append(jnp.broadcast_to(after[SUBLANES - 1:SUBLANES, :], after.shape))
        return tuple(new_carry)

    carry = lax.fori_loop(0, ts // block_rows, scan_block,
                          tuple(hc_ref[s] for s in range(n_slab)), unroll=2)
    for s in range(n_slab):
        hc_ref[s] = carry[s]
    h_all = jnp.concatenate([b_ref[s] for s in range(n_slab)], axis=1)
    o_ref[0] = (h_all * sgr_ref[0].astype(F32)).astype(BF16)


def _rglru(xr, sgr, cw, cb, wa, ba, wx, bx, lam, ts):
    bsz, seq, c = xr.shape
    consts = [cw, cb, wa, ba, wx, bx, lam]
    return pl.pallas_call(
        functools.partial(_rglru_kernel, ts=ts),
        grid=(bsz, seq // ts),
        in_specs=[
            pl.BlockSpec((1, ts, c), lambda b, i: (b, i, 0)),
            pl.BlockSpec((1, ts, c), lambda b, i: (b, i, 0)),
        ] + [_const_spec(a.shape) for a in consts],
        out_specs=pl.BlockSpec((1, ts, c), lambda b, i: (b, i, 0)),
        out_shape=jax.ShapeDtypeStruct((bsz, seq, c), BF16),
        scratch_shapes=[
            pltpu.VMEM((SUBLANES, c), F32),
            pltpu.VMEM((c // LANES, ts, LANES), F32),
            pltpu.VMEM((c // LANES, ts, LANES), F32),
            pltpu.VMEM((c // LANES, SUBLANES, LANES), F32),
        ],
        compiler_params=pltpu.CompilerParams(
            dimension_semantics=("arbitrary", "arbitrary"), vmem_limit_bytes=VMEM_LIMIT),
        name="rglru",
    )(xr, sgr, *consts)


def _out_proj_kernel(x_ref, yr_ref, ya_ref, w32_ref, mod_ref, o_ref, w_ref):
    @pl.when((pl.program_id(0) == 0) & (pl.program_id(1) == 0))
    def _():
        w_ref[...] = w32_ref[...].astype(BF16)

    d_rnn = yr_ref.shape[2]
    y = jnp.dot(yr_ref[0], w_ref[0:d_rnn, :], preferred_element_type=F32)
    y = y + lax.dot_general(ya_ref[0], w_ref[d_rnn:, :], TN_DIMS,
                            preferred_element_type=F32)
    o_ref[0] = x_ref[0] + mod_ref[0, 2:3, :] * y


def _out_proj(x, y_rnn, y_att, w_o, mod, tm):
    bsz, seq, d = x.shape
    return pl.pallas_call(
        _out_proj_kernel,
        grid=(bsz, seq // tm),
        in_specs=[
            pl.BlockSpec((1, tm, d), lambda b, i: (b, i, 0)),
            pl.BlockSpec((1, tm, y_rnn.shape[2]), lambda b, i: (b, i, 0)),
            pl.BlockSpec((1, y_att.shape[1], tm), lambda b, i: (b, 0, i)),
            _const_spec(w_o.shape),
            pl.BlockSpec((1, 3, d), lambda b, i: (b, 0, 0)),
        ],
        out_specs=pl.BlockSpec((1, tm, d), lambda b, i: (b, i, 0)),
        out_shape=jax.ShapeDtypeStruct((bsz, seq, d), F32),
        scratch_shapes=[pltpu.VMEM(w_o.shape, BF16)],
        compiler_params=pltpu.CompilerParams(
            dimension_semantics=("arbitrary", "arbitrary"), vmem_limit_bytes=VMEM_LIMIT),
        name="out_proj",
    )(x, y_rnn, y_att, w_o, mod)


def _block_diag_groups(w):
    nb, n, _ = w.shape
    per = MXU_DIM // n
    w = w.reshape(nb // per, per, n, n)
    eye = jnp.eye(per, dtype=w.dtype)
    return jnp.einsum("gpij,pq->gpiqj", w, eye).reshape(nb // per, MXU_DIM, MXU_DIM)


def kernel(x, c, positions, w_ada, b_ada, w_in, conv_w, conv_b, w_rg_a, b_rg_a, w_rg_x, b_rg_x,
           lru_lambda, q_a_norm, w_uq, kv_a_norm, w_ukv, q_norm_nope, q_norm_rope, k_norm_nope,
           k_norm_rope, w_out):
    bsz, seq, d = x.shape
    depth = w_in.shape[0]
    d_rnn = conv_w.shape[2]
    q_lora = q_a_norm.shape[1]
    kv_lora = kv_a_norm.shape[1]
    tile = 512

    inv_freq = 1.0 / (ROPE_THETA ** (jnp.arange(0, QK_ROPE, 2, dtype=F32) / QK_ROPE))
    pos = positions.astype(F32)
    pos_row = pos.reshape(bsz, 1, seq)
    inv_col = inv_freq.reshape(-1, 1)
    c_pad = jnp.zeros((SUBLANES, d), F32).at[:bsz].set(c)

    for l in range(depth):
        mod = _adaln(c_pad, w_ada[l], b_ada[l].reshape(1, -1))[:bsz].reshape(bsz, 3, d)

        ukv = w_ukv[l].reshape(kv_lora, MLA_HEADS, QK_NOPE + V_HEAD)
        w = dict(
            d_rnn=d_rnn, wt=jnp.swapaxes(w_in[l], 0, 1).astype(BF16),
            qan=q_a_norm[l].reshape(1, -1), kvan=kv_a_norm[l].reshape(1, -1),
            wuqt=w_uq[l].T.astype(BF16),
            wuk=ukv[:, :, :QK_NOPE].reshape(kv_lora, -1).astype(BF16),
            wuvt=ukv[:, :, QK_NOPE:].reshape(kv_lora, -1).T.astype(BF16),
            gqn=q_norm_nope[l].reshape(-1, 1), gqr=q_norm_rope[l].reshape(-1, 1),
            gkn=k_norm_nope[l].reshape(1, -1), gkr=k_norm_rope[l].reshape(-1, 1),
        )
        xr, sgr, sga, q_t, k, v_t, bound = _in_proj(x, mod, pos_row, inv_col, w, tile)

        y_att = lax.cond(
            jnp.max(bound) < MAX_SAFE_SCORE_BOUND,
            functools.partial(_attention, tq=tile, running_max=False),
            functools.partial(_attention, tq=tile, running_max=True),
            q_t, k, v_t, sga)
        y_rnn = _rglru(
            xr, sgr, conv_w[l], conv_b[l].reshape(1, -1),
            _block_diag_groups(w_rg_a[l]).astype(BF16), b_rg_a[l].reshape(1, -1),
            _block_diag_groups(w_rg_x[l]).astype(BF16), b_rg_x[l].reshape(1, -1),
            lru_lambda[l].reshape(1, -1), tile)

        x = _out_proj(x, y_rnn, y_att, w_out[l], mod, tile)
    return x
```

```python
import functools
import math

import jax
import jax.numpy as jnp
from jax import lax
from jax.experimental import pallas as pl
from jax.experimental.pallas import tpu as pltpu

CONV_WIDTH = 4
LRU_C = 8.0
MLA_HEADS = 8
QK_NOPE = 128
QK_ROPE = 64
V_HEAD = 128
ROPE_THETA = 10000.0
EPS = 1e-6

QK_PAD = 256
MXU_DIM = 256
VMEM_LIMIT = 56 * 1024 * 1024
ROW_TILE = 512
RGLRU_ROW_TILE = 1024
ADALN_COL_TILE = 768

F32 = jnp.float32
BF16 = jnp.bfloat16

NT_DIMS = (((1,), (1,)), ((), ()))
TN_DIMS = (((0,), (0,)), ((), ()))


def _const_spec(shape):
    nd = len(shape)
    return pl.BlockSpec(shape, lambda *_: (0,) * nd, pipeline_mode=pl.Buffered(1))


def _adaln_kernel(c_ref, w_ref, b_ref, o_ref):
    c = c_ref[...]
    c_act = (c * jax.nn.sigmoid(c)).astype(BF16)
    o_ref[...] = jnp.dot(c_act, w_ref[...].astype(BF16), preferred_element_type=F32) + b_ref[...]


def _adaln(c_pad, w_ada, b_ada):
    rows, d = c_pad.shape
    n = w_ada.shape[1]
    tn = ADALN_COL_TILE
    return pl.pallas_call(
        _adaln_kernel,
        grid=(n // tn,),
        in_specs=[
            pl.BlockSpec((rows, d), lambda j: (0, 0)),
            pl.BlockSpec((d, tn), lambda j: (0, j)),
            pl.BlockSpec((1, tn), lambda j: (0, j)),
        ],
        out_specs=pl.BlockSpec((rows, tn), lambda j: (0, j)),
        out_shape=jax.ShapeDtypeStruct((rows, n), F32),
        compiler_params=pltpu.CompilerParams(
            dimension_semantics=("arbitrary",), vmem_limit_bytes=VMEM_LIMIT),
        name="adaln_mod",
    )(c_pad, w_ada, b_ada)


IN_PROJ_ROWS = 256


def _rms(v, axis):
    return v * lax.rsqrt(jnp.mean(v * v, axis=axis, keepdims=True) + EPS)


def _silu(v):
    half = 0.5 * v
    return half * jnp.tanh(half) + half


def _in_proj_kernel(x_ref, mod_ref, posr_ref, invc_ref, wt_ref,
                    qan_ref, kvan_ref, wuqt_ref, wuk_ref, wuvt_ref,
                    gqn_ref, gqr_ref, gkn_ref, gkr_ref,
                    xr_ref, sgr_ref, sga_ref, qt_ref, k_ref, vt_ref, bound_ref, *, q_scale):
    d_rnn, d_att = xr_ref.shape[2], sga_ref.shape[1]
    q_lora, kv_lora = qan_ref.shape[1], kvan_ref.shape[1]
    qc0, kvc0 = 2 * d_rnn, 2 * d_rnn + q_lora
    kr0 = kvc0 + kv_lora
    ga0 = kr0 + QK_ROPE
    scale = mod_ref[0, 1:2, :]
    shift = mod_ref[0, 0:1, :]
    hd = QK_NOPE + QK_ROPE
    half = QK_ROPE // 2
    gkn, gkr = gkn_ref[...], gkr_ref[...]
    k_bound = jnp.sqrt(QK_NOPE * jnp.max(gkn * gkn, axis=1, keepdims=True)
                       + QK_ROPE * jnp.max(gkr * gkr, axis=0, keepdims=True))
    pad_row = lax.broadcasted_iota(jnp.int32, (QK_PAD - hd, IN_PROJ_ROWS), 0)

    chunks = [slice(r0, r0 + IN_PROJ_ROWS) for r0 in range(0, x_ref.shape[1], IN_PROJ_ROWS)]
    hs = [(_rms(x_ref[0, rs, :], -1) * (1.0 + scale) + shift).astype(BF16) for rs in chunks]
    for rs, h in zip(chunks, hs):

        def proj(lo, hi):
            return lax.dot_general(h, wt_ref[lo:hi, :], NT_DIMS, preferred_element_type=F32)

        latent = proj(qc0, kr0)
        qcn = (_rms(latent[:, :q_lora], -1) * qan_ref[...]).astype(BF16)
        kvcn = (_rms(latent[:, q_lora:], -1) * kvan_ref[...]).astype(BF16)
        kg_t = lax.dot_general(wt_ref[kr0:ga0 + d_att // 2, :], h, NT_DIMS,
                               preferred_element_type=F32)
        kr_t = kg_t[:QK_ROPE]
        sga_ref[0, 0:d_att // 2, rs] = _silu(kg_t[QK_ROPE:]).astype(BF16)
        q_t = lax.dot_general(wuqt_ref[...], qcn, NT_DIMS, preferred_element_type=F32)
        v_t = lax.dot_general(wuvt_ref[...], kvcn, NT_DIMS, preferred_element_type=F32)
        kn_all = jnp.dot(kvcn, wuk_ref[...], preferred_element_type=F32)

        ang_t = invc_ref[...] * posr_ref[0, :, rs]
        cos_t, sin_t = jnp.cos(ang_t), jnp.sin(ang_t)

        def rope_t(v):
            v1, v2 = v[:half], v[half:]
            return v1 * cos_t - v2 * sin_t, v1 * sin_t + v2 * cos_t

        bound_max = None
        for hh in range(MLA_HEADS):
            qn = q_t[hh * hd:hh * hd + QK_NOPE]
            qn = _rms(qn, 0) * (gqn_ref[...] * q_scale)
            qr = q_t[hh * hd + QK_NOPE:(hh + 1) * hd]
            r1, r2 = rope_t(_rms(qr, 0) * (gqr_ref[...] * q_scale))
            q_sq = (jnp.sum(qn * qn, axis=0, keepdims=True)
                    + jnp.sum(r1 * r1 + r2 * r2, axis=0, keepdims=True))
            bound = jnp.sqrt(q_sq) * k_bound
            bound_max = bound if bound_max is None else jnp.maximum(bound_max, bound)
            qt_ref[0, hh, 0:QK_NOPE, rs] = qn.astype(BF16)
            qt_ref[0, hh, QK_NOPE:QK_NOPE + half, rs] = r1.astype(BF16)
            qt_ref[0, hh, QK_NOPE + half:hd, rs] = r2.astype(BF16)
            qt_ref[0, hh, hd:QK_PAD, rs] = jnp.where(pad_row == 0, -bound, 0.0).astype(BF16)
        bound_ref[0, :, rs] = bound_max

        for hh in range(MLA_HEADS):
            vt_ref[0, hh, :, rs] = v_t[hh * V_HEAD:(hh + 1) * V_HEAD].astype(BF16)
        k1, k2 = rope_t(_rms(kr_t, 0) * gkr)
        kpe_t = jnp.concatenate([k1, k2, jnp.where(pad_row == 0, 1.0, 0.0)], axis=0)
        kpe = kpe_t.T.astype(BF16)
        for hh in range(MLA_HEADS):
            kn = kn_all[:, hh * QK_NOPE:(hh + 1) * QK_NOPE]
            k_ref[0, hh, rs, 0:QK_NOPE] = (_rms(kn, -1) * gkn).astype(BF16)
            k_ref[0, hh, rs, QK_NOPE:QK_PAD] = kpe

        xr_ref[0, rs, :] = proj(0, d_rnn).astype(BF16)
        gr = proj(d_rnn, 2 * d_rnn)
        sgr_ref[0, rs, :] = _silu(gr).astype(BF16)
        ga_t = lax.dot_general(wt_ref[ga0 + d_att // 2:ga0 + d_att, :], h, NT_DIMS,
                               preferred_element_type=F32)
        sga_ref[0, d_att // 2:d_att, rs] = _silu(ga_t).astype(BF16)


def _in_proj(x, mod, pos_row, inv_col, w, tm):
    bsz, seq, d = x.shape
    hd = QK_NOPE + QK_ROPE
    q_scale = (hd ** -0.5) * math.log2(math.e)
    consts = [w["wt"],
              w["qan"], w["kvan"], w["wuqt"], w["wuk"], w["wuvt"],
              w["gqn"], w["gqr"], w["gkn"], w["gkr"]]
    d_rnn, d_att = w["d_rnn"], MLA_HEADS * V_HEAD
    return pl.pallas_call(
        functools.partial(_in_proj_kernel, q_scale=q_scale),
        grid=(bsz, seq // tm),
        in_specs=[
            pl.BlockSpec((1, tm, d), lambda b, i: (b, i, 0)),
            pl.BlockSpec((1, 3, d), lambda b, i: (b, 0, 0)),
            pl.BlockSpec((1, 1, tm), lambda b, i: (b, 0, i)),
            _const_spec(inv_col.shape),
        ] + [_const_spec(a.shape) for a in consts],
        out_specs=[
            pl.BlockSpec((1, tm, d_rnn), lambda b, i: (b, i, 0)),
            pl.BlockSpec((1, tm, d_rnn), lambda b, i: (b, i, 0)),
            pl.BlockSpec((1, d_att, tm), lambda b, i: (b, 0, i)),
            pl.BlockSpec((1, MLA_HEADS, QK_PAD, tm), lambda b, i: (b, 0, 0, i)),
            pl.BlockSpec((1, MLA_HEADS, tm, QK_PAD), lambda b, i: (b, 0, i, 0)),
            pl.BlockSpec((1, MLA_HEADS, V_HEAD, tm), lambda b, i: (b, 0, 0, i)),
            pl.BlockSpec((1, 1, tm), lambda b, i: (b, 0, i)),
        ],
        out_shape=[
            jax.ShapeDtypeStruct((bsz, seq, d_rnn), BF16),
            jax.ShapeDtypeStruct((bsz, seq, d_rnn), BF16),
            jax.ShapeDtypeStruct((bsz, d_att, seq), BF16),
            jax.ShapeDtypeStruct((bsz, MLA_HEADS, QK_PAD, seq), BF16),
            jax.ShapeDtypeStruct((bsz, MLA_HEADS, seq, QK_PAD), BF16),
            jax.ShapeDtypeStruct((bsz, MLA_HEADS, V_HEAD, seq), BF16),
            jax.ShapeDtypeStruct((bsz, 1, seq), F32),
        ],
        compiler_params=pltpu.CompilerParams(
            dimension_semantics=("arbitrary", "arbitrary"), vmem_limit_bytes=VMEM_LIMIT),
        name="in_proj",
    )(x, mod, pos_row, inv_col, *consts)


ATTN_HEADS_PER_STEP = 4
ATTN_KEY_SUB = 256
ATTN_BLOCKS_PER_TRIP = 4
MAX_SAFE_SCORE_BOUND = 56.0


def _attn_kernel(qt_ref, k_ref, vt_ref, sga_ref, o_ref, acc_ref, *, tq, running_max):
    qi = pl.program_id(2)
    nh = qt_ref.shape[1]

    sub = tq if running_max else ATTN_KEY_SUB

    def first_query(unit, masked):
        return unit[2] * sub if masked[unit[0]] else 0

    def scores(unit, js, masked):
        bi, hh, c = unit
        key0 = pl.multiple_of(js[bi] * tq + c * sub, sub)
        k_sub = k_ref[0, hh, pl.ds(key0, sub), :]
        q_t = qt_ref[0, hh, :, first_query(unit, masked):]
        return jnp.dot(k_sub, q_t, preferred_element_type=F32)

    def accumulate(unit, js, s_t, m, l, masked, init):
        bi, hh, c = unit
        q0 = first_query(unit, masked)
        assert not (init and q0), "the unit that initialises acc must cover every query"
        if masked[bi]:
            key = lax.broadcasted_iota(jnp.int32, s_t.shape, 0) + c * sub
            qry = lax.broadcasted_iota(jnp.int32, s_t.shape, 1) + q0
            s_t = jnp.where(key <= qry, s_t, -jnp.inf)
        key0 = pl.multiple_of(js[bi] * tq + c * sub, sub)
        v_sub = vt_ref[0, hh, :, pl.ds(key0, sub)]
        if running_max:
            m_new = jnp.maximum(m, jnp.max(s_t, axis=0, keepdims=True))
            alpha = jnp.exp2(m - m_new)
            p_t = jnp.exp2(s_t - m_new)
            l_new = alpha * l + jnp.sum(p_t, axis=0, keepdims=True)
            pv = jnp.dot(v_sub, p_t.astype(BF16), preferred_element_type=F32)
            acc_ref[hh] = pv if init else alpha * acc_ref[hh] + pv
        else:
            m_new = m
            p_t = jnp.exp2(s_t)
            l_part = l[:, q0:] + jnp.sum(p_t, axis=0, keepdims=True)
            l_new = l_part if q0 == 0 else jnp.concatenate([l[:, :q0], l_part], axis=1)
            pv = jnp.dot(v_sub, p_t.astype(BF16), preferred_element_type=F32)
            if init:
                acc_ref[hh] = pv
            else:
                acc_ref[hh, :, q0:] += pv
        return m_new, l_new

    def blocks(js, masked, carry, first=False):
        units = [(bi, hh, c) for bi in range(len(js)) for hh in range(nh)
                 for c in range(tq // sub)]
        carry = list(carry)
        s_next = scores(units[0], js, masked)
        for idx, unit in enumerate(units):
            s_cur = s_next
            if idx + 1 < len(units):
                s_next = scores(units[idx + 1], js, masked)
            hh = unit[1]
            init = first and unit[0] == 0 and unit[2] == 0
            carry[hh] = accumulate(unit, js, s_cur, carry[hh][0], carry[hh][1], masked, init)
        return tuple(carry)

    carry = tuple((jnp.full((1, tq), -jnp.inf, F32), jnp.zeros((1, tq), F32)) for _ in range(nh))
    if running_max:
        carry = blocks([qi], [True], carry, first=True)
        carry = lax.fori_loop(0, qi, lambda j, c: blocks([j], [False], c), carry)
    else:
        g = ATTN_BLOCKS_PER_TRIP
        tails = [functools.partial(blocks, [qi - r + t for t in range(r + 1)], [False] * r + [True],
                                   first=True) for r in range(g)]
        carry = lax.switch(qi % g, tails, carry)
        if k_ref.shape[2] // tq > g:
            carry = lax.fori_loop(
                0, qi // g,
                lambda p, c: blocks([g * p + t for t in range(g)], [False] * g, c), carry)

    for hh in range(nh):
        o_t = acc_ref[hh] / carry[hh][1]
        rows = slice(hh * V_HEAD, (hh + 1) * V_HEAD)
        o_ref[0, rows, :] = (o_t * sga_ref[0, rows, :].astype(F32)).astype(BF16)


def _attention(q_t, k, v_t, sga, tq, running_max):
    bsz, nh, _, seq = q_t.shape
    hb = ATTN_HEADS_PER_STEP
    return pl.pallas_call(
        functools.partial(_attn_kernel, tq=tq, running_max=running_max),
        grid=(bsz, nh // hb, seq // tq),
        in_specs=[
            pl.BlockSpec((1, hb, QK_PAD, tq), lambda b, h, i: (b, h, 0, i)),
            pl.BlockSpec((1, hb, seq, QK_PAD), lambda b, h, i: (b, h, 0, 0)),
            pl.BlockSpec((1, hb, V_HEAD, seq), lambda b, h, i: (b, h, 0, 0)),
            pl.BlockSpec((1, hb * V_HEAD, tq), lambda b, h, i: (b, h, i)),
        ],
        out_specs=pl.BlockSpec((1, hb * V_HEAD, tq), lambda b, h, i: (b, h, i)),
        out_shape=jax.ShapeDtypeStruct((bsz, nh * V_HEAD, seq), BF16),
        scratch_shapes=[pltpu.VMEM((hb, V_HEAD, tq), F32)],
        compiler_params=pltpu.CompilerParams(
            dimension_semantics=("arbitrary", "arbitrary", "arbitrary"),
            vmem_limit_bytes=VMEM_LIMIT),
        name="mla_attention_online" if running_max else "mla_attention",
    )(q_t, k, v_t, sga)


SUBLANES = 8
LANES = 128


def _rglru_kernel(xr_ref, sgr_ref, cw_ref, cb_ref, wa_ref, ba_ref, wx_ref, bx_ref, lam_ref,
                  o_ref, tail_ref, a_ref, b_ref, hc_ref, *, ts):
    @pl.when(pl.program_id(1) == 0)
    def _():
        tail_ref[...] = jnp.zeros_like(tail_ref)
        hc_ref[...] = jnp.zeros_like(hc_ref)

    blk = MXU_DIM
    n_delay = CONV_WIDTH - 1
    r_i = lax.broadcasted_iota(jnp.int32, (n_delay * blk, blk), 0)
    c_i = lax.broadcasted_iota(jnp.int32, (n_delay * blk, blk), 1)
    delay_mat = jnp.where((r_i % blk) - c_i == r_i // blk + 1, 1.0, 0.0).astype(BF16)
    row8 = lax.broadcasted_iota(jnp.int32, tail_ref.shape, 0)
    tail = tail_ref[...]
    conv = []
    for b0 in range(0, ts, blk):
        xb = xr_ref[0, b0:b0 + blk, :]
        x0 = xb.astype(F32)
        delayed = jnp.dot(delay_mat, xb, preferred_element_type=F32)
        acc = cb_ref[...] + cw_ref[n_delay:n_delay + 1, :] * x0
        for d in range(1, CONV_WIDTH):
            xd = delayed[(d - 1) * blk:d * blk]
            head = jnp.where(row8 < d, pltpu.roll(tail, d, 0), xd[0:SUBLANES])
            xd = jnp.concatenate([head, xd[SUBLANES:]], axis=0)
            acc = acc + cw_ref[n_delay - d:n_delay - d + 1, :] * xd
        tail = x0[blk - SUBLANES:blk]
        conv.append(acc)
    tail_ref[...] = tail
    xc = jnp.concatenate(conv, axis=0)

    xcb = xc.astype(BF16)
    n_grp = wa_ref.shape[0]

    def gate(w_ref, bias_ref):
        parts = [jnp.dot(xcb[:, g * MXU_DIM:(g + 1) * MXU_DIM], w_ref[g],
                         preferred_element_type=F32) for g in range(n_grp)]
        z = jnp.concatenate(parts, axis=1) + bias_ref[...]
        return 0.5 * jnp.tanh(0.5 * z) + 0.5

    r = gate(wa_ref, ba_ref)
    i = gate(wx_ref, bx_ref)
    nl = -lam_ref[...]
    log_a_unit = (-LRU_C) * (jnp.maximum(nl, 0.0) + jnp.log1p(jnp.exp(-jnp.abs(nl))))
    a_all = jnp.exp2(r * (log_a_unit * math.log2(math.e)))
    t = jnp.tanh(r * log_a_unit)
    u = -2.0 * t
    coef = jnp.where(u > 0.0, u * lax.rsqrt(u * (1.0 - t)), 0.0)
    b_all = coef * (i * xc)
    n_slab = a_ref.shape[0]
    for s in range(n_slab):
        a_ref[s] = a_all[:, s * LANES:(s + 1) * LANES]
        b_ref[s] = b_all[:, s * LANES:(s + 1) * LANES]

    row = lax.broadcasted_iota(jnp.int32, (SUBLANES, LANES), 0)
    block_rows = SUBLANES * SUBLANES

    def scan_block(blk, carry):
        base = pl.multiple_of(blk * block_rows, block_rows)
        new_carry = []
        for s in range(n_slab):
            h_loc, a_run = [], []
            for j in range(SUBLANES):
                rows_j = pl.ds(base + j, SUBLANES, stride=SUBLANES)
                a = a_ref[s, rows_j, :]
                b = b_ref[s, rows_j, :]
                h_loc.append(b if j == 0 else a * h_loc[-1] + b)
                a_run.append(a if j == 0 else a * a_run[-1])
            pa, pb = a_run[-1], h_loc[-1]
            for dist in (1, 2, 4):
                keep = row >= dist
                pb = jnp.where(keep, pa * pltpu.roll(pb, dist, 0) + pb, pb)
                pa = jnp.where(keep, pa * pltpu.roll(pa, dist, 0), pa)
            after = pa * carry[s] + pb
            h_in = jnp.where(row == 0, carry[s], pltpu.roll(after, 1, 0))
            for j in range(SUBLANES):
                rows_j = pl.ds(base + j, SUBLANES, stride=SUBLANES)
                b_ref[s, rows_j, :] = a_run[j] * h_in + h_loc[j]
            new_carry.append(jnp.broadcast_to(after[SUBLANES - 1:SUBLANES, :], after.shape))
        return tuple(new_carry)

    carry = lax.fori_loop(0, ts // block_rows, scan_block,
                          tuple(hc_ref[s] for s in range(n_slab)), unroll=2)
    for s in range(n_slab):
        hc_ref[s] = carry[s]
    h_all = jnp.concatenate([b_ref[s] for s in range(n_slab)], axis=1)
    o_ref[0] = (h_all * sgr_ref[0].astype(F32)).astype(BF16)


def _rglru(xr, sgr, cw, cb, wa, ba, wx, bx, lam, ts):
    bsz, seq, c = xr.shape
    consts = [cw, cb, wa, ba, wx, bx, lam]
    return pl.pallas_call(
        functools.partial(_rglru_kernel, ts=ts),
        grid=(bsz, seq // ts),
        in_specs=[
            pl.BlockSpec((1, ts, c), lambda b, i: (b, i, 0)),
            pl.BlockSpec((1, ts, c), lambda b, i: (b, i, 0)),
        ] + [_const_spec(a.shape) for a in consts],
        out_specs=pl.BlockSpec((1, ts, c), lambda b, i: (b, i, 0)),
        out_shape=jax.ShapeDtypeStruct((bsz, seq, c), BF16),
        scratch_shapes=[
            pltpu.VMEM((SUBLANES, c), F32),
            pltpu.VMEM((c // LANES, ts, LANES), F32),
            pltpu.VMEM((c // LANES, ts, LANES), F32),
            pltpu.VMEM((c // LANES, SUBLANES, LANES), F32),
        ],
        compiler_params=pltpu.CompilerParams(
            dimension_semantics=("arbitrary", "arbitrary"), vmem_limit_bytes=VMEM_LIMIT),
        name="rglru",
    )(xr, sgr, *consts)


def _out_proj_kernel(x_ref, yr_ref, ya_ref, w32_ref, mod_ref, o_ref, w_ref):
    @pl.when((pl.program_id(0) == 0) & (pl.program_id(1) == 0))
    def _():
        w_ref[...] = w32_ref[...].astype(BF16)

    d_rnn = yr_ref.shape[2]
    y = jnp.dot(yr_ref[0], w_ref[0:d_rnn, :], preferred_element_type=F32)
    y = y + lax.dot_general(ya_ref[0], w_ref[d_rnn:, :], TN_DIMS,
                            preferred_element_type=F32)
    o_ref[0] = x_ref[0] + mod_ref[0, 2:3, :] * y


def _out_proj(x, y_rnn, y_att, w_o, mod, tm):
    bsz, seq, d = x.shape
    return pl.pallas_call(
        _out_proj_kernel,
        grid=(bsz, seq // tm),
        in_specs=[
            pl.BlockSpec((1, tm, d), lambda b, i: (b, i, 0)),
            pl.BlockSpec((1, tm, y_rnn.shape[2]), lambda b, i: (b, i, 0)),
            pl.BlockSpec((1, y_att.shape[1], tm), lambda b, i: (b, 0, i)),
            _const_spec(w_o.shape),
            pl.BlockSpec((1, 3, d), lambda b, i: (b, 0, 0)),
        ],
        out_specs=pl.BlockSpec((1, tm, d), lambda b, i: (b, i, 0)),
        out_shape=jax.ShapeDtypeStruct((bsz, seq, d), F32),
        scratch_shapes=[pltpu.VMEM(w_o.shape, BF16)],
        compiler_params=pltpu.CompilerParams(
            dimension_semantics=("arbitrary", "arbitrary"), vmem_limit_bytes=VMEM_LIMIT),
        name="out_proj",
    )(x, y_rnn, y_att, w_o, mod)


def _block_diag_groups(w):
    nb, n, _ = w.shape
    per = MXU_DIM // n
    w = w.reshape(nb // per, per, n, n)
    eye = jnp.eye(per, dtype=w.dtype)
    return jnp.einsum("gpij,pq->gpiqj", w, eye).reshape(nb // per, MXU_DIM, MXU_DIM)


def kernel(x, c, positions, w_ada, b_ada, w_in, conv_w, conv_b, w_rg_a, b_rg_a, w_rg_x, b_rg_x,
           lru_lambda, q_a_norm, w_uq, kv_a_norm, w_ukv, q_norm_nope, q_norm_rope, k_norm_nope,
           k_norm_rope, w_out):
    bsz, seq, d = x.shape
    depth = w_in.shape[0]
    d_rnn = conv_w.shape[2]
    q_lora = q_a_norm.shape[1]
    kv_lora = kv_a_norm.shape[1]
    tile = ROW_TILE

    inv_freq = 1.0 / (ROPE_THETA ** (jnp.arange(0, QK_ROPE, 2, dtype=F32) / QK_ROPE))
    pos = positions.astype(F32)
    pos_row = pos.reshape(bsz, 1, seq)
    inv_col = inv_freq.reshape(-1, 1)
    c_pad = jnp.zeros((SUBLANES, d), F32).at[:bsz].set(c)

    for l in range(depth):
        mod = _adaln(c_pad, w_ada[l], b_ada[l].reshape(1, -1))[:bsz].reshape(bsz, 3, d)

        ukv = w_ukv[l].reshape(kv_lora, MLA_HEADS, QK_NOPE + V_HEAD)
        w = dict(
            d_rnn=d_rnn, wt=jnp.swapaxes(w_in[l], 0, 1).astype(BF16),
            qan=q_a_norm[l].reshape(1, -1), kvan=kv_a_norm[l].reshape(1, -1),
            wuqt=w_uq[l].T.astype(BF16),
            wuk=ukv[:, :, :QK_NOPE].reshape(kv_lora, -1).astype(BF16),
            wuvt=ukv[:, :, QK_NOPE:].reshape(kv_lora, -1).T.astype(BF16),
            gqn=q_norm_nope[l].reshape(-1, 1), gqr=q_norm_rope[l].reshape(-1, 1),
            gkn=k_norm_nope[l].reshape(1, -1), gkr=k_norm_rope[l].reshape(-1, 1),
        )
        xr, sgr, sga, q_t, k, v_t, bound = _in_proj(x, mod, pos_row, inv_col, w, tile)

        y_att = lax.cond(
            jnp.max(bound) < MAX_SAFE_SCORE_BOUND,
            functools.partial(_attention, tq=tile, running_max=False),
            functools.partial(_attention, tq=tile, running_max=True),
            q_t, k, v_t, sga)
        y_rnn = _rglru(
            xr, sgr, conv_w[l], conv_b[l].reshape(1, -1),
            _block_diag_groups(w_rg_a[l]).astype(BF16), b_rg_a[l].reshape(1, -1),
            _block_diag_groups(w_rg_x[l]).astype(BF16), b_rg_x[l].reshape(1, -1),
            lru_lambda[l].reshape(1, -1), RGLRU_ROW_TILE)

        x = _out_proj(x, y_rnn, y_att, w_out[l], mod, tile)
    return x
```

```python
import functools
import math

import jax
import jax.numpy as jnp
from jax import lax
from jax.experimental import pallas as pl
from jax.experimental.pallas import tpu as pltpu

CONV_WIDTH = 4
LRU_C = 8.0
MLA_HEADS = 8
QK_NOPE = 128
QK_ROPE = 64
V_HEAD = 128
ROPE_THETA = 10000.0
EPS = 1e-6

QK_PAD = 256
MXU_DIM = 256
VMEM_LIMIT = 56 * 1024 * 1024
ROW_TILE = 512
RGLRU_ROW_TILE = 1024
ADALN_COL_TILE = 768

F32 = jnp.float32
BF16 = jnp.bfloat16

NT_DIMS = (((1,), (1,)), ((), ()))
TN_DIMS = (((0,), (0,)), ((), ()))


def _const_spec(shape):
    nd = len(shape)
    return pl.BlockSpec(shape, lambda *_: (0,) * nd, pipeline_mode=pl.Buffered(1))


def _adaln_kernel(c_ref, w_ref, b_ref, o_ref):
    c = c_ref[...]
    c_act = (c * jax.nn.sigmoid(c)).astype(BF16)
    o_ref[...] = jnp.dot(c_act, w_ref[...].astype(BF16), preferred_element_type=F32) + b_ref[...]


def _adaln(c_pad, w_ada, b_ada):
    rows, d = c_pad.shape
    n = w_ada.shape[1]
    tn = ADALN_COL_TILE
    return pl.pallas_call(
        _adaln_kernel,
        grid=(n // tn,),
        in_specs=[
            pl.BlockSpec((rows, d), lambda j: (0, 0)),
            pl.BlockSpec((d, tn), lambda j: (0, j)),
            pl.BlockSpec((1, tn), lambda j: (0, j)),
        ],
        out_specs=pl.BlockSpec((rows, tn), lambda j: (0, j)),
        out_shape=jax.ShapeDtypeStruct((rows, n), F32),
        compiler_params=pltpu.CompilerParams(
            dimension_semantics=("arbitrary",), vmem_limit_bytes=VMEM_LIMIT),
        name="adaln_mod",
    )(c_pad, w_ada, b_ada)


IN_PROJ_ROWS = 256


def _rms(v, axis):
    return v * lax.rsqrt(jnp.mean(v * v, axis=axis, keepdims=True) + EPS)


def _silu(v):
    half = 0.5 * v
    return half * jnp.tanh(half) + half


def _in_proj_kernel(x_ref, mod_ref, posr_ref, invc_ref, wt_ref,
                    qan_ref, kvan_ref, wuqt_ref, wuk_ref, wuvt_ref,
                    gqn_ref, gqr_ref, gkn_ref, gkr_ref,
                    xr_ref, sgr_ref, sga_ref, qt_ref, k_ref, vt_ref, bound_ref, *, q_scale):
    d_rnn, d_att = xr_ref.shape[2], sga_ref.shape[1]
    q_lora, kv_lora = qan_ref.shape[1], kvan_ref.shape[1]
    qc0, kvc0 = 2 * d_rnn, 2 * d_rnn + q_lora
    kr0 = kvc0 + kv_lora
    ga0 = kr0 + QK_ROPE
    scale = mod_ref[0, 1:2, :]
    shift = mod_ref[0, 0:1, :]
    hd = QK_NOPE + QK_ROPE
    half = QK_ROPE // 2
    gkn, gkr = gkn_ref[...], gkr_ref[...]
    gqn, gqr = gqn_ref[...] * q_scale, gqr_ref[...] * q_scale
    k_bound = jnp.sqrt(QK_NOPE * jnp.max(gkn * gkn, axis=1, keepdims=True)
                       + QK_ROPE * jnp.max(gkr * gkr, axis=0, keepdims=True))
    q_bound = jnp.sqrt(QK_NOPE * jnp.max(gqn * gqn, axis=0, keepdims=True)
                       + QK_ROPE * jnp.max(gqr * gqr, axis=0, keepdims=True))
    bound = q_bound * k_bound
    pad_row = lax.broadcasted_iota(jnp.int32, (QK_PAD - hd, IN_PROJ_ROWS), 0)
    q_pad = jnp.where(pad_row == 0, -bound, 0.0).astype(BF16)

    chunks = [slice(r0, r0 + IN_PROJ_ROWS) for r0 in range(0, x_ref.shape[1], IN_PROJ_ROWS)]
    hs = [(_rms(x_ref[0, rs, :], -1) * (1.0 + scale) + shift).astype(BF16) for rs in chunks]
    for rs, h in zip(chunks, hs):

        def proj(lo, hi):
            return lax.dot_general(h, wt_ref[lo:hi, :], NT_DIMS, preferred_element_type=F32)

        latent = proj(qc0, kr0)
        qcn = (_rms(latent[:, :q_lora], -1) * qan_ref[...]).astype(BF16)
        kvcn = (_rms(latent[:, q_lora:], -1) * kvan_ref[...]).astype(BF16)
        kg_t = lax.dot_general(wt_ref[kr0:ga0 + d_att // 2, :], h, NT_DIMS,
                               preferred_element_type=F32)
        kr_t = kg_t[:QK_ROPE]
        sga_ref[0, 0:d_att // 2, rs] = _silu(kg_t[QK_ROPE:]).astype(BF16)
        q_t = lax.dot_general(wuqt_ref[...], qcn, NT_DIMS, preferred_element_type=F32)
        v_t = lax.dot_general(wuvt_ref[...], kvcn, NT_DIMS, preferred_element_type=F32)
        kn_all = jnp.dot(kvcn, wuk_ref[...], preferred_element_type=F32)

        ang_t = invc_ref[...] * posr_ref[0, :, rs]
        cos_t, sin_t = jnp.cos(ang_t), jnp.sin(ang_t)

        def rope_t(v):
            v1, v2 = v[:half], v[half:]
            return v1 * cos_t - v2 * sin_t, v1 * sin_t + v2 * cos_t

        for hh in range(MLA_HEADS):
            qn = _rms(q_t[hh * hd:hh * hd + QK_NOPE], 0) * gqn
            r1, r2 = rope_t(_rms(q_t[hh * hd + QK_NOPE:(hh + 1) * hd], 0) * gqr)
            qt_ref[0, hh, 0:QK_NOPE, rs] = qn.astype(BF16)
            qt_ref[0, hh, QK_NOPE:QK_NOPE + half, rs] = r1.astype(BF16)
            qt_ref[0, hh, QK_NOPE + half:hd, rs] = r2.astype(BF16)
            qt_ref[0, hh, hd:QK_PAD, rs] = q_pad
        bound_ref[0, :, rs] = jnp.broadcast_to(bound, (1, IN_PROJ_ROWS))

        for hh in range(MLA_HEADS):
            vt_ref[0, hh, :, rs] = v_t[hh * V_HEAD:(hh + 1) * V_HEAD].astype(BF16)
        k1, k2 = rope_t(_rms(kr_t, 0) * gkr)
        kpe_t = jnp.concatenate([k1, k2, jnp.where(pad_row == 0, 1.0, 0.0)], axis=0)
        kpe = kpe_t.T.astype(BF16)
        for hh in range(MLA_HEADS):
            kn = kn_all[:, hh * QK_NOPE:(hh + 1) * QK_NOPE]
            k_ref[0, hh, rs, 0:QK_NOPE] = (_rms(kn, -1) * gkn).astype(BF16)
            k_ref[0, hh, rs, QK_NOPE:QK_PAD] = kpe

        xr_ref[0, rs, :] = proj(0, d_rnn).astype(BF16)
        gr = proj(d_rnn, 2 * d_rnn)
        sgr_ref[0, rs, :] = _silu(gr).astype(BF16)
        ga_t = lax.dot_general(wt_ref[ga0 + d_att // 2:ga0 + d_att, :], h, NT_DIMS,
                               preferred_element_type=F32)
        sga_ref[0, d_att // 2:d_att, rs] = _silu(ga_t).astype(BF16)


def _in_proj(x, mod, pos_row, inv_col, w, tm):
    bsz, seq, d = x.shape
    hd = QK_NOPE + QK_ROPE
    q_scale = (hd ** -0.5) * math.log2(math.e)
    consts = [w["wt"],
              w["qan"], w["kvan"], w["wuqt"], w["wuk"], w["wuvt"],
              w["gqn"], w["gqr"], w["gkn"], w["gkr"]]
    d_rnn, d_att = w["d_rnn"], MLA_HEADS * V_HEAD
    return pl.pallas_call(
        functools.partial(_in_proj_kernel, q_scale=q_scale),
        grid=(bsz, seq // tm),
        in_specs=[
            pl.BlockSpec((1, tm, d), lambda b, i: (b, i, 0)),
            pl.BlockSpec((1, 3, d), lambda b, i: (b, 0, 0)),
            pl.BlockSpec((1, 1, tm), lambda b, i: (b, 0, i)),
            _const_spec(inv_col.shape),
        ] + [_const_spec(a.shape) for a in consts],
        out_specs=[
            pl.BlockSpec((1, tm, d_rnn), lambda b, i: (b, i, 0)),
            pl.BlockSpec((1, tm, d_rnn), lambda b, i: (b, i, 0)),
            pl.BlockSpec((1, d_att, tm), lambda b, i: (b, 0, i)),
            pl.BlockSpec((1, MLA_HEADS, QK_PAD, tm), lambda b, i: (b, 0, 0, i)),
            pl.BlockSpec((1, MLA_HEADS, tm, QK_PAD), lambda b, i: (b, 0, i, 0)),
            pl.BlockSpec((1, MLA_HEADS, V_HEAD, tm), lambda b, i: (b, 0, 0, i)),
            pl.BlockSpec((1, 1, tm), lambda b, i: (b, 0, i)),
        ],
        out_shape=[
            jax.ShapeDtypeStruct((bsz, seq, d_rnn), BF16),
            jax.ShapeDtypeStruct((bsz, seq, d_rnn), BF16),
            jax.ShapeDtypeStruct((bsz, d_att, seq), BF16),
            jax.ShapeDtypeStruct((bsz, MLA_HEADS, QK_PAD, seq), BF16),
            jax.ShapeDtypeStruct((bsz, MLA_HEADS, seq, QK_PAD), BF16),
            jax.ShapeDtypeStruct((bsz, MLA_HEADS, V_HEAD, seq), BF16),
            jax.ShapeDtypeStruct((bsz, 1, seq), F32),
        ],
        compiler_params=pltpu.CompilerParams(
            dimension_semantics=("arbitrary", "arbitrary"), vmem_limit_bytes=VMEM_LIMIT),
        name="in_proj",
    )(x, mod, pos_row, inv_col, *consts)


ATTN_HEADS_PER_STEP = 4
ATTN_KEY_SUB = 256
ATTN_BLOCKS_PER_TRIP = 4
MAX_SAFE_SCORE_BOUND = 56.0


def _attn_kernel(qt_ref, k_ref, vt_ref, sga_ref, o_ref, acc_ref, *, tq, running_max):
    qi = pl.program_id(2)
    nh = qt_ref.shape[1]

    sub = tq if running_max else ATTN_KEY_SUB

    def first_query(unit, masked):
        return unit[2] * sub if masked[unit[0]] else 0

    def scores(unit, js, masked):
        bi, hh, c = unit
        key0 = pl.multiple_of(js[bi] * tq + c * sub, sub)
        k_sub = k_ref[0, hh, pl.ds(key0, sub), :]
        q_t = qt_ref[0, hh, :, first_query(unit, masked):]
        return jnp.dot(k_sub, q_t, preferred_element_type=F32)

    def accumulate(unit, js, s_t, m, l, masked, init):
        bi, hh, c = unit
        q0 = first_query(unit, masked)
        assert not (init and q0), "the unit that initialises acc must cover every query"
        if masked[bi]:
            key = lax.broadcasted_iota(jnp.int32, s_t.shape, 0) + c * sub
            qry = lax.broadcasted_iota(jnp.int32, s_t.shape, 1) + q0
            s_t = jnp.where(key <= qry, s_t, -jnp.inf)
        key0 = pl.multiple_of(js[bi] * tq + c * sub, sub)
        v_sub = vt_ref[0, hh, :, pl.ds(key0, sub)]
        if running_max:
            m_new = jnp.maximum(m, jnp.max(s_t, axis=0, keepdims=True))
            alpha = jnp.exp2(m - m_new)
            p_t = jnp.exp2(s_t - m_new)
            l_new = alpha * l + jnp.sum(p_t, axis=0, keepdims=True)
            pv = jnp.dot(v_sub, p_t.astype(BF16), preferred_element_type=F32)
            acc_ref[hh] = pv if init else alpha * acc_ref[hh] + pv
        else:
            m_new = m
            p_t = jnp.exp2(s_t)
            l_part = l[:, q0:] + jnp.sum(p_t, axis=0, keepdims=True)
            l_new = l_part if q0 == 0 else jnp.concatenate([l[:, :q0], l_part], axis=1)
            pv = jnp.dot(v_sub, p_t.astype(BF16), preferred_element_type=F32)
            if init:
                acc_ref[hh] = pv
            else:
                acc_ref[hh, :, q0:] += pv
        return m_new, l_new

    def blocks(js, masked, carry, first=False):
        units = [(bi, hh, c) for bi in range(len(js)) for hh in range(nh)
                 for c in range(tq // sub)]
        carry = list(carry)
        s_next = scores(units[0], js, masked)
        for idx, unit in enumerate(units):
            s_cur = s_next
            if idx + 1 < len(units):
                s_next = scores(units[idx + 1], js, masked)
            hh = unit[1]
            init = first and unit[0] == 0 and unit[2] == 0
            carry[hh] = accumulate(unit, js, s_cur, carry[hh][0], carry[hh][1], masked, init)
        return tuple(carry)

    carry = tuple((jnp.full((1, tq), -jnp.inf, F32), jnp.zeros((1, tq), F32)) for _ in range(nh))
    if running_max:
        carry = blocks([qi], [True], carry, first=True)
        carry = lax.fori_loop(0, qi, lambda j, c: blocks([j], [False], c), carry)
    else:
        g = ATTN_BLOCKS_PER_TRIP
        tails = [functools.partial(blocks, [qi - r + t for t in range(r + 1)], [False] * r + [True],
                                   first=True) for r in range(g)]
        carry = lax.switch(qi % g, tails, carry)
        if k_ref.shape[2] // tq > g:
            carry = lax.fori_loop(
                0, qi // g,
                lambda p, c: blocks([g * p + t for t in range(g)], [False] * g, c), carry)

    for hh in range(nh):
        o_t = acc_ref[hh] / carry[hh][1]
        rows = slice(hh * V_HEAD, (hh + 1) * V_HEAD)
        o_ref[0, rows, :] = (o_t * sga_ref[0, rows, :].astype(F32)).astype(BF16)


def _attention(q_t, k, v_t, sga, tq, running_max):
    bsz, nh, _, seq = q_t.shape
    hb = ATTN_HEADS_PER_STEP
    return pl.pallas_call(
        functools.partial(_attn_kernel, tq=tq, running_max=running_max),
        grid=(bsz, nh // hb, seq // tq),
        in_specs=[
            pl.BlockSpec((1, hb, QK_PAD, tq), lambda b, h, i: (b, h, 0, i)),
            pl.BlockSpec((1, hb, seq, QK_PAD), lambda b, h, i: (b, h, 0, 0)),
            pl.BlockSpec((1, hb, V_HEAD, seq), lambda b, h, i: (b, h, 0, 0)),
            pl.BlockSpec((1, hb * V_HEAD, tq), lambda b, h, i: (b, h, i)),
        ],
        out_specs=pl.BlockSpec((1, hb * V_HEAD, tq), lambda b, h, i: (b, h, i)),
        out_shape=jax.ShapeDtypeStruct((bsz, nh * V_HEAD, seq), BF16),
        scratch_shapes=[pltpu.VMEM((hb, V_HEAD, tq), F32)],
        compiler_params=pltpu.CompilerParams(
            dimension_semantics=("arbitrary", "arbitrary", "arbitrary"),
            vmem_limit_bytes=VMEM_LIMIT),
        name="mla_attention_online" if running_max else "mla_attention",
    )(q_t, k, v_t, sga)


SUBLANES = 8
LANES = 128


def _rglru_kernel(xr_ref, sgr_ref, cw_ref, cb_ref, wa_ref, ba_ref, wx_ref, bx_ref, lam_ref,
                  o_ref, tail_ref, a_ref, b_ref, hc_ref, *, ts):
    @pl.when(pl.program_id(1) == 0)
    def _():
        tail_ref[...] = jnp.zeros_like(tail_ref)
        hc_ref[...] = jnp.zeros_like(hc_ref)

    blk = MXU_DIM
    n_delay = CONV_WIDTH - 1
    r_i = lax.broadcasted_iota(jnp.int32, (n_delay * blk, blk), 0)
    c_i = lax.broadcasted_iota(jnp.int32, (n_delay * blk, blk), 1)
    delay_mat = jnp.where((r_i % blk) - c_i == r_i // blk + 1, 1.0, 0.0).astype(BF16)
    row8 = lax.broadcasted_iota(jnp.int32, tail_ref.shape, 0)
    tail = tail_ref[...]
    conv = []
    for b0 in range(0, ts, blk):
        xb = xr_ref[0, b0:b0 + blk, :]
        x0 = xb.astype(F32)
        delayed = jnp.dot(delay_mat, xb, preferred_element_type=F32)
        acc = cb_ref[...] + cw_ref[n_delay:n_delay + 1, :] * x0
        for d in range(1, CONV_WIDTH):
            xd = delayed[(d - 1) * blk:d * blk]
            head = jnp.where(row8 < d, pltpu.roll(tail, d, 0), xd[0:SUBLANES])
            xd = jnp.concatenate([head, xd[SUBLANES:]], axis=0)
            acc = acc + cw_ref[n_delay - d:n_delay - d + 1, :] * xd
        tail = x0[blk - SUBLANES:blk]
        conv.append(acc)
    tail_ref[...] = tail
    xc = jnp.concatenate(conv, axis=0)

    xcb = xc.astype(BF16)
    n_grp = wa_ref.shape[0]

    def gate(w_ref, bias_ref):
        parts = [jnp.dot(xcb[:, g * MXU_DIM:(g + 1) * MXU_DIM], w_ref[g],
                         preferred_element_type=F32) for g in range(n_grp)]
        z = jnp.concatenate(parts, axis=1) + bias_ref[...]
        return 0.5 * jnp.tanh(0.5 * z) + 0.5

    r = gate(wa_ref, ba_ref)
    i = gate(wx_ref, bx_ref)
    nl = -lam_ref[...]
    log_a_unit = (-LRU_C) * (jnp.maximum(nl, 0.0) + jnp.log1p(jnp.exp(-jnp.abs(nl))))
    a_all = jnp.exp2(r * (log_a_unit * math.log2(math.e)))
    t = jnp.tanh(r * log_a_unit)
    u = -2.0 * t
    coef = jnp.where(u > 0.0, u * lax.rsqrt(u * (1.0 - t)), 0.0)
    b_all = coef * (i * xc)
    n_slab = a_ref.shape[0]
    for s in range(n_slab):
        a_ref[s] = a_all[:, s * LANES:(s + 1) * LANES]
        b_ref[s] = b_all[:, s * LANES:(s + 1) * LANES]

    row = lax.broadcasted_iota(jnp.int32, (SUBLANES, LANES), 0)
    block_rows = SUBLANES * SUBLANES

    def scan_block(blk, carry):
        base = pl.multiple_of(blk * block_rows, block_rows)
        new_carry = []
        for s in range(n_slab):
            h_loc, a_run = [], []
            for j in range(SUBLANES):
                rows_j = pl.ds(base + j, SUBLANES, stride=SUBLANES)
                a = a_ref[s, rows_j, :]
                b = b_ref[s, rows_j, :]
                h_loc.append(b if j == 0 else a * h_loc[-1] + b)
                a_run.append(a if j == 0 else a * a_run[-1])
            pa, pb = a_run[-1], h_loc[-1]
            for dist in (1, 2, 4):
                keep = row >= dist
                pb = jnp.where(keep, pa * pltpu.roll(pb, dist, 0) + pb, pb)
                pa = jnp.where(keep, pa * pltpu.roll(pa, dist, 0), pa)
            after = pa * carry[s] + pb
            h_in = jnp.where(row == 0, carry[s], pltpu.roll(after, 1, 0))
            for j in range(SUBLANES):
                rows_j = pl.ds(base + j, SUBLANES, stride=SUBLANES)
                b_ref[s, rows_j, :] = a_run[j] * h_in + h_loc[j]
            new_carry.append(jnp.broadcast_to(after[SUBLANES - 1:SUBLANES, :], after.shape))
        return tuple(new_carry)

    carry = lax.fori_loop(0, ts // block_rows, scan_block,
                          tuple(hc_ref[s] for s in range(n_slab)), unroll=2)
    for s in range(n_slab):
        hc_ref[s] = carry[s]
    h_all = jnp.concatenate([b_ref[s] for s in range(n_slab)], axis=1)
    o_ref[0] = (h_all * sgr_ref[0].astype(F32)).astype(BF16)


def _rglru(xr, sgr, cw, cb, wa, ba, wx, bx, lam, ts):
    bsz, seq, c = xr.shape
    consts = [cw, cb, wa, ba, wx, bx, lam]
    return pl.pallas_call(
        functools.partial(_rglru_kernel, ts=ts),
        grid=(bsz, seq // ts),
        in_specs=[
            pl.BlockSpec((1, ts, c), lambda b, i: (b, i, 0)),
            pl.BlockSpec((1, ts, c), lambda b, i: (b, i, 0)),
        ] + [_const_spec(a.shape) for a in consts],
        out_specs=pl.BlockSpec((1, ts, c), lambda b, i: (b, i, 0)),
        out_shape=jax.ShapeDtypeStruct((bsz, seq, c), BF16),
        scratch_shapes=[
            pltpu.VMEM((SUBLANES, c), F32),
            pltpu.VMEM((c // LANES, ts, LANES), F32),
            pltpu.VMEM((c // LANES, ts, LANES), F32),
            pltpu.VMEM((c // LANES, SUBLANES, LANES), F32),
        ],
        compiler_params=pltpu.CompilerParams(
            dimension_semantics=("arbitrary", "arbitrary"), vmem_limit_bytes=VMEM_LIMIT),
        name="rglru",
    )(xr, sgr, *consts)


def _out_proj_kernel(x_ref, yr_ref, ya_ref, w32_ref, mod_ref, o_ref, w_ref):
    @pl.when((pl.program_id(0) == 0) & (pl.program_id(1) == 0))
    def _():
        w_ref[...] = w32_ref[...].astype(BF16)

    d_rnn = yr_ref.shape[2]
    y = jnp.dot(yr_ref[0], w_ref[0:d_rnn, :], preferred_element_type=F32)
    y = y + lax.dot_general(ya_ref[0], w_ref[d_rnn:, :], TN_DIMS,
                            preferred_element_type=F32)
    o_ref[0] = x_ref[0] + mod_ref[0, 2:3, :] * y


def _out_proj(x, y_rnn, y_att, w_o, mod, tm):
    bsz, seq, d = x.shape
    return pl.pallas_call(
        _out_proj_kernel,
        grid=(bsz, seq // tm),
        in_specs=[
            pl.BlockSpec((1, tm, d), lambda b, i: (b, i, 0)),
            pl.BlockSpec((1, tm, y_rnn.shape[2]), lambda b, i: (b, i, 0)),
            pl.BlockSpec((1, y_att.shape[1], tm), lambda b, i: (b, 0, i)),
            _const_spec(w_o.shape),
            pl.BlockSpec((1, 3, d), lambda b, i: (b, 0, 0)),
        ],
        out_specs=pl.BlockSpec((1, tm, d), lambda b, i: (b, i, 0)),
        out_shape=jax.ShapeDtypeStruct((bsz, seq, d), F32),
        scratch_shapes=[pltpu.VMEM(w_o.shape, BF16)],
        compiler_params=pltpu.CompilerParams(
            dimension_semantics=("arbitrary", "arbitrary"), vmem_limit_bytes=VMEM_LIMIT),
        name="out_proj",
    )(x, y_rnn, y_att, w_o, mod)


def _block_diag_groups(w):
    nb, n, _ = w.shape
    per = MXU_DIM // n
    w = w.reshape(nb // per, per, n, n)
    eye = jnp.eye(per, dtype=w.dtype)
    return jnp.einsum("gpij,pq->gpiqj", w, eye).reshape(nb // per, MXU_DIM, MXU_DIM)


def kernel(x, c, positions, w_ada, b_ada, w_in, conv_w, conv_b, w_rg_a, b_rg_a, w_rg_x, b_rg_x,
           lru_lambda, q_a_norm, w_uq, kv_a_norm, w_ukv, q_norm_nope, q_norm_rope, k_norm_nope,
           k_norm_rope, w_out):
    bsz, seq, d = x.shape
    depth = w_in.shape[0]
    d_rnn = conv_w.shape[2]
    q_lora = q_a_norm.shape[1]
    kv_lora = kv_a_norm.shape[1]
    tile = ROW_TILE

    inv_freq = 1.0 / (ROPE_THETA ** (jnp.arange(0, QK_ROPE, 2, dtype=F32) / QK_ROPE))
    pos = positions.astype(F32)
    pos_row = pos.reshape(bsz, 1, seq)
    inv_col = inv_freq.reshape(-1, 1)
    c_pad = jnp.zeros((SUBLANES, d), F32).at[:bsz].set(c)

    for l in range(depth):
        mod = _adaln(c_pad, w_ada[l], b_ada[l].reshape(1, -1))[:bsz].reshape(bsz, 3, d)

        ukv = w_ukv[l].reshape(kv_lora, MLA_HEADS, QK_NOPE + V_HEAD)
        w = dict(
            d_rnn=d_rnn, wt=jnp.swapaxes(w_in[l], 0, 1).astype(BF16),
            qan=q_a_norm[l].reshape(1, -1), kvan=kv_a_norm[l].reshape(1, -1),
            wuqt=w_uq[l].T.astype(BF16),
            wuk=ukv[:, :, :QK_NOPE].reshape(kv_lora, -1).astype(BF16),
            wuvt=ukv[:, :, QK_NOPE:].reshape(kv_lora, -1).T.astype(BF16),
            gqn=q_norm_nope[l].reshape(-1, 1), gqr=q_norm_rope[l].reshape(-1, 1),
            gkn=k_norm_nope[l].reshape(1, -1), gkr=k_norm_rope[l].reshape(-1, 1),
        )
        xr, sgr, sga, q_t, k, v_t, bound = _in_proj(x, mod, pos_row, inv_col, w, tile)

        y_att = lax.cond(
            jnp.max(bound) < MAX_SAFE_SCORE_BOUND,
            functools.partial(_attention, tq=tile, running_max=False),
            functools.partial(_attention, tq=tile, running_max=True),
            q_t, k, v_t, sga)
        y_rnn = _rglru(
            xr, sgr, conv_w[l], conv_b[l].reshape(1, -1),
            _block_diag_groups(w_rg_a[l]).astype(BF16), b_rg_a[l].reshape(1, -1),
            _block_diag_groups(w_rg_x[l]).astype(BF16), b_rg_x[l].reshape(1, -1),
            lru_lambda[l].reshape(1, -1), RGLRU_ROW_TILE)

        x = _out_proj(x, y_rnn, y_att, w_out[l], mod, tile)
    return x
```

```python
import functools
import math

import jax
import jax.numpy as jnp
from jax import lax
from jax.experimental import pallas as pl
from jax.experimental.pallas import tpu as pltpu

CONV_WIDTH = 4
LRU_C = 8.0
MLA_HEADS = 8
QK_NOPE = 128
QK_ROPE = 64
V_HEAD = 128
ROPE_THETA = 10000.0
EPS = 1e-6

QK_PAD = 256
MXU_DIM = 256
VMEM_LIMIT = 56 * 1024 * 1024
ROW_TILE = 512
RGLRU_ROW_TILE = 1024
ADALN_COL_TILE = 768

F32 = jnp.float32
BF16 = jnp.bfloat16

NT_DIMS = (((1,), (1,)), ((), ()))
TN_DIMS = (((0,), (0,)), ((), ()))


def _const_spec(shape):
    nd = len(shape)
    return pl.BlockSpec(shape, lambda *_: (0,) * nd, pipeline_mode=pl.Buffered(1))


def _adaln_kernel(c_ref, w_ref, b_ref, o_ref):
    c = c_ref[...]
    c_act = (c * jax.nn.sigmoid(c)).astype(BF16)
    o_ref[...] = jnp.dot(c_act, w_ref[...].astype(BF16), preferred_element_type=F32) + b_ref[...]


def _adaln(c_pad, w_ada, b_ada):
    rows, d = c_pad.shape
    n = w_ada.shape[1]
    tn = ADALN_COL_TILE
    return pl.pallas_call(
        _adaln_kernel,
        grid=(n // tn,),
        in_specs=[
            pl.BlockSpec((rows, d), lambda j: (0, 0)),
            pl.BlockSpec((d, tn), lambda j: (0, j)),
            pl.BlockSpec((1, tn), lambda j: (0, j)),
        ],
        out_specs=pl.BlockSpec((rows, tn), lambda j: (0, j)),
        out_shape=jax.ShapeDtypeStruct((rows, n), F32),
        compiler_params=pltpu.CompilerParams(
            dimension_semantics=("arbitrary",), vmem_limit_bytes=VMEM_LIMIT),
        name="adaln_mod",
    )(c_pad, w_ada, b_ada)


IN_PROJ_ROWS = 256


def _rms(v, axis):
    return v * lax.rsqrt(jnp.mean(v * v, axis=axis, keepdims=True) + EPS)


def _silu(v):
    half = 0.5 * v
    return half * jnp.tanh(half) + half


def _in_proj_kernel(x_ref, mod_ref, posr_ref, invc_ref, wt_ref,
                    qan_ref, kvan_ref, wuqt_ref, wuk_ref, wuvt_ref,
                    gqn_ref, gqr_ref, gkn_ref, gkr_ref,
                    xr_ref, sgr_ref, sga_ref, qt_ref, k_ref, vt_ref, bound_ref, *, q_scale):
    d_rnn, d_att = xr_ref.shape[2], sga_ref.shape[1]
    q_lora, kv_lora = qan_ref.shape[1], kvan_ref.shape[1]
    qc0, kvc0 = 2 * d_rnn, 2 * d_rnn + q_lora
    kr0 = kvc0 + kv_lora
    ga0 = kr0 + QK_ROPE
    scale = mod_ref[0, 1:2, :]
    shift = mod_ref[0, 0:1, :]
    hd = QK_NOPE + QK_ROPE
    half = QK_ROPE // 2
    gkn, gkr = gkn_ref[...], gkr_ref[...]
    gqn, gqr = gqn_ref[...] * q_scale, gqr_ref[...] * q_scale
    k_bound = jnp.sqrt(QK_NOPE * jnp.max(gkn * gkn, axis=1, keepdims=True)
                       + QK_ROPE * jnp.max(gkr * gkr, axis=0, keepdims=True))
    q_bound = jnp.sqrt(QK_NOPE * jnp.max(gqn * gqn, axis=0, keepdims=True)
                       + QK_ROPE * jnp.max(gqr * gqr, axis=0, keepdims=True))
    bound = q_bound * k_bound
    pad_row = lax.broadcasted_iota(jnp.int32, (QK_PAD - hd, IN_PROJ_ROWS), 0)
    q_pad = jnp.where(pad_row == 0, -bound, 0.0).astype(BF16)

    chunks = [slice(r0, r0 + IN_PROJ_ROWS) for r0 in range(0, x_ref.shape[1], IN_PROJ_ROWS)]
    hs = [(_rms(x_ref[0, rs, :], -1) * (1.0 + scale) + shift).astype(BF16) for rs in chunks]
    for rs, h in zip(chunks, hs):

        def proj(lo, hi):
            return lax.dot_general(h, wt_ref[lo:hi, :], NT_DIMS, preferred_element_type=F32)

        latent = proj(qc0, kr0)
        qcn = (_rms(latent[:, :q_lora], -1) * qan_ref[...]).astype(BF16)
        kvcn = (_rms(latent[:, q_lora:], -1) * kvan_ref[...]).astype(BF16)
        kg_t = lax.dot_general(wt_ref[kr0:ga0 + d_att // 2, :], h, NT_DIMS,
                               preferred_element_type=F32)
        kr_t = kg_t[:QK_ROPE]
        sga_ref[0, 0:d_att // 2, rs] = _silu(kg_t[QK_ROPE:]).astype(BF16)
        q_t = lax.dot_general(wuqt_ref[...], qcn, NT_DIMS, preferred_element_type=F32)
        v_t = lax.dot_general(wuvt_ref[...], kvcn, NT_DIMS, preferred_element_type=F32)
        kn_all = jnp.dot(kvcn, wuk_ref[...], preferred_element_type=F32)

        ang_t = invc_ref[...] * posr_ref[0, :, rs]
        cos_t, sin_t = jnp.cos(ang_t), jnp.sin(ang_t)

        def rope_t(v):
            v1, v2 = v[:half], v[half:]
            return v1 * cos_t - v2 * sin_t, v1 * sin_t + v2 * cos_t

        for hh in range(MLA_HEADS):
            qn = _rms(q_t[hh * hd:hh * hd + QK_NOPE], 0) * gqn
            r1, r2 = rope_t(_rms(q_t[hh * hd + QK_NOPE:(hh + 1) * hd], 0) * gqr)
            qt_ref[0, hh, 0:QK_NOPE, rs] = qn.astype(BF16)
            qt_ref[0, hh, QK_NOPE:QK_NOPE + half, rs] = r1.astype(BF16)
            qt_ref[0, hh, QK_NOPE + half:hd, rs] = r2.astype(BF16)
            qt_ref[0, hh, hd:QK_PAD, rs] = q_pad
        bound_ref[0, :, rs] = jnp.broadcast_to(bound, (1, IN_PROJ_ROWS))

        for hh in range(MLA_HEADS):
            vt_ref[0, hh, :, rs] = v_t[hh * V_HEAD:(hh + 1) * V_HEAD].astype(BF16)
        k1, k2 = rope_t(_rms(kr_t, 0) * gkr)
        kpe_t = jnp.concatenate([k1, k2, jnp.where(pad_row == 0, 1.0, 0.0)], axis=0)
        kpe = kpe_t.T.astype(BF16)
        for hh in range(MLA_HEADS):
            kn = kn_all[:, hh * QK_NOPE:(hh + 1) * QK_NOPE]
            k_ref[0, hh, rs, 0:QK_NOPE] = (_rms(kn, -1) * gkn).astype(BF16)
            k_ref[0, hh, rs, QK_NOPE:QK_PAD] = kpe

        xr_ref[0, rs, :] = proj(0, d_rnn).astype(BF16)
        gr = proj(d_rnn, 2 * d_rnn)
        sgr_ref[0, rs, :] = _silu(gr).astype(BF16)
        ga_t = lax.dot_general(wt_ref[ga0 + d_att // 2:ga0 + d_att, :], h, NT_DIMS,
                               preferred_element_type=F32)
        sga_ref[0, d_att // 2:d_att, rs] = _silu(ga_t).astype(BF16)


def _in_proj(x, mod, pos_row, inv_col, w, tm):
    bsz, seq, d = x.shape
    hd = QK_NOPE + QK_ROPE
    q_scale = (hd ** -0.5) * math.log2(math.e)
    consts = [w["wt"],
              w["qan"], w["kvan"], w["wuqt"], w["wuk"], w["wuvt"],
              w["gqn"], w["gqr"], w["gkn"], w["gkr"]]
    d_rnn, d_att = w["d_rnn"], MLA_HEADS * V_HEAD
    return pl.pallas_call(
        functools.partial(_in_proj_kernel, q_scale=q_scale),
        grid=(bsz, seq // tm),
        in_specs=[
            pl.BlockSpec((1, tm, d), lambda b, i: (b, i, 0)),
            pl.BlockSpec((1, 3, d), lambda b, i: (b, 0, 0)),
            pl.BlockSpec((1, 1, tm), lambda b, i: (b, 0, i)),
            _const_spec(inv_col.shape),
        ] + [_const_spec(a.shape) for a in consts],
        out_specs=[
            pl.BlockSpec((1, tm, d_rnn), lambda b, i: (b, i, 0)),
            pl.BlockSpec((1, tm, d_rnn), lambda b, i: (b, i, 0)),
            pl.BlockSpec((1, d_att, tm), lambda b, i: (b, 0, i)),
            pl.BlockSpec((1, MLA_HEADS, QK_PAD, tm), lambda b, i: (b, 0, 0, i)),
            pl.BlockSpec((1, MLA_HEADS, tm, QK_PAD), lambda b, i: (b, 0, i, 0)),
            pl.BlockSpec((1, MLA_HEADS, V_HEAD, tm), lambda b, i: (b, 0, 0, i)),
            pl.BlockSpec((1, 1, tm), lambda b, i: (b, 0, i)),
        ],
        out_shape=[
            jax.ShapeDtypeStruct((bsz, seq, d_rnn), BF16),
            jax.ShapeDtypeStruct((bsz, seq, d_rnn), BF16),
            jax.ShapeDtypeStruct((bsz, d_att, seq), BF16),
            jax.ShapeDtypeStruct((bsz, MLA_HEADS, QK_PAD, seq), BF16),
            jax.ShapeDtypeStruct((bsz, MLA_HEADS, seq, QK_PAD), BF16),
            jax.ShapeDtypeStruct((bsz, MLA_HEADS, V_HEAD, seq), BF16),
            jax.ShapeDtypeStruct((bsz, 1, seq), F32),
        ],
        compiler_params=pltpu.CompilerParams(
            dimension_semantics=("arbitrary", "arbitrary"), vmem_limit_bytes=VMEM_LIMIT),
        name="in_proj",
    )(x, mod, pos_row, inv_col, *consts)


ATTN_HEADS_PER_STEP = 4
ATTN_KEY_SUB = 256
ATTN_BLOCKS_PER_TRIP = 4
MAX_SAFE_SCORE_BOUND = 56.0


def _attn_kernel(qt_ref, k_ref, vt_ref, sga_ref, o_ref, acc_ref, *, tq, running_max):
    qi = pl.program_id(2)
    nh = qt_ref.shape[1]

    sub = tq if running_max else ATTN_KEY_SUB

    def first_query(unit, masked):
        return unit[2] * sub if masked[unit[0]] else 0

    def scores(unit, js, masked):
        bi, hh, c = unit
        key0 = pl.multiple_of(js[bi] * tq + c * sub, sub)
        k_sub = k_ref[0, hh, pl.ds(key0, sub), :]
        q_t = qt_ref[0, hh, :, first_query(unit, masked):]
        return jnp.dot(k_sub, q_t, preferred_element_type=F32)

    def accumulate(unit, js, s_t, m, l, masked, init):
        bi, hh, c = unit
        q0 = first_query(unit, masked)
        assert not (init and q0), "the unit that initialises acc must cover every query"
        if masked[bi]:
            key = lax.broadcasted_iota(jnp.int32, s_t.shape, 0) + c * sub
            qry = lax.broadcasted_iota(jnp.int32, s_t.shape, 1) + q0
            s_t = jnp.where(key <= qry, s_t, -jnp.inf)
        key0 = pl.multiple_of(js[bi] * tq + c * sub, sub)
        v_sub = vt_ref[0, hh, :, pl.ds(key0, sub)]
        if running_max:
            m_new = jnp.maximum(m, jnp.max(s_t, axis=0, keepdims=True))
            alpha = jnp.exp2(m - m_new)
            p_t = jnp.exp2(s_t - m_new)
            l_new = alpha * l + jnp.sum(p_t, axis=0, keepdims=True)
            pv = jnp.dot(v_sub, p_t.astype(BF16), preferred_element_type=F32)
            acc_ref[hh] = pv if init else alpha * acc_ref[hh] + pv
        else:
            m_new = m
            p_t = jnp.exp2(s_t)
            l_part = l[:, q0:] + jnp.sum(p_t, axis=0, keepdims=True)
            l_new = l_part if q0 == 0 else jnp.concatenate([l[:, :q0], l_part], axis=1)
            pv = jnp.dot(v_sub, p_t.astype(BF16), preferred_element_type=F32)
            if init:
                acc_ref[hh] = pv
            else:
                acc_ref[hh, :, q0:] += pv
        return m_new, l_new

    def blocks(js, masked, carry, first=False):
        units = [(bi, hh, c) for bi in range(len(js)) for c in range(tq // sub)
                 for hh in range(nh)]
        carry = list(carry)
        s_next = scores(units[0], js, masked)
        for idx, unit in enumerate(units):
            s_cur = s_next
            if idx + 1 < len(units):
                s_next = scores(units[idx + 1], js, masked)
            hh = unit[1]
            init = first and unit[0] == 0 and unit[2] == 0
            carry[hh] = accumulate(unit, js, s_cur, carry[hh][0], carry[hh][1], masked, init)
        return tuple(carry)

    carry = tuple((jnp.full((1, tq), -jnp.inf, F32), jnp.zeros((1, tq), F32)) for _ in range(nh))
    if running_max:
        carry = blocks([qi], [True], carry, first=True)
        carry = lax.fori_loop(0, qi, lambda j, c: blocks([j], [False], c), carry)
    else:
        g = ATTN_BLOCKS_PER_TRIP
        tails = [functools.partial(blocks, [qi - r + t for t in range(r + 1)], [False] * r + [True],
                                   first=True) for r in range(g)]
        carry = lax.switch(qi % g, tails, carry)
        if k_ref.shape[2] // tq > g:
            carry = lax.fori_loop(
                0, qi // g,
                lambda p, c: blocks([g * p + t for t in range(g)], [False] * g, c), carry)

    for hh in range(nh):
        o_t = acc_ref[hh] / carry[hh][1]
        rows = slice(hh * V_HEAD, (hh + 1) * V_HEAD)
        o_ref[0, rows, :] = (o_t * sga_ref[0, rows, :].astype(F32)).astype(BF16)


def _attention(q_t, k, v_t, sga, tq, running_max):
    bsz, nh, _, seq = q_t.shape
    hb = ATTN_HEADS_PER_STEP
    return pl.pallas_call(
        functools.partial(_attn_kernel, tq=tq, running_max=running_max),
        grid=(bsz, nh // hb, seq // tq),
        in_specs=[
            pl.BlockSpec((1, hb, QK_PAD, tq), lambda b, h, i: (b, h, 0, i)),
            pl.BlockSpec((1, hb, seq, QK_PAD), lambda b, h, i: (b, h, 0, 0)),
            pl.BlockSpec((1, hb, V_HEAD, seq), lambda b, h, i: (b, h, 0, 0)),
            pl.BlockSpec((1, hb * V_HEAD, tq), lambda b, h, i: (b, h, i)),
        ],
        out_specs=pl.BlockSpec((1, hb * V_HEAD, tq), lambda b, h, i: (b, h, i)),
        out_shape=jax.ShapeDtypeStruct((bsz, nh * V_HEAD, seq), BF16),
        scratch_shapes=[pltpu.VMEM((hb, V_HEAD, tq), F32)],
        compiler_params=pltpu.CompilerParams(
            dimension_semantics=("arbitrary", "arbitrary", "arbitrary"),
            vmem_limit_bytes=VMEM_LIMIT),
        name="mla_attention_online" if running_max else "mla_attention",
    )(q_t, k, v_t, sga)


SUBLANES = 8
LANES = 128


def _rglru_kernel(xr_ref, sgr_ref, cw_ref, cb_ref, wa_ref, ba_ref, wx_ref, bx_ref, lam_ref,
                  o_ref, tail_ref, a_ref, b_ref, hc_ref, *, ts):
    @pl.when(pl.program_id(1) == 0)
    def _():
        tail_ref[...] = jnp.zeros_like(tail_ref)
        hc_ref[...] = jnp.zeros_like(hc_ref)

    blk = MXU_DIM
    n_delay = CONV_WIDTH - 1
    r_i = lax.broadcasted_iota(jnp.int32, (n_delay * blk, blk), 0)
    c_i = lax.broadcasted_iota(jnp.int32, (n_delay * blk, blk), 1)
    delay_mat = jnp.where((r_i % blk) - c_i == r_i // blk + 1, 1.0, 0.0).astype(BF16)
    row8 = lax.broadcasted_iota(jnp.int32, tail_ref.shape, 0)
    tail = tail_ref[...]
    conv = []
    for b0 in range(0, ts, blk):
        xb = xr_ref[0, b0:b0 + blk, :]
        x0 = xb.astype(F32)
        delayed = jnp.dot(delay_mat, xb, preferred_element_type=F32)
        acc = cb_ref[...] + cw_ref[n_delay:n_delay + 1, :] * x0
        for d in range(1, CONV_WIDTH):
            xd = delayed[(d - 1) * blk:d * blk]
            head = jnp.where(row8 < d, pltpu.roll(tail, d, 0), xd[0:SUBLANES])
            xd = jnp.concatenate([head, xd[SUBLANES:]], axis=0)
            acc = acc + cw_ref[n_delay - d:n_delay - d + 1, :] * xd
        tail = x0[blk - SUBLANES:blk]
        conv.append(acc)
    tail_ref[...] = tail
    xc = jnp.concatenate(conv, axis=0)

    xcb = xc.astype(BF16)
    n_grp = wa_ref.shape[0]

    def gate(w_ref, bias_ref):
        parts = [jnp.dot(xcb[:, g * MXU_DIM:(g + 1) * MXU_DIM], w_ref[g],
                         preferred_element_type=F32) for g in range(n_grp)]
        z = jnp.concatenate(parts, axis=1) + bias_ref[...]
        return 0.5 * jnp.tanh(0.5 * z) + 0.5

    r = gate(wa_ref, ba_ref)
    i = gate(wx_ref, bx_ref)
    nl = -lam_ref[...]
    log_a_unit = (-LRU_C) * (jnp.maximum(nl, 0.0) + jnp.log1p(jnp.exp(-jnp.abs(nl))))
    a_all = jnp.exp2(r * (log_a_unit * math.log2(math.e)))
    t = jnp.tanh(r * log_a_unit)
    u = -2.0 * t
    coef = jnp.where(u > 0.0, u * lax.rsqrt(u * (1.0 - t)), 0.0)
    b_all = coef * (i * xc)
    n_slab = a_ref.shape[0]
    for s in range(n_slab):
        a_ref[s] = a_all[:, s * LANES:(s + 1) * LANES]
        b_ref[s] = b_all[:, s * LANES:(s + 1) * LANES]

    row = lax.broadcasted_iota(jnp.int32, (SUBLANES, LANES), 0)
    block_rows = SUBLANES * SUBLANES

    def scan_block(blk, carry):
        base = pl.multiple_of(blk * block_rows, block_rows)
        new_carry = []
        for s in range(n_slab):
            h_loc, a_run = [], []
            for j in range(SUBLANES):
                rows_j = pl.ds(base + j, SUBLANES, stride=SUBLANES)
                a = a_ref[s, rows_j, :]
                b = b_ref[s, rows_j, :]
                h_loc.append(b if j == 0 else a * h_loc[-1] + b)
                a_run.append(a if j == 0 else a * a_run[-1])
            pa, pb = a_run[-1], h_loc[-1]
            for dist in (1, 2, 4):
                keep = row >= dist
                pb = jnp.where(keep, pa * pltpu.roll(pb, dist, 0) + pb, pb)
                pa = jnp.where(keep, pa * pltpu.roll(pa, dist, 0), pa)
            after = pa * carry[s] + pb
            h_in = jnp.where(row == 0, carry[s], pltpu.roll(after, 1, 0))
            for j in range(SUBLANES):
                rows_j = pl.ds(base + j, SUBLANES, stride=SUBLANES)
                b_ref[s, rows_j, :] = a_run[j] * h_in + h_loc[j]
            new_carry.append(jnp.broadcast_to(after[SUBLANES - 1:SUBLANES, :], after.shape))
        return tuple(new_carry)

    carry = lax.fori_loop(0, ts // block_rows, scan_block,
                          tuple(hc_ref[s] for s in range(n_slab)), unroll=2)
    for s in range(n_slab):
        hc_ref[s] = carry[s]
    h_all = jnp.concatenate([b_ref[s] for s in range(n_slab)], axis=1)
    o_ref[0] = (h_all * sgr_ref[0].astype(F32)).astype(BF16)


def _rglru(xr, sgr, cw, cb, wa, ba, wx, bx, lam, ts):
    bsz, seq, c = xr.shape
    consts = [cw, cb, wa, ba, wx, bx, lam]
    return pl.pallas_call(
        functools.partial(_rglru_kernel, ts=ts),
        grid=(bsz, seq // ts),
        in_specs=[
            pl.BlockSpec((1, ts, c), lambda b, i: (b, i, 0)),
            pl.BlockSpec((1, ts, c), lambda b, i: (b, i, 0)),
        ] + [_const_spec(a.shape) for a in consts],
        out_specs=pl.BlockSpec((1, ts, c), lambda b, i: (b, i, 0)),
        out_shape=jax.ShapeDtypeStruct((bsz, seq, c), BF16),
        scratch_shapes=[
            pltpu.VMEM((SUBLANES, c), F32),
            pltpu.VMEM((c // LANES, ts, LANES), F32),
            pltpu.VMEM((c // LANES, ts, LANES), F32),
            pltpu.VMEM((c // LANES, SUBLANES, LANES), F32),
        ],
        compiler_params=pltpu.CompilerParams(
            dimension_semantics=("arbitrary", "arbitrary"), vmem_limit_bytes=VMEM_LIMIT),
        name="rglru",
    )(xr, sgr, *consts)


def _out_proj_kernel(x_ref, yr_ref, ya_ref, w32_ref, mod_ref, o_ref, w_ref):
    @pl.when((pl.program_id(0) == 0) & (pl.program_id(1) == 0))
    def _():
        w_ref[...] = w32_ref[...].astype(BF16)

    d_rnn = yr_ref.shape[2]
    y = jnp.dot(yr_ref[0], w_ref[0:d_rnn, :], preferred_element_type=F32)
    y = y + lax.dot_general(ya_ref[0], w_ref[d_rnn:, :], TN_DIMS,
                            preferred_element_type=F32)
    o_ref[0] = x_ref[0] + mod_ref[0, 2:3, :] * y


def _out_proj(x, y_rnn, y_att, w_o, mod, tm):
    bsz, seq, d = x.shape
    return pl.pallas_call(
        _out_proj_kernel,
        grid=(bsz, seq // tm),
        in_specs=[
            pl.BlockSpec((1, tm, d), lambda b, i: (b, i, 0)),
            pl.BlockSpec((1, tm, y_rnn.shape[2]), lambda b, i: (b, i, 0)),
            pl.BlockSpec((1, y_att.shape[1], tm), lambda b, i: (b, 0, i)),
            _const_spec(w_o.shape),
            pl.BlockSpec((1, 3, d), lambda b, i: (b, 0, 0)),
        ],
        out_specs=pl.BlockSpec((1, tm, d), lambda b, i: (b, i, 0)),
        out_shape=jax.ShapeDtypeStruct((bsz, seq, d), F32),
        scratch_shapes=[pltpu.VMEM(w_o.shape, BF16)],
        compiler_params=pltpu.CompilerParams(
            dimension_semantics=("arbitrary", "arbitrary"), vmem_limit_bytes=VMEM_LIMIT),
        name="out_proj",
    )(x, y_rnn, y_att, w_o, mod)


def _block_diag_groups(w):
    nb, n, _ = w.shape
    per = MXU_DIM // n
    w = w.reshape(nb // per, per, n, n)
    eye = jnp.eye(per, dtype=w.dtype)
    return jnp.einsum("gpij,pq->gpiqj", w, eye).reshape(nb // per, MXU_DIM, MXU_DIM)


def kernel(x, c, positions, w_ada, b_ada, w_in, conv_w, conv_b, w_rg_a, b_rg_a, w_rg_x, b_rg_x,
           lru_lambda, q_a_norm, w_uq, kv_a_norm, w_ukv, q_norm_nope, q_norm_rope, k_norm_nope,
           k_norm_rope, w_out):
    bsz, seq, d = x.shape
    depth = w_in.shape[0]
    d_rnn = conv_w.shape[2]
    q_lora = q_a_norm.shape[1]
    kv_lora = kv_a_norm.shape[1]
    tile = ROW_TILE

    inv_freq = 1.0 / (ROPE_THETA ** (jnp.arange(0, QK_ROPE, 2, dtype=F32) / QK_ROPE))
    pos = positions.astype(F32)
    pos_row = pos.reshape(bsz, 1, seq)
    inv_col = inv_freq.reshape(-1, 1)
    c_pad = jnp.zeros((SUBLANES, d), F32).at[:bsz].set(c)

    for l in range(depth):
        mod = _adaln(c_pad, w_ada[l], b_ada[l].reshape(1, -1))[:bsz].reshape(bsz, 3, d)

        ukv = w_ukv[l].reshape(kv_lora, MLA_HEADS, QK_NOPE + V_HEAD)
        w = dict(
            d_rnn=d_rnn, wt=jnp.swapaxes(w_in[l], 0, 1).astype(BF16),
            qan=q_a_norm[l].reshape(1, -1), kvan=kv_a_norm[l].reshape(1, -1),
            wuqt=w_uq[l].T.astype(BF16),
            wuk=ukv[:, :, :QK_NOPE].reshape(kv_lora, -1).astype(BF16),
            wuvt=ukv[:, :, QK_NOPE:].reshape(kv_lora, -1).T.astype(BF16),
            gqn=q_norm_nope[l].reshape(-1, 1), gqr=q_norm_rope[l].reshape(-1, 1),
            gkn=k_norm_nope[l].reshape(1, -1), gkr=k_norm_rope[l].reshape(-1, 1),
        )
        xr, sgr, sga, q_t, k, v_t, bound = _in_proj(x, mod, pos_row, inv_col, w, tile)

        y_att = lax.cond(
            jnp.max(bound) < MAX_SAFE_SCORE_BOUND,
            functools.partial(_attention, tq=tile, running_max=False),
            functools.partial(_attention, tq=tile, running_max=True),
            q_t, k, v_t, sga)
        y_rnn = _rglru(
            xr, sgr, conv_w[l], conv_b[l].reshape(1, -1),
            _block_diag_groups(w_rg_a[l]).astype(BF16), b_rg_a[l].reshape(1, -1),
            _block_diag_groups(w_rg_x[l]).astype(BF16), b_rg_x[l].reshape(1, -1),
            lru_lambda[l].reshape(1, -1), RGLRU_ROW_TILE)

        x = _out_proj(x, y_rnn, y_att, w_out[l], mod, tile)
    return x
```

```python
import functools
import math

import jax
import jax.numpy as jnp
from jax import lax
from jax.experimental import pallas as pl
from jax.experimental.pallas import tpu as pltpu

CONV_WIDTH = 4
LRU_C = 8.0
MLA_HEADS = 8
QK_NOPE = 128
QK_ROPE = 64
V_HEAD = 128
ROPE_THETA = 10000.0
EPS = 1e-6

QK_PAD = 256
MXU_DIM = 256
VMEM_LIMIT = 56 * 1024 * 1024
ROW_TILE = 512
RGLRU_ROW_TILE = 1024
ADALN_COL_TILE = 768

F32 = jnp.float32
BF16 = jnp.bfloat16

NT_DIMS = (((1,), (1,)), ((), ()))
TN_DIMS = (((0,), (0,)), ((), ()))


def _const_spec(shape):
    nd = len(shape)
    return pl.BlockSpec(shape, lambda *_: (0,) * nd, pipeline_mode=pl.Buffered(1))


def _adaln_kernel(c_ref, w_ref, b_ref, o_ref):
    c = c_ref[...]
    c_act = (c * jax.nn.sigmoid(c)).astype(BF16)
    o_ref[...] = jnp.dot(c_act, w_ref[...].astype(BF16), preferred_element_type=F32) + b_ref[...]


def _adaln(c_pad, w_ada, b_ada):
    rows, d = c_pad.shape
    n = w_ada.shape[1]
    tn = ADALN_COL_TILE
    return pl.pallas_call(
        _adaln_kernel,
        grid=(n // tn,),
        in_specs=[
            pl.BlockSpec((rows, d), lambda j: (0, 0)),
            pl.BlockSpec((d, tn), lambda j: (0, j)),
            pl.BlockSpec((1, tn), lambda j: (0, j)),
        ],
        out_specs=pl.BlockSpec((rows, tn), lambda j: (0, j)),
        out_shape=jax.ShapeDtypeStruct((rows, n), F32),
        compiler_params=pltpu.CompilerParams(
            dimension_semantics=("arbitrary",), vmem_limit_bytes=VMEM_LIMIT),
        name="adaln_mod",
    )(c_pad, w_ada, b_ada)


IN_PROJ_ROWS = 256


def _rms(v, axis):
    return v * lax.rsqrt(jnp.mean(v * v, axis=axis, keepdims=True) + EPS)


def _silu(v):
    half = 0.5 * v
    return half * jnp.tanh(half) + half


def _in_proj_kernel(x_ref, mod_ref, posr_ref, invc_ref, wt_ref,
                    qan_ref, kvan_ref, wuqt_ref, wuk_ref, wuvt_ref,
                    gqn_ref, gqr_ref, gkn_ref, gkr_ref,
                    xr_ref, sgr_ref, sga_ref, qt_ref, k_ref, vt_ref, bound_ref, *, q_scale):
    d_rnn, d_att = xr_ref.shape[2], sga_ref.shape[1]
    q_lora, kv_lora = qan_ref.shape[1], kvan_ref.shape[1]
    qc0, kvc0 = 2 * d_rnn, 2 * d_rnn + q_lora
    kr0 = kvc0 + kv_lora
    ga0 = kr0 + QK_ROPE
    scale = mod_ref[0, 1:2, :]
    shift = mod_ref[0, 0:1, :]
    hd = QK_NOPE + QK_ROPE
    half = QK_ROPE // 2
    gkn, gkr = gkn_ref[...], gkr_ref[...]
    gqn, gqr = gqn_ref[...] * q_scale, gqr_ref[...] * q_scale
    k_bound = jnp.sqrt(QK_NOPE * jnp.max(gkn * gkn, axis=1, keepdims=True)
                       + QK_ROPE * jnp.max(gkr * gkr, axis=0, keepdims=True))
    q_bound = jnp.sqrt(QK_NOPE * jnp.max(gqn * gqn, axis=0, keepdims=True)
                       + QK_ROPE * jnp.max(gqr * gqr, axis=0, keepdims=True))
    bound = q_bound * k_bound
    pad_row = lax.broadcasted_iota(jnp.int32, (QK_PAD - hd, IN_PROJ_ROWS), 0)
    q_pad = jnp.where(pad_row == 0, -bound, 0.0).astype(BF16)

    chunks = [slice(r0, r0 + IN_PROJ_ROWS) for r0 in range(0, x_ref.shape[1], IN_PROJ_ROWS)]
    hs = [(_rms(x_ref[0, rs, :], -1) * (1.0 + scale) + shift).astype(BF16) for rs in chunks]
    for rs, h in zip(chunks, hs):

        def proj(lo, hi):
            return lax.dot_general(h, wt_ref[lo:hi, :], NT_DIMS, preferred_element_type=F32)

        latent = proj(qc0, kr0)
        qcn = (_rms(latent[:, :q_lora], -1) * qan_ref[...]).astype(BF16)
        kvcn = (_rms(latent[:, q_lora:], -1) * kvan_ref[...]).astype(BF16)
        kg_t = lax.dot_general(wt_ref[kr0:ga0 + d_att // 2, :], h, NT_DIMS,
                               preferred_element_type=F32)
        kr_t = kg_t[:QK_ROPE]
        sga_ref[0, 0:d_att // 2, rs] = _silu(kg_t[QK_ROPE:]).astype(BF16)
        q_t = lax.dot_general(wuqt_ref[...], qcn, NT_DIMS, preferred_element_type=F32)
        v_t = lax.dot_general(wuvt_ref[...], kvcn, NT_DIMS, preferred_element_type=F32)
        kn_all = jnp.dot(kvcn, wuk_ref[...], preferred_element_type=F32)

        ang_t = invc_ref[...] * posr_ref[0, :, rs]
        cos_t, sin_t = jnp.cos(ang_t), jnp.sin(ang_t)

        def rope_t(v):
            v1, v2 = v[:half], v[half:]
            return v1 * cos_t - v2 * sin_t, v1 * sin_t + v2 * cos_t

        for hh in range(MLA_HEADS):
            qn = _rms(q_t[hh * hd:hh * hd + QK_NOPE], 0) * gqn
            r1, r2 = rope_t(_rms(q_t[hh * hd + QK_NOPE:(hh + 1) * hd], 0) * gqr)
            qt_ref[0, hh, 0:QK_NOPE, rs] = qn.astype(BF16)
            qt_ref[0, hh, QK_NOPE:QK_NOPE + half, rs] = r1.astype(BF16)
            qt_ref[0, hh, QK_NOPE + half:hd, rs] = r2.astype(BF16)
            qt_ref[0, hh, hd:QK_PAD, rs] = q_pad
        bound_ref[0, :, rs] = jnp.broadcast_to(bound, (1, IN_PROJ_ROWS))

        for hh in range(MLA_HEADS):
            vt_ref[0, hh, :, rs] = v_t[hh * V_HEAD:(hh + 1) * V_HEAD].astype(BF16)
        k1, k2 = rope_t(_rms(kr_t, 0) * gkr)
        kpe_t = jnp.concatenate([k1, k2, jnp.where(pad_row == 0, 1.0, 0.0)], axis=0)
        kpe = kpe_t.T.astype(BF16)
        for hh in range(MLA_HEADS):
            kn = kn_all[:, hh * QK_NOPE:(hh + 1) * QK_NOPE]
            k_ref[0, hh, rs, 0:QK_NOPE] = (_rms(kn, -1) * gkn).astype(BF16)
            k_ref[0, hh, rs, QK_NOPE:QK_PAD] = kpe

        xr_ref[0, rs, :] = proj(0, d_rnn).astype(BF16)
        gr = proj(d_rnn, 2 * d_rnn)
        sgr_ref[0, rs, :] = _silu(gr).astype(BF16)
        ga_t = lax.dot_general(wt_ref[ga0 + d_att // 2:ga0 + d_att, :], h, NT_DIMS,
                               preferred_element_type=F32)
        sga_ref[0, d_att // 2:d_att, rs] = _silu(ga_t).astype(BF16)


def _in_proj(x, mod, pos_row, inv_col, w, tm):
    bsz, seq, d = x.shape
    hd = QK_NOPE + QK_ROPE
    q_scale = (hd ** -0.5) * math.log2(math.e)
    consts = [w["wt"],
              w["qan"], w["kvan"], w["wuqt"], w["wuk"], w["wuvt"],
              w["gqn"], w["gqr"], w["gkn"], w["gkr"]]
    d_rnn, d_att = w["d_rnn"], MLA_HEADS * V_HEAD
    return pl.pallas_call(
        functools.partial(_in_proj_kernel, q_scale=q_scale),
        grid=(bsz, seq // tm),
        in_specs=[
            pl.BlockSpec((1, tm, d), lambda b, i: (b, i, 0)),
            pl.BlockSpec((1, 3, d), lambda b, i: (b, 0, 0)),
            pl.BlockSpec((1, 1, tm), lambda b, i: (b, 0, i)),
            _const_spec(inv_col.shape),
        ] + [_const_spec(a.shape) for a in consts],
        out_specs=[
            pl.BlockSpec((1, tm, d_rnn), lambda b, i: (b, i, 0)),
            pl.BlockSpec((1, tm, d_rnn), lambda b, i: (b, i, 0)),
            pl.BlockSpec((1, d_att, tm), lambda b, i: (b, 0, i)),
            pl.BlockSpec((1, MLA_HEADS, QK_PAD, tm), lambda b, i: (b, 0, 0, i)),
            pl.BlockSpec((1, MLA_HEADS, tm, QK_PAD), lambda b, i: (b, 0, i, 0)),
            pl.BlockSpec((1, MLA_HEADS, V_HEAD, tm), lambda b, i: (b, 0, 0, i)),
            pl.BlockSpec((1, 1, tm), lambda b, i: (b, 0, i)),
        ],
        out_shape=[
            jax.ShapeDtypeStruct((bsz, seq, d_rnn), BF16),
            jax.ShapeDtypeStruct((bsz, seq, d_rnn), BF16),
            jax.ShapeDtypeStruct((bsz, d_att, seq), BF16),
            jax.ShapeDtypeStruct((bsz, MLA_HEADS, QK_PAD, seq), BF16),
            jax.ShapeDtypeStruct((bsz, MLA_HEADS, seq, QK_PAD), BF16),
            jax.ShapeDtypeStruct((bsz, MLA_HEADS, V_HEAD, seq), BF16),
            jax.ShapeDtypeStruct((bsz, 1, seq), F32),
        ],
        compiler_params=pltpu.CompilerParams(
            dimension_semantics=("arbitrary", "arbitrary"), vmem_limit_bytes=VMEM_LIMIT),
        name="in_proj",
    )(x, mod, pos_row, inv_col, *consts)


ATTN_HEADS_PER_STEP = 4
ATTN_KEY_SUB = 256
ATTN_SCORE_LOOKAHEAD = 2
ATTN_BLOCKS_PER_TRIP = 4
MAX_SAFE_SCORE_BOUND = 56.0


def _attn_kernel(qt_ref, k_ref, vt_ref, sga_ref, o_ref, acc_ref, *, tq, running_max):
    qi = pl.program_id(2)
    nh = qt_ref.shape[1]

    sub = tq if running_max else ATTN_KEY_SUB

    def first_query(unit, masked):
        return unit[2] * sub if masked[unit[0]] else 0

    def scores(unit, js, masked):
        bi, hh, c = unit
        key0 = pl.multiple_of(js[bi] * tq + c * sub, sub)
        k_sub = k_ref[0, hh, pl.ds(key0, sub), :]
        q_t = qt_ref[0, hh, :, first_query(unit, masked):]
        return jnp.dot(k_sub, q_t, preferred_element_type=F32)

    def accumulate(unit, js, s_t, m, l, masked, init):
        bi, hh, c = unit
        q0 = first_query(unit, masked)
        assert not (init and q0), "the unit that initialises acc must cover every query"
        if masked[bi]:
            key = lax.broadcasted_iota(jnp.int32, s_t.shape, 0) + c * sub
            qry = lax.broadcasted_iota(jnp.int32, s_t.shape, 1) + q0
            s_t = jnp.where(key <= qry, s_t, -jnp.inf)
        key0 = pl.multiple_of(js[bi] * tq + c * sub, sub)
        v_sub = vt_ref[0, hh, :, pl.ds(key0, sub)]
        if running_max:
            m_new = jnp.maximum(m, jnp.max(s_t, axis=0, keepdims=True))
            alpha = jnp.exp2(m - m_new)
            p_t = jnp.exp2(s_t - m_new)
            l_new = alpha * l + jnp.sum(p_t, axis=0, keepdims=True)
            pv = jnp.dot(v_sub, p_t.astype(BF16), preferred_element_type=F32)
            acc_ref[hh] = pv if init else alpha * acc_ref[hh] + pv
        else:
            m_new = m
            p_t = jnp.exp2(s_t)
            l_part = l[:, q0:] + jnp.sum(p_t, axis=0, keepdims=True)
            l_new = l_part if q0 == 0 else jnp.concatenate([l[:, :q0], l_part], axis=1)
            pv = jnp.dot(v_sub, p_t.astype(BF16), preferred_element_type=F32)
            if init:
                acc_ref[hh] = pv
            else:
                acc_ref[hh, :, q0:] += pv
        return m_new, l_new

    def blocks(js, masked, carry, first=False):
        units = [(bi, hh, c) for bi in range(len(js)) for c in range(tq // sub)
                 for hh in range(nh)]
        carry = list(carry)
        ahead = ATTN_SCORE_LOOKAHEAD
        queue = [scores(u, js, masked) for u in units[:ahead]]
        for idx, unit in enumerate(units):
            s_cur = queue.pop(0)
            if idx + ahead < len(units):
                queue.append(scores(units[idx + ahead], js, masked))
            hh = unit[1]
            init = first and unit[0] == 0 and unit[2] == 0
            carry[hh] = accumulate(unit, js, s_cur, carry[hh][0], carry[hh][1], masked, init)
        return tuple(carry)

    carry = tuple((jnp.full((1, tq), -jnp.inf, F32), jnp.zeros((1, tq), F32)) for _ in range(nh))
    if running_max:
        carry = blocks([qi], [True], carry, first=True)
        carry = lax.fori_loop(0, qi, lambda j, c: blocks([j], [False], c), carry)
    else:
        g = ATTN_BLOCKS_PER_TRIP
        tails = [functools.partial(blocks, [qi - r + t for t in range(r + 1)], [False] * r + [True],
                                   first=True) for r in range(g)]
        carry = lax.switch(qi % g, tails, carry)
        if k_ref.shape[2] // tq > g:
            carry = lax.fori_loop(
                0, qi // g,
                lambda p, c: blocks([g * p + t for t in range(g)], [False] * g, c), carry)

    for hh in range(nh):
        o_t = acc_ref[hh] / carry[hh][1]
        rows = slice(hh * V_HEAD, (hh + 1) * V_HEAD)
        o_ref[0, rows, :] = (o_t * sga_ref[0, rows, :].astype(F32)).astype(BF16)


def _attention(q_t, k, v_t, sga, tq, running_max):
    bsz, nh, _, seq = q_t.shape
    hb = ATTN_HEADS_PER_STEP
    return pl.pallas_call(
        functools.partial(_attn_kernel, tq=tq, running_max=running_max),
        grid=(bsz, nh // hb, seq // tq),
        in_specs=[
            pl.BlockSpec((1, hb, QK_PAD, tq), lambda b, h, i: (b, h, 0, i)),
            pl.BlockSpec((1, hb, seq, QK_PAD), lambda b, h, i: (b, h, 0, 0)),
            pl.BlockSpec((1, hb, V_HEAD, seq), lambda b, h, i: (b, h, 0, 0)),
            pl.BlockSpec((1, hb * V_HEAD, tq), lambda b, h, i: (b, h, i)),
        ],
        out_specs=pl.BlockSpec((1, hb * V_HEAD, tq), lambda b, h, i: (b, h, i)),
        out_shape=jax.ShapeDtypeStruct((bsz, nh * V_HEAD, seq), BF16),
        scratch_shapes=[pltpu.VMEM((hb, V_HEAD, tq), F32)],
        compiler_params=pltpu.CompilerParams(
            dimension_semantics=("arbitrary", "arbitrary", "arbitrary"),
            vmem_limit_bytes=VMEM_LIMIT),
        name="mla_attention_online" if running_max else "mla_attention",
    )(q_t, k, v_t, sga)


SUBLANES = 8
LANES = 128


def _rglru_kernel(xr_ref, sgr_ref, cw_ref, cb_ref, wa_ref, ba_ref, wx_ref, bx_ref, lam_ref,
                  o_ref, tail_ref, a_ref, b_ref, hc_ref, *, ts):
    @pl.when(pl.program_id(1) == 0)
    def _():
        tail_ref[...] = jnp.zeros_like(tail_ref)
        hc_ref[...] = jnp.zeros_like(hc_ref)

    blk = MXU_DIM
    n_delay = CONV_WIDTH - 1
    r_i = lax.broadcasted_iota(jnp.int32, (n_delay * blk, blk), 0)
    c_i = lax.broadcasted_iota(jnp.int32, (n_delay * blk, blk), 1)
    delay_mat = jnp.where((r_i % blk) - c_i == r_i // blk + 1, 1.0, 0.0).astype(BF16)
    row8 = lax.broadcasted_iota(jnp.int32, tail_ref.shape, 0)
    tail = tail_ref[...]
    conv = []
    for b0 in range(0, ts, blk):
        xb = xr_ref[0, b0:b0 + blk, :]
        x0 = xb.astype(F32)
        delayed = jnp.dot(delay_mat, xb, preferred_element_type=F32)
        acc = cb_ref[...] + cw_ref[n_delay:n_delay + 1, :] * x0
        for d in range(1, CONV_WIDTH):
            xd = delayed[(d - 1) * blk:d * blk]
            head = jnp.where(row8 < d, pltpu.roll(tail, d, 0), xd[0:SUBLANES])
            xd = jnp.concatenate([head, xd[SUBLANES:]], axis=0)
            acc = acc + cw_ref[n_delay - d:n_delay - d + 1, :] * xd
        tail = x0[blk - SUBLANES:blk]
        conv.append(acc)
    tail_ref[...] = tail
    xc = jnp.concatenate(conv, axis=0)

    xcb = xc.astype(BF16)
    n_grp = wa_ref.shape[0]

    def gate(w_ref, bias_ref):
        parts = [jnp.dot(xcb[:, g * MXU_DIM:(g + 1) * MXU_DIM], w_ref[g],
                         preferred_element_type=F32) for g in range(n_grp)]
        z = jnp.concatenate(parts, axis=1) + bias_ref[...]
        return 0.5 * jnp.tanh(0.5 * z) + 0.5

    r = gate(wa_ref, ba_ref)
    i = gate(wx_ref, bx_ref)
    nl = -lam_ref[...]
    log_a_unit = (-LRU_C) * (jnp.maximum(nl, 0.0) + jnp.log1p(jnp.exp(-jnp.abs(nl))))
    a_all = jnp.exp2(r * (log_a_unit * math.log2(math.e)))
    t = jnp.tanh(r * log_a_unit)
    u = -2.0 * t
    coef = jnp.where(u > 0.0, u * lax.rsqrt(u * (1.0 - t)), 0.0)
    b_all = coef * (i * xc)
    n_slab = a_ref.shape[0]
    for s in range(n_slab):
        a_ref[s] = a_all[:, s * LANES:(s + 1) * LANES]
        b_ref[s] = b_all[:, s * LANES:(s + 1) * LANES]

    row = lax.broadcasted_iota(jnp.int32, (SUBLANES, LANES), 0)
    block_rows = SUBLANES * SUBLANES

    def scan_block(blk, carry):
        base = pl.multiple_of(blk * block_rows, block_rows)
        new_carry = []
        for s in range(n_slab):
            h_loc, a_run = [], []
            for j in range(SUBLANES):
                rows_j = pl.ds(base + j, SUBLANES, stride=SUBLANES)
                a = a_ref[s, rows_j, :]
                b = b_ref[s, rows_j, :]
                h_loc.append(b if j == 0 else a * h_loc[-1] + b)
                a_run.append(a if j == 0 else a * a_run[-1])
            pa, pb = a_run[-1], h_loc[-1]
            for dist in (1, 2, 4):
                keep = row >= dist
                pb = jnp.where(keep, pa * pltpu.roll(pb, dist, 0) + pb, pb)
                pa = jnp.where(keep, pa * pltpu.roll(pa, dist, 0), pa)
            after = pa * carry[s] + pb
            h_in = jnp.where(row == 0, carry[s], pltpu.roll(after, 1, 0))
            for j in range(SUBLANES):
                rows_j = pl.ds(base + j, SUBLANES, stride=SUBLANES)
                b_ref[s, rows_j, :] = a_run[j] * h_in + h_loc[j]
            new_carry.append(jnp.broadcast_to(after[SUBLANES - 1:SUBLANES, :], after.shape))
        return tuple(new_carry)

    carry = lax.fori_loop(0, ts // block_rows, scan_block,
                          tuple(hc_ref[s] for s in range(n_slab)), unroll=2)
    for s in range(n_slab):
        hc_ref[s] = carry[s]
    h_all = jnp.concatenate([b_ref[s] for s in range(n_slab)], axis=1)
    o_ref[0] = (h_all * sgr_ref[0].astype(F32)).astype(BF16)


def _rglru(xr, sgr, cw, cb, wa, ba, wx, bx, lam, ts):
    bsz, seq, c = xr.shape
    consts = [cw, cb, wa, ba, wx, bx, lam]
    return pl.pallas_call(
        functools.partial(_rglru_kernel, ts=ts),
        grid=(bsz, seq // ts),
        in_specs=[
            pl.BlockSpec((1, ts, c), lambda b, i: (b, i, 0)),
            pl.BlockSpec((1, ts, c), lambda b, i: (b, i, 0)),
        ] + [_const_spec(a.shape) for a in consts],
        out_specs=pl.BlockSpec((1, ts, c), lambda b, i: (b, i, 0)),
        out_shape=jax.ShapeDtypeStruct((bsz, seq, c), BF16),
        scratch_shapes=[
            pltpu.VMEM((SUBLANES, c), F32),
            pltpu.VMEM((c // LANES, ts, LANES), F32),
            pltpu.VMEM((c // LANES, ts, LANES), F32),
            pltpu.VMEM((c // LANES, SUBLANES, LANES), F32),
        ],
        compiler_params=pltpu.CompilerParams(
            dimension_semantics=("arbitrary", "arbitrary"), vmem_limit_bytes=VMEM_LIMIT),
        name="rglru",
    )(xr, sgr, *consts)


def _out_proj_kernel(x_ref, yr_ref, ya_ref, w32_ref, mod_ref, o_ref, w_ref):
    @pl.when((pl.program_id(0) == 0) & (pl.program_id(1) == 0))
    def _():
        w_ref[...] = w32_ref[...].astype(BF16)

    d_rnn = yr_ref.shape[2]
    y = jnp.dot(yr_ref[0], w_ref[0:d_rnn, :], preferred_element_type=F32)
    y = y + lax.dot_general(ya_ref[0], w_ref[d_rnn:, :], TN_DIMS,
                            preferred_element_type=F32)
    o_ref[0] = x_ref[0] + mod_ref[0, 2:3, :] * y


def _out_proj(x, y_rnn, y_att, w_o, mod, tm):
    bsz, seq, d = x.shape
    return pl.pallas_call(
        _out_proj_kernel,
        grid=(bsz, seq // tm),
        in_specs=[
            pl.BlockSpec((1, tm, d), lambda b, i: (b, i, 0)),
            pl.BlockSpec((1, tm, y_rnn.shape[2]), lambda b, i: (b, i, 0)),
            pl.BlockSpec((1, y_att.shape[1], tm), lambda b, i: (b, 0, i)),
            _const_spec(w_o.shape),
            pl.BlockSpec((1, 3, d), lambda b, i: (b, 0, 0)),
        ],
        out_specs=pl.BlockSpec((1, tm, d), lambda b, i: (b, i, 0)),
        out_shape=jax.ShapeDtypeStruct((bsz, seq, d), F32),
        scratch_shapes=[pltpu.VMEM(w_o.shape, BF16)],
        compiler_params=pltpu.CompilerParams(
            dimension_semantics=("arbitrary", "arbitrary"), vmem_limit_bytes=VMEM_LIMIT),
        name="out_proj",
    )(x, y_rnn, y_att, w_o, mod)


def _block_diag_groups(w):
    nb, n, _ = w.shape
    per = MXU_DIM // n
    w = w.reshape(nb // per, per, n, n)
    eye = jnp.eye(per, dtype=w.dtype)
    return jnp.einsum("gpij,pq->gpiqj", w, eye).reshape(nb // per, MXU_DIM, MXU_DIM)


def kernel(x, c, positions, w_ada, b_ada, w_in, conv_w, conv_b, w_rg_a, b_rg_a, w_rg_x, b_rg_x,
           lru_lambda, q_a_norm, w_uq, kv_a_norm, w_ukv, q_norm_nope, q_norm_rope, k_norm_nope,
           k_norm_rope, w_out):
    bsz, seq, d = x.shape
    depth = w_in.shape[0]
    d_rnn = conv_w.shape[2]
    q_lora = q_a_norm.shape[1]
    kv_lora = kv_a_norm.shape[1]
    tile = ROW_TILE

    inv_freq = 1.0 / (ROPE_THETA ** (jnp.arange(0, QK_ROPE, 2, dtype=F32) / QK_ROPE))
    pos = positions.astype(F32)
    pos_row = pos.reshape(bsz, 1, seq)
    inv_col = inv_freq.reshape(-1, 1)
    c_pad = jnp.zeros((SUBLANES, d), F32).at[:bsz].set(c)

    for l in range(depth):
        mod = _adaln(c_pad, w_ada[l], b_ada[l].reshape(1, -1))[:bsz].reshape(bsz, 3, d)

        ukv = w_ukv[l].reshape(kv_lora, MLA_HEADS, QK_NOPE + V_HEAD)
        w = dict(
            d_rnn=d_rnn, wt=jnp.swapaxes(w_in[l], 0, 1).astype(BF16),
            qan=q_a_norm[l].reshape(1, -1), kvan=kv_a_norm[l].reshape(1, -1),
            wuqt=w_uq[l].T.astype(BF16),
            wuk=ukv[:, :, :QK_NOPE].reshape(kv_lora, -1).astype(BF16),
            wuvt=ukv[:, :, QK_NOPE:].reshape(kv_lora, -1).T.astype(BF16),
            gqn=q_norm_nope[l].reshape(-1, 1), gqr=q_norm_rope[l].reshape(-1, 1),
            gkn=k_norm_nope[l].reshape(1, -1), gkr=k_norm_rope[l].reshape(-1, 1),
        )
        xr, sgr, sga, q_t, k, v_t, bound = _in_proj(x, mod, pos_row, inv_col, w, tile)

        y_att = lax.cond(
            jnp.max(bound) < MAX_SAFE_SCORE_BOUND,
            functools.partial(_attention, tq=tile, running_max=False),
            functools.partial(_attention, tq=tile, running_max=True),
            q_t, k, v_t, sga)
        y_rnn = _rglru(
            xr, sgr, conv_w[l], conv_b[l].reshape(1, -1),
            _block_diag_groups(w_rg_a[l]).astype(BF16), b_rg_a[l].reshape(1, -1),
            _block_diag_groups(w_rg_x[l]).astype(BF16), b_rg_x[l].reshape(1, -1),
            lru_lambda[l].reshape(1, -1), RGLRU_ROW_TILE)

        x = _out_proj(x, y_rnn, y_att, w_out[l], mod, tile)
    return x
```

```python
import functools
import math

import jax
import jax.numpy as jnp
from jax import lax
from jax.experimental import pallas as pl
from jax.experimental.pallas import tpu as pltpu

CONV_WIDTH = 4
LRU_C = 8.0
MLA_HEADS = 8
QK_NOPE = 128
QK_ROPE = 64
V_HEAD = 128
ROPE_THETA = 10000.0
EPS = 1e-6

QK_PAD = 256
MXU_DIM = 256
VMEM_LIMIT = 56 * 1024 * 1024
ROW_TILE = 512
RGLRU_ROW_TILE = 1024
ADALN_COL_TILE = 768

F32 = jnp.float32
BF16 = jnp.bfloat16

NT_DIMS = (((1,), (1,)), ((), ()))
TN_DIMS = (((0,), (0,)), ((), ()))


def _const_spec(shape):
    nd = len(shape)
    return pl.BlockSpec(shape, lambda *_: (0,) * nd, pipeline_mode=pl.Buffered(1))


def _adaln_kernel(c_ref, w_ref, b_ref, o_ref):
    c = c_ref[...]
    c_act = (c * jax.nn.sigmoid(c)).astype(BF16)
    o_ref[...] = jnp.dot(c_act, w_ref[...].astype(BF16), preferred_element_type=F32) + b_ref[...]


def _adaln(c_pad, w_ada, b_ada):
    rows, d = c_pad.shape
    n = w_ada.shape[1]
    tn = ADALN_COL_TILE
    return pl.pallas_call(
        _adaln_kernel,
        grid=(n // tn,),
        in_specs=[
            pl.BlockSpec((rows, d), lambda j: (0, 0)),
            pl.BlockSpec((d, tn), lambda j: (0, j)),
            pl.BlockSpec((1, tn), lambda j: (0, j)),
        ],
        out_specs=pl.BlockSpec((rows, tn), lambda j: (0, j)),
        out_shape=jax.ShapeDtypeStruct((rows, n), F32),
        compiler_params=pltpu.CompilerParams(
            dimension_semantics=("arbitrary",), vmem_limit_bytes=VMEM_LIMIT),
        name="adaln_mod",
    )(c_pad, w_ada, b_ada)


IN_PROJ_ROWS = 256


def _rms(v, axis):
    return v * lax.rsqrt(jnp.mean(v * v, axis=axis, keepdims=True) + EPS)


def _silu(v):
    half = 0.5 * v
    return half * jnp.tanh(half) + half


def _in_proj_kernel(x_ref, mod_ref, posr_ref, invc_ref, wt_ref,
                    qan_ref, kvan_ref, wuqt_ref, wuk_ref, wuvt_ref,
                    gqn_ref, gqr_ref, gkn_ref, gkr_ref,
                    xr_ref, sgr_ref, sga_ref, qt_ref, k_ref, vt_ref, bound_ref, *, q_scale):
    d_rnn, d_att = xr_ref.shape[2], sga_ref.shape[1]
    q_lora, kv_lora = qan_ref.shape[1], kvan_ref.shape[1]
    qc0, kvc0 = 2 * d_rnn, 2 * d_rnn + q_lora
    kr0 = kvc0 + kv_lora
    ga0 = kr0 + QK_ROPE
    scale = mod_ref[0, 1:2, :]
    shift = mod_ref[0, 0:1, :]
    hd = QK_NOPE + QK_ROPE
    half = QK_ROPE // 2
    gkn, gkr = gkn_ref[...], gkr_ref[...]
    gqn, gqr = gqn_ref[...] * q_scale, gqr_ref[...] * q_scale
    k_bound = jnp.sqrt(QK_NOPE * jnp.max(gkn * gkn, axis=1, keepdims=True)
                       + QK_ROPE * jnp.max(gkr * gkr, axis=0, keepdims=True))
    q_bound = jnp.sqrt(QK_NOPE * jnp.max(gqn * gqn, axis=0, keepdims=True)
                       + QK_ROPE * jnp.max(gqr * gqr, axis=0, keepdims=True))
    bound = q_bound * k_bound
    pad_row = lax.broadcasted_iota(jnp.int32, (QK_PAD - hd, IN_PROJ_ROWS), 0)
    q_pad = jnp.where(pad_row == 0, -bound, 0.0).astype(BF16)

    chunks = [slice(r0, r0 + IN_PROJ_ROWS) for r0 in range(0, x_ref.shape[1], IN_PROJ_ROWS)]
    hs = [(_rms(x_ref[0, rs, :], -1) * (1.0 + scale) + shift).astype(BF16) for rs in chunks]
    for rs, h in zip(chunks, hs):

        def proj(lo, hi):
            return lax.dot_general(h, wt_ref[lo:hi, :], NT_DIMS, preferred_element_type=F32)

        latent = proj(qc0, kr0)
        qcn = (_rms(latent[:, :q_lora], -1) * qan_ref[...]).astype(BF16)
        kvcn = (_rms(latent[:, q_lora:], -1) * kvan_ref[...]).astype(BF16)
        kg_t = lax.dot_general(wt_ref[kr0:ga0 + d_att // 2, :], h, NT_DIMS,
                               preferred_element_type=F32)
        kr_t = kg_t[:QK_ROPE]
        sga_ref[0, 0:d_att // 2, rs] = _silu(kg_t[QK_ROPE:]).astype(BF16)
        q_t = lax.dot_general(wuqt_ref[...], qcn, NT_DIMS, preferred_element_type=F32)
        v_t = lax.dot_general(wuvt_ref[...], kvcn, NT_DIMS, preferred_element_type=F32)
        kn_all = jnp.dot(kvcn, wuk_ref[...], preferred_element_type=F32)

        ang_t = invc_ref[...] * posr_ref[0, :, rs]
        cos_t, sin_t = jnp.cos(ang_t), jnp.sin(ang_t)

        def rope_t(v):
            v1, v2 = v[:half], v[half:]
            return v1 * cos_t - v2 * sin_t, v1 * sin_t + v2 * cos_t

        for hh in range(MLA_HEADS):
            qn = _rms(q_t[hh * hd:hh * hd + QK_NOPE], 0) * gqn
            r1, r2 = rope_t(_rms(q_t[hh * hd + QK_NOPE:(hh + 1) * hd], 0) * gqr)
            qt_ref[0, hh, 0:QK_NOPE, rs] = qn.astype(BF16)
            qt_ref[0, hh, QK_NOPE:QK_NOPE + half, rs] = r1.astype(BF16)
            qt_ref[0, hh, QK_NOPE + half:hd, rs] = r2.astype(BF16)
            qt_ref[0, hh, hd:QK_PAD, rs] = q_pad
        bound_ref[0, :, rs] = jnp.broadcast_to(bound, (1, IN_PROJ_ROWS))

        for hh in range(MLA_HEADS):
            vt_ref[0, hh, :, rs] = v_t[hh * V_HEAD:(hh + 1) * V_HEAD].astype(BF16)
        k1, k2 = rope_t(_rms(kr_t, 0) * gkr)
        kpe_t = jnp.concatenate([k1, k2, jnp.where(pad_row == 0, 1.0, 0.0)], axis=0)
        kpe = kpe_t.T.astype(BF16)
        for hh in range(MLA_HEADS):
            kn = kn_all[:, hh * QK_NOPE:(hh + 1) * QK_NOPE]
            k_ref[0, hh, rs, 0:QK_NOPE] = (_rms(kn, -1) * gkn).astype(BF16)
            k_ref[0, hh, rs, QK_NOPE:QK_PAD] = kpe

        xr_ref[0, rs, :] = proj(0, d_rnn).astype(BF16)
        gr = proj(d_rnn, 2 * d_rnn)
        sgr_ref[0, rs, :] = _silu(gr).astype(BF16)
        ga_t = lax.dot_general(wt_ref[ga0 + d_att // 2:ga0 + d_att, :], h, NT_DIMS,
                               preferred_element_type=F32)
        sga_ref[0, d_att // 2:d_att, rs] = _silu(ga_t).astype(BF16)


def _in_proj(x, mod, pos_row, inv_col, w, tm):
    bsz, seq, d = x.shape
    hd = QK_NOPE + QK_ROPE
    q_scale = (hd ** -0.5) * math.log2(math.e)
    consts = [w["wt"],
              w["qan"], w["kvan"], w["wuqt"], w["wuk"], w["wuvt"],
              w["gqn"], w["gqr"], w["gkn"], w["gkr"]]
    d_rnn, d_att = w["d_rnn"], MLA_HEADS * V_HEAD
    return pl.pallas_call(
        functools.partial(_in_proj_kernel, q_scale=q_scale),
        grid=(bsz, seq // tm),
        in_specs=[
            pl.BlockSpec((1, tm, d), lambda b, i: (b, i, 0)),
            pl.BlockSpec((1, 3, d), lambda b, i: (b, 0, 0)),
            pl.BlockSpec((1, 1, tm), lambda b, i: (b, 0, i)),
            _const_spec(inv_col.shape),
        ] + [_const_spec(a.shape) for a in consts],
        out_specs=[
            pl.BlockSpec((1, tm, d_rnn), lambda b, i: (b, i, 0)),
            pl.BlockSpec((1, tm, d_rnn), lambda b, i: (b, i, 0)),
            pl.BlockSpec((1, d_att, tm), lambda b, i: (b, 0, i)),
            pl.BlockSpec((1, MLA_HEADS, QK_PAD, tm), lambda b, i: (b, 0, 0, i)),
            pl.BlockSpec((1, MLA_HEADS, tm, QK_PAD), lambda b, i: (b, 0, i, 0)),
            pl.BlockSpec((1, MLA_HEADS, V_HEAD, tm), lambda b, i: (b, 0, 0, i)),
            pl.BlockSpec((1, 1, tm), lambda b, i: (b, 0, i)),
        ],
        out_shape=[
            jax.ShapeDtypeStruct((bsz, seq, d_rnn), BF16),
            jax.ShapeDtypeStruct((bsz, seq, d_rnn), BF16),
            jax.ShapeDtypeStruct((bsz, d_att, seq), BF16),
            jax.ShapeDtypeStruct((bsz, MLA_HEADS, QK_PAD, seq), BF16),
            jax.ShapeDtypeStruct((bsz, MLA_HEADS, seq, QK_PAD), BF16),
            jax.ShapeDtypeStruct((bsz, MLA_HEADS, V_HEAD, seq), BF16),
            jax.ShapeDtypeStruct((bsz, 1, seq), F32),
        ],
        compiler_params=pltpu.CompilerParams(
            dimension_semantics=("arbitrary", "arbitrary"), vmem_limit_bytes=VMEM_LIMIT),
        name="in_proj",
    )(x, mod, pos_row, inv_col, *consts)


ATTN_HEADS_PER_STEP = 4
ATTN_KEY_SUB = 256
ATTN_SCORE_LOOKAHEAD = 3
ATTN_BLOCKS_PER_TRIP = 4
MAX_SAFE_SCORE_BOUND = 56.0


def _attn_kernel(qt_ref, k_ref, vt_ref, sga_ref, o_ref, acc_ref, *, tq, running_max):
    qi = pl.program_id(2)
    nh = qt_ref.shape[1]

    sub = tq if running_max else ATTN_KEY_SUB

    def first_query(unit, masked):
        return unit[2] * sub if masked[unit[0]] else 0

    def scores(unit, js, masked):
        bi, hh, c = unit
        key0 = pl.multiple_of(js[bi] * tq + c * sub, sub)
        k_sub = k_ref[0, hh, pl.ds(key0, sub), :]
        q_t = qt_ref[0, hh, :, first_query(unit, masked):]
        return jnp.dot(k_sub, q_t, preferred_element_type=F32)

    def accumulate(unit, js, s_t, m, l, masked, init):
        bi, hh, c = unit
        q0 = first_query(unit, masked)
        assert not (init and q0), "the unit that initialises acc must cover every query"
        if masked[bi]:
            key = lax.broadcasted_iota(jnp.int32, s_t.shape, 0) + c * sub
            qry = lax.broadcasted_iota(jnp.int32, s_t.shape, 1) + q0
            s_t = jnp.where(key <= qry, s_t, -jnp.inf)
        key0 = pl.multiple_of(js[bi] * tq + c * sub, sub)
        v_sub = vt_ref[0, hh, :, pl.ds(key0, sub)]
        if running_max:
            m_new = jnp.maximum(m, jnp.max(s_t, axis=0, keepdims=True))
            alpha = jnp.exp2(m - m_new)
            p_t = jnp.exp2(s_t - m_new)
            l_new = alpha * l + jnp.sum(p_t, axis=0, keepdims=True)
            pv = jnp.dot(v_sub, p_t.astype(BF16), preferred_element_type=F32)
            acc_ref[hh] = pv if init else alpha * acc_ref[hh] + pv
        else:
            m_new = m
            p_t = jnp.exp2(s_t)
            l_part = l[:, q0:] + jnp.sum(p_t, axis=0, keepdims=True)
            l_new = l_part if q0 == 0 else jnp.concatenate([l[:, :q0], l_part], axis=1)
            pv = jnp.dot(v_sub, p_t.astype(BF16), preferred_element_type=F32)
            if init:
                acc_ref[hh] = pv
            else:
                acc_ref[hh, :, q0:] += pv
        return m_new, l_new

    def blocks(js, masked, carry, first=False):
        units = [(bi, hh, c) for bi in range(len(js)) for c in range(tq // sub)
                 for hh in range(nh)]
        carry = list(carry)
        ahead = ATTN_SCORE_LOOKAHEAD
        queue = [scores(u, js, masked) for u in units[:ahead]]
        for idx, unit in enumerate(units):
            s_cur = queue.pop(0)
            if idx + ahead < len(units):
                queue.append(scores(units[idx + ahead], js, masked))
            hh = unit[1]
            init = first and unit[0] == 0 and unit[2] == 0
            carry[hh] = accumulate(unit, js, s_cur, carry[hh][0], carry[hh][1], masked, init)
        return tuple(carry)

    carry = tuple((jnp.full((1, tq), -jnp.inf, F32), jnp.zeros((1, tq), F32)) for _ in range(nh))
    if running_max:
        carry = blocks([qi], [True], carry, first=True)
        carry = lax.fori_loop(0, qi, lambda j, c: blocks([j], [False], c), carry)
    else:
        g = ATTN_BLOCKS_PER_TRIP
        tails = [functools.partial(blocks, [qi - r + t for t in range(r + 1)], [False] * r + [True],
                                   first=True) for r in range(g)]
        carry = lax.switch(qi % g, tails, carry)
        if k_ref.shape[2] // tq > g:
            carry = lax.fori_loop(
                0, qi // g,
                lambda p, c: blocks([g * p + t for t in range(g)], [False] * g, c), carry)

    for hh in range(nh):
        o_t = acc_ref[hh] / carry[hh][1]
        rows = slice(hh * V_HEAD, (hh + 1) * V_HEAD)
        o_ref[0, rows, :] = (o_t * sga_ref[0, rows, :].astype(F32)).astype(BF16)


def _attention(q_t, k, v_t, sga, tq, running_max):
    bsz, nh, _, seq = q_t.shape
    hb = ATTN_HEADS_PER_STEP
    return pl.pallas_call(
        functools.partial(_attn_kernel, tq=tq, running_max=running_max),
        grid=(bsz, nh // hb, seq // tq),
        in_specs=[
            pl.BlockSpec((1, hb, QK_PAD, tq), lambda b, h, i: (b, h, 0, i)),
            pl.BlockSpec((1, hb, seq, QK_PAD), lambda b, h, i: (b, h, 0, 0)),
            pl.BlockSpec((1, hb, V_HEAD, seq), lambda b, h, i: (b, h, 0, 0)),
            pl.BlockSpec((1, hb * V_HEAD, tq), lambda b, h, i: (b, h, i)),
        ],
        out_specs=pl.BlockSpec((1, hb * V_HEAD, tq), lambda b, h, i: (b, h, i)),
        out_shape=jax.ShapeDtypeStruct((bsz, nh * V_HEAD, seq), BF16),
        scratch_shapes=[pltpu.VMEM((hb, V_HEAD, tq), F32)],
        compiler_params=pltpu.CompilerParams(
            dimension_semantics=("arbitrary", "arbitrary", "arbitrary"),
            vmem_limit_bytes=VMEM_LIMIT),
        name="mla_attention_online" if running_max else "mla_attention",
    )(q_t, k, v_t, sga)


SUBLANES = 8
LANES = 128


def _rglru_kernel(xr_ref, sgr_ref, cw_ref, cb_ref, wa_ref, ba_ref, wx_ref, bx_ref, lam_ref,
                  o_ref, tail_ref, a_ref, b_ref, hc_ref, *, ts):
    @pl.when(pl.program_id(1) == 0)
    def _():
        tail_ref[...] = jnp.zeros_like(tail_ref)
        hc_ref[...] = jnp.zeros_like(hc_ref)

    blk = MXU_DIM
    n_delay = CONV_WIDTH - 1
    r_i = lax.broadcasted_iota(jnp.int32, (n_delay * blk, blk), 0)
    c_i = lax.broadcasted_iota(jnp.int32, (n_delay * blk, blk), 1)
    delay_mat = jnp.where((r_i % blk) - c_i == r_i // blk + 1, 1.0, 0.0).astype(BF16)
    row8 = lax.broadcasted_iota(jnp.int32, tail_ref.shape, 0)
    tail = tail_ref[...]
    conv = []
    for b0 in range(0, ts, blk):
        xb = xr_ref[0, b0:b0 + blk, :]
        x0 = xb.astype(F32)
        delayed = jnp.dot(delay_mat, xb, preferred_element_type=F32)
        acc = cb_ref[...] + cw_ref[n_delay:n_delay + 1, :] * x0
        for d in range(1, CONV_WIDTH):
            xd = delayed[(d - 1) * blk:d * blk]
            head = jnp.where(row8 < d, pltpu.roll(tail, d, 0), xd[0:SUBLANES])
            xd = jnp.concatenate([head, xd[SUBLANES:]], axis=0)
            acc = acc + cw_ref[n_delay - d:n_delay - d + 1, :] * xd
        tail = x0[blk - SUBLANES:blk]
        conv.append(acc)
    tail_ref[...] = tail
    xc = jnp.concatenate(conv, axis=0)

    xcb = xc.astype(BF16)
    n_grp = wa_ref.shape[0]

    def gate(w_ref, bias_ref):
        parts = [jnp.dot(xcb[:, g * MXU_DIM:(g + 1) * MXU_DIM], w_ref[g],
                         preferred_element_type=F32) for g in range(n_grp)]
        z = jnp.concatenate(parts, axis=1) + bias_ref[...]
        return 0.5 * jnp.tanh(0.5 * z) + 0.5

    r = gate(wa_ref, ba_ref)
    i = gate(wx_ref, bx_ref)
    nl = -lam_ref[...]
    log_a_unit = (-LRU_C) * (jnp.maximum(nl, 0.0) + jnp.log1p(jnp.exp(-jnp.abs(nl))))
    a_all = jnp.exp2(r * (log_a_unit * math.log2(math.e)))
    t = jnp.tanh(r * log_a_unit)
    u = -2.0 * t
    coef = jnp.where(u > 0.0, u * lax.rsqrt(u * (1.0 - t)), 0.0)
    b_all = coef * (i * xc)
    n_slab = a_ref.shape[0]
    for s in range(n_slab):
        a_ref[s] = a_all[:, s * LANES:(s + 1) * LANES]
        b_ref[s] = b_all[:, s * LANES:(s + 1) * LANES]

    row = lax.broadcasted_iota(jnp.int32, (SUBLANES, LANES), 0)
    block_rows = SUBLANES * SUBLANES

    def scan_block(blk, carry):
        base = pl.multiple_of(blk * block_rows, block_rows)
        new_carry = []
        for s in range(n_slab):
            h_loc, a_run = [], []
            for j in range(SUBLANES):
                rows_j = pl.ds(base + j, SUBLANES, stride=SUBLANES)
                a = a_ref[s, rows_j, :]
                b = b_ref[s, rows_j, :]
                h_loc.append(b if j == 0 else a * h_loc[-1] + b)
                a_run.append(a if j == 0 else a * a_run[-1])
            pa, pb = a_run[-1], h_loc[-1]
            for dist in (1, 2, 4):
                keep = row >= dist
                pb = jnp.where(keep, pa * pltpu.roll(pb, dist, 0) + pb, pb)
                pa = jnp.where(keep, pa * pltpu.roll(pa, dist, 0), pa)
            after = pa * carry[s] + pb
            h_in = jnp.where(row == 0, carry[s], pltpu.roll(after, 1, 0))
            for j in range(SUBLANES):
                rows_j = pl.ds(base + j, SUBLANES, stride=SUBLANES)
                b_ref[s, rows_j, :] = a_run[j] * h_in + h_loc[j]
            new_carry.append(jnp.broadcast_to(after[SUBLANES - 1:SUBLANES, :], after.shape))
        return tuple(new_carry)

    carry = lax.fori_loop(0, ts // block_rows, scan_block,
                          tuple(hc_ref[s] for s in range(n_slab)), unroll=2)
    for s in range(n_slab):
        hc_ref[s] = carry[s]
    h_all = jnp.concatenate([b_ref[s] for s in range(n_slab)], axis=1)
    o_ref[0] = (h_all * sgr_ref[0].astype(F32)).astype(BF16)


def _rglru(xr, sgr, cw, cb, wa, ba, wx, bx, lam, ts):
    bsz, seq, c = xr.shape
    consts = [cw, cb, wa, ba, wx, bx, lam]
    return pl.pallas_call(
        functools.partial(_rglru_kernel, ts=ts),
        grid=(bsz, seq // ts),
        in_specs=[
            pl.BlockSpec((1, ts, c), lambda b, i: (b, i, 0)),
            pl.BlockSpec((1, ts, c), lambda b, i: (b, i, 0)),
        ] + [_const_spec(a.shape) for a in consts],
        out_specs=pl.BlockSpec((1, ts, c), lambda b, i: (b, i, 0)),
        out_shape=jax.ShapeDtypeStruct((bsz, seq, c), BF16),
        scratch_shapes=[
            pltpu.VMEM((SUBLANES, c), F32),
            pltpu.VMEM((c // LANES, ts, LANES), F32),
            pltpu.VMEM((c // LANES, ts, LANES), F32),
            pltpu.VMEM((c // LANES, SUBLANES, LANES), F32),
        ],
        compiler_params=pltpu.CompilerParams(
            dimension_semantics=("arbitrary", "arbitrary"), vmem_limit_bytes=VMEM_LIMIT),
        name="rglru",
    )(xr, sgr, *consts)


def _out_proj_kernel(x_ref, yr_ref, ya_ref, w32_ref, mod_ref, o_ref, w_ref):
    @pl.when((pl.program_id(0) == 0) & (pl.program_id(1) == 0))
    def _():
        w_ref[...] = w32_ref[...].astype(BF16)

    d_rnn = yr_ref.shape[2]
    y = jnp.dot(yr_ref[0], w_ref[0:d_rnn, :], preferred_element_type=F32)
    y = y + lax.dot_general(ya_ref[0], w_ref[d_rnn:, :], TN_DIMS,
                            preferred_element_type=F32)
    o_ref[0] = x_ref[0] + mod_ref[0, 2:3, :] * y


def _out_proj(x, y_rnn, y_att, w_o, mod, tm):
    bsz, seq, d = x.shape
    return pl.pallas_call(
        _out_proj_kernel,
        grid=(bsz, seq // tm),
        in_specs=[
            pl.BlockSpec((1, tm, d), lambda b, i: (b, i, 0)),
            pl.BlockSpec((1, tm, y_rnn.shape[2]), lambda b, i: (b, i, 0)),
            pl.BlockSpec((1, y_att.shape[1], tm), lambda b, i: (b, 0, i)),
            _const_spec(w_o.shape),
            pl.BlockSpec((1, 3, d), lambda b, i: (b, 0, 0)),
        ],
        out_specs=pl.BlockSpec((1, tm, d), lambda b, i: (b, i, 0)),
        out_shape=jax.ShapeDtypeStruct((bsz, seq, d), F32),
        scratch_shapes=[pltpu.VMEM(w_o.shape, BF16)],
        compiler_params=pltpu.CompilerParams(
            dimension_semantics=("arbitrary", "arbitrary"), vmem_limit_bytes=VMEM_LIMIT),
        name="out_proj",
    )(x, y_rnn, y_att, w_o, mod)


def _block_diag_groups(w):
    nb, n, _ = w.shape
    per = MXU_DIM // n
    w = w.reshape(nb // per, per, n, n)
    eye = jnp.eye(per, dtype=w.dtype)
    return jnp.einsum("gpij,pq->gpiqj", w, eye).reshape(nb // per, MXU_DIM, MXU_DIM)


def kernel(x, c, positions, w_ada, b_ada, w_in, conv_w, conv_b, w_rg_a, b_rg_a, w_rg_x, b_rg_x,
           lru_lambda, q_a_norm, w_uq, kv_a_norm, w_ukv, q_norm_nope, q_norm_rope, k_norm_nope,
           k_norm_rope, w_out):
    bsz, seq, d = x.shape
    depth = w_in.shape[0]
    d_rnn = conv_w.shape[2]
    q_lora = q_a_norm.shape[1]
    kv_lora = kv_a_norm.shape[1]
    tile = ROW_TILE

    inv_freq = 1.0 / (ROPE_THETA ** (jnp.arange(0, QK_ROPE, 2, dtype=F32) / QK_ROPE))
    pos = positions.astype(F32)
    pos_row = pos.reshape(bsz, 1, seq)
    inv_col = inv_freq.reshape(-1, 1)
    c_pad = jnp.zeros((SUBLANES, d), F32).at[:bsz].set(c)

    for l in range(depth):
        mod = _adaln(c_pad, w_ada[l], b_ada[l].reshape(1, -1))[:bsz].reshape(bsz, 3, d)

        ukv = w_ukv[l].reshape(kv_lora, MLA_HEADS, QK_NOPE + V_HEAD)
        w = dict(
            d_rnn=d_rnn, wt=jnp.swapaxes(w_in[l], 0, 1).astype(BF16),
            qan=q_a_norm[l].reshape(1, -1), kvan=kv_a_norm[l].reshape(1, -1),
            wuqt=w_uq[l].T.astype(BF16),
            wuk=ukv[:, :, :QK_NOPE].reshape(kv_lora, -1).astype(BF16),
            wuvt=ukv[:, :, QK_NOPE:].reshape(kv_lora, -1).T.astype(BF16),
            gqn=q_norm_nope[l].reshape(-1, 1), gqr=q_norm_rope[l].reshape(-1, 1),
            gkn=k_norm_nope[l].reshape(1, -1), gkr=k_norm_rope[l].reshape(-1, 1),
        )
        xr, sgr, sga, q_t, k, v_t, bound = _in_proj(x, mod, pos_row, inv_col, w, tile)

        y_att = lax.cond(
            jnp.max(bound) < MAX_SAFE_SCORE_BOUND,
            functools.partial(_attention, tq=tile, running_max=False),
            functools.partial(_attention, tq=tile, running_max=True),
            q_t, k, v_t, sga)
        y_rnn = _rglru(
            xr, sgr, conv_w[l], conv_b[l].reshape(1, -1),
            _block_diag_groups(w_rg_a[l]).astype(BF16), b_rg_a[l].reshape(1, -1),
            _block_diag_groups(w_rg_x[l]).astype(BF16), b_rg_x[l].reshape(1, -1),
            lru_lambda[l].reshape(1, -1), RGLRU_ROW_TILE)

        x = _out_proj(x, y_rnn, y_att, w_out[l], mod, tile)
    return x
```
